```python
import jax, jax.numpy as jnp
from jax import lax
import numpy as np

D_MODEL = 4096
BATCH = 2
SEQ = 4096
DEPTH = 2

N_META = 16
N_BRANCH = 4
MIX_W = D_MODEL // 4
ML_HEADS = 4
ML_HD = MIX_W // ML_HEADS
ML_CHUNK = 64
SB_HEADS = 8
SB_HD = MIX_W // SB_HEADS
SB_BLOCK = 128
HG_HEADS = 8
HG_HD = MIX_W // HG_HEADS
HG_CHUNK = 64
RG_BLOCKS = 16
RG_BD = MIX_W // RG_BLOCKS
RG_CONV = 4
RG_C = 8.0
D_FF = -(-8 * D_MODEL // (3 * 256)) * 256
EPS = 1e-6
NEG = -1e30

IN_SIZES = (MIX_W, MIX_W, MIX_W, MIX_W, ML_HEADS, ML_HEADS,
            MIX_W, MIX_W, MIX_W,
            MIX_W, MIX_W, MIX_W, MIX_W,
            MIX_W, MIX_W,
            N_BRANCH * D_MODEL)
N_IN = sum(IN_SIZES)
ML_F_OFF = 4 * MIX_W + ML_HEADS

kernel_name = "hybrid_mlstm_stickbreak_hgrn2_rglru_block"


def rmsnorm(x, g):
    xf = x.astype(jnp.float32)
    y = xf * lax.rsqrt(jnp.mean(xf * xf, axis=-1, keepdims=True) + EPS)
    return (y * g.astype(jnp.float32)).astype(x.dtype)


def split_cols(h):
    offs, acc = [], 0
    for s in IN_SIZES[:-1]:
        acc += s
        offs.append(acc)
    return jnp.split(h, offs, axis=-1)


def pad_front(t, n):
    return jnp.pad(t, ((0, 0), (n, 0), (0, 0)))


def to_chunks(t, n_heads, c):
    B, Lp, W = t.shape
    return t.reshape(B, Lp // c, c, n_heads, W // n_heads).transpose(1, 0, 3, 2, 4)


def gate_chunks(g, c):
    B, Lp, H = g.shape
    return g.reshape(B, Lp // c, c, H).transpose(1, 0, 3, 2)


def from_chunks(o):
    nc, B, H, c, d = o.shape
    return o.transpose(1, 0, 3, 2, 4).reshape(B, nc * c, H, d)


def mlstm(q, k, v, o_pre, i_pre, f_pre, g_norm):
    f32 = jnp.float32
    B, L, _ = q.shape
    pad = ML_CHUNK - N_META
    Lp = L + pad
    valid = (jnp.arange(Lp) >= pad)[None, :, None]
    qc = to_chunks(pad_front(q.astype(f32) * ML_HD ** -0.5, pad), ML_HEADS, ML_CHUNK)
    kc = to_chunks(pad_front(k.astype(f32), pad), ML_HEADS, ML_CHUNK)
    vc = to_chunks(pad_front(v.astype(f32), pad), ML_HEADS, ML_CHUNK)
    log_i = gate_chunks(jnp.where(valid, pad_front(i_pre.astype(f32), pad), NEG), ML_CHUNK)
    log_f = gate_chunks(jnp.where(valid, jax.nn.log_sigmoid(pad_front(f_pre.astype(f32), pad)), 0.0), ML_CHUNK)
    causal = jnp.tril(jnp.ones((ML_CHUNK, ML_CHUNK), dtype=bool))

    def step(carry, inp):
        C, n, m = carry
        qb, kb, vb, li, lf = inp
        b = jnp.cumsum(lf, axis=-1)
        D = jnp.where(causal, b[..., :, None] - b[..., None, :] + li[..., None, :], NEG)
        g = b + m[..., None]
        m_out = jnp.maximum(g, jnp.max(D, axis=-1))
        s = jnp.einsum('bhtd,bhsd->bhts', qb, kb) * jnp.exp(D - m_out[..., None])
        inter = jnp.exp(g - m_out)
        num = jnp.einsum('bhts,bhse->bhte', s, vb) + inter[..., None] * jnp.einsum('bhtd,bhde->bhte', qb, C)
        den = jnp.sum(s, axis=-1) + inter * jnp.einsum('bhtd,bhd->bht', qb, n)
        h = num / jnp.maximum(jnp.abs(den), jnp.exp(-m_out))[..., None]
        b_end = b[..., -1]
        wlog = b_end[..., None] - b + li
        m_new = jnp.maximum(b_end + m, jnp.max(wlog, axis=-1))
        w = jnp.exp(wlog - m_new[..., None])
        decay = jnp.exp(b_end + m - m_new)
        C_new = decay[..., None, None] * C + jnp.einsum('bhs,bhsd,bhse->bhde', w, kb, vb)
        n_new = decay[..., None] * n + jnp.einsum('bhs,bhsd->bhd', w, kb)
        return (C_new, n_new, m_new), h

    init = (jnp.zeros((B, ML_HEADS, ML_HD, ML_HD), f32),
            jnp.zeros((B, ML_HEADS, ML_HD), f32),
            jnp.zeros((B, ML_HEADS), f32))
    _, hs = lax.scan(step, init, (qc, kc, vc, log_i, log_f))
    h = rmsnorm(from_chunks(hs)[:, pad:], g_norm)
    return (h.reshape(B, L, MIX_W) * jax.nn.sigmoid(o_pre.astype(f32))).astype(q.dtype)


def stick_breaking(q, k, v, gq, gk):
    f32 = jnp.float32
    B, L, _ = q.shape
    pad = SB_BLOCK - N_META
    Lp = L + pad
    nb = Lp // SB_BLOCK

    def heads(t):
        return pad_front(t.astype(f32), pad).reshape(B, Lp, SB_HEADS, SB_HD).transpose(0, 2, 1, 3)

    qh = rmsnorm(heads(q), gq)
    kh = rmsnorm(heads(k), gk)
    vh = heads(v)
    qb = qh.reshape(B, SB_HEADS, nb, SB_BLOCK, SB_HD).transpose(2, 0, 1, 3, 4)
    key_pos = jnp.arange(Lp)

    def block(args):
        q_blk, blk = args
        z = jnp.einsum('bhtd,bhsd->bhts', q_blk, kh) * SB_HD ** -0.5
        t_pos = blk * SB_BLOCK + jnp.arange(SB_BLOCK)
        vis = (key_pos[None, :] < t_pos[:, None]) & (key_pos[None, :] >= pad)
        log_keep = jnp.where(vis, jax.nn.log_sigmoid(-z), 0.0)
        log_between = lax.cumsum(log_keep, axis=3, reverse=True) - log_keep
        a = jnp.where(vis, jnp.exp(jax.nn.log_sigmoid(z) + log_between), 0.0)
        return jnp.einsum('bhts,bhsd->bhtd', a, vh)

    o = lax.map(block, (qb, jnp.arange(nb)))
    o = o.transpose(1, 0, 3, 2, 4).reshape(B, Lp, MIX_W)[:, pad:]
    return o.astype(q.dtype)


def hgrn2(q, f_pre, i_in, g_pre, lb, g_norm):
    f32 = jnp.float32
    B, L, _ = q.shape
    pad = HG_CHUNK - N_META
    Lp = L + pad
    valid = (jnp.arange(Lp) >= pad)[None, :, None]
    fp = f_pre.astype(f32)
    sig = jax.nn.sigmoid(fp)
    log_f = jnp.log(lb + (1.0 - lb) * sig)
    kk = (1.0 - lb) * (1.0 - sig)
    fc = to_chunks(jnp.where(valid, pad_front(log_f, pad), 0.0), HG_HEADS, HG_CHUNK)
    kc = to_chunks(jnp.where(valid, pad_front(kk, pad), 0.0), HG_HEADS, HG_CHUNK)
    qc = to_chunks(pad_front(jax.nn.silu(q.astype(f32)), pad), HG_HEADS, HG_CHUNK)
    ic = to_chunks(pad_front(i_in.astype(f32), pad), HG_HEADS, HG_CHUNK)
    causal = jnp.tril(jnp.ones((HG_CHUNK, HG_CHUNK), dtype=bool))

    def step(S, inp):
        qb, kb, ib, lf = inp
        b = jnp.cumsum(lf, axis=2)
        diff = b[:, :, :, None, :] - b[:, :, None, :, :]
        decay = jnp.exp(jnp.where(causal[:, :, None], diff, NEG))
        att = jnp.einsum('bhtd,bhtsd,bhsd->bhts', qb, decay, kb)
        o = jnp.einsum('bhts,bhse->bhte', att, ib) + jnp.einsum('bhtd,bhde->bhte', qb * jnp.exp(b), S)
        b_end = b[:, :, -1]
        S_new = jnp.exp(b_end)[..., None] * S + jnp.einsum('bhsd,bhse->bhde', kb * jnp.exp(b_end[:, :, None] - b), ib)
        return S_new, o

    S0 = jnp.zeros((B, HG_HEADS, HG_HD, HG_HD), f32)
    _, os_ = lax.scan(step, S0, (qc, kc, ic, fc))
    o = rmsnorm(from_chunks(os_)[:, pad:], g_norm).reshape(B, L, MIX_W)
    return (o * jax.nn.sigmoid(g_pre.astype(f32))).astype(q.dtype)


def _lin_combine(e1, e2):
    a1, b1 = e1
    a2, b2 = e2
    return a1 * a2, a2 * b1 + b2


def rglru(xb, yb, conv_w, conv_b, wa, ba, wx, bx, lam):
    f32 = jnp.float32
    B, L, W = xb.shape
    xc = lax.conv_general_dilated(xb.astype(f32), conv_w.astype(f32)[:, None, :],
                                  window_strides=(1,), padding=[(RG_CONV - 1, 0)],
                                  dimension_numbers=('NWC', 'WIO', 'NWC'),
                                  feature_group_count=W) + conv_b
    xblk = xc.reshape(B, L, RG_BLOCKS, RG_BD)
    r = jax.nn.sigmoid(jnp.einsum('blni,nij->blnj', xblk, wa).reshape(B, L, W) + ba)
    ig = jax.nn.sigmoid(jnp.einsum('blni,nij->blnj', xblk, wx).reshape(B, L, W) + bx)
    log_a = -RG_C * r * jax.nn.softplus(-lam)
    a = jnp.exp(log_a)
    u = jnp.sqrt(-jnp.expm1(2.0 * log_a)) * (ig * xc)
    _, hseq = lax.associative_scan(_lin_combine, (a, u), axis=1)
    return (hseq * jax.nn.gelu(yb.astype(f32))).astype(xb.dtype)


def setup_inputs(seed: int = 0) -> dict:
    key = jax.random.key(seed)
    ks = jax.random.split(key, 24)

    def nrm(k, shape, scale):
        return jax.random.normal(k, shape, jnp.float32) * scale

    x = nrm(ks[0], (BATCH, SEQ, D_MODEL), 1.0)
    meta = nrm(ks[1], (N_META, D_MODEL), 1.0)
    norm_mix = 1.0 + nrm(ks[2], (DEPTH, D_MODEL), 0.02)
    norm_ffn = 1.0 + nrm(ks[3], (DEPTH, D_MODEL), 0.02)
    w_in = nrm(ks[4], (DEPTH, D_MODEL, N_IN), D_MODEL ** -0.5)
    b_in = nrm(ks[5], (DEPTH, N_IN), 0.02)
    b_in = b_in.at[:, ML_F_OFF:ML_F_OFF + ML_HEADS].add(jnp.linspace(3.0, 6.0, ML_HEADS))
    ml_norm = 1.0 + nrm(ks[6], (DEPTH, ML_HEADS, ML_HD), 0.02)
    sb_q_norm = 1.0 + nrm(ks[7], (DEPTH, SB_HD), 0.02)
    sb_k_norm = 1.0 + nrm(ks[8], (DEPTH, SB_HD), 0.02)
    hg_lb = nrm(ks[9], (DEPTH, MIX_W), 0.1)
    hg_norm = 1.0 + nrm(ks[10], (DEPTH, HG_HEADS, HG_HD), 0.02)
    rg_conv_w = nrm(ks[11], (DEPTH, RG_CONV, MIX_W), RG_CONV ** -0.5)
    rg_conv_b = nrm(ks[12], (DEPTH, MIX_W), 0.02)
    rg_wa = nrm(ks[13], (DEPTH, RG_BLOCKS, RG_BD, RG_BD), RG_BD ** -0.5)
    rg_ba = nrm(ks[14], (DEPTH, MIX_W), 0.02)
    rg_wx = nrm(ks[15], (DEPTH, RG_BLOCKS, RG_BD, RG_BD), RG_BD ** -0.5)
    rg_bx = nrm(ks[16], (DEPTH, MIX_W), 0.02)
    a_pow = jax.random.uniform(ks[17], (DEPTH, MIX_W), jnp.float32, minval=0.9, maxval=0.999)
    p = a_pow ** (1.0 / RG_C)
    rg_lambda = jnp.log(p) - jnp.log1p(-p)
    w_up = nrm(ks[18], (DEPTH, N_BRANCH, MIX_W, D_MODEL), MIX_W ** -0.5)
    w_out = nrm(ks[19], (DEPTH, D_MODEL, D_MODEL), D_MODEL ** -0.5)
    w_ffn_gate = nrm(ks[20], (DEPTH, D_MODEL, D_FF), D_MODEL ** -0.5)
    w_ffn_up = nrm(ks[21], (DEPTH, D_MODEL, D_FF), D_MODEL ** -0.5)
    w_ffn_down = nrm(ks[22], (DEPTH, D_FF, D_MODEL), D_FF ** -0.5)
    return {"x": x, "meta": meta, "norm_mix": norm_mix, "norm_ffn": norm_ffn,
            "w_in": w_in, "b_in": b_in, "ml_norm": ml_norm,
            "sb_q_norm": sb_q_norm, "sb_k_norm": sb_k_norm,
            "hg_lb": hg_lb, "hg_norm": hg_norm,
            "rg_conv_w": rg_conv_w, "rg_conv_b": rg_conv_b, "rg_wa": rg_wa, "rg_ba": rg_ba,
            "rg_wx": rg_wx, "rg_bx": rg_bx, "rg_lambda": rg_lambda,
            "w_up": w_up, "w_out": w_out,
            "w_ffn_gate": w_ffn_gate, "w_ffn_up": w_ffn_up, "w_ffn_down": w_ffn_down}


def reference(x, meta, norm_mix, norm_ffn, w_in, b_in, ml_norm, sb_q_norm, sb_k_norm,
              hg_lb, hg_norm, rg_conv_w, rg_conv_b, rg_wa, rg_ba, rg_wx, rg_bx, rg_lambda,
              w_up, w_out, w_ffn_gate, w_ffn_up, w_ffn_down):
    B = x.shape[0]
    h = jnp.concatenate([jnp.broadcast_to(meta[None].astype(x.dtype), (B, N_META, D_MODEL)), x], axis=1)
    L = h.shape[1]
    p_lb = jax.nn.softmax(hg_lb.astype(jnp.float32), axis=0)
    lower_bounds = jnp.clip(jnp.cumsum(p_lb, axis=0) - p_lb[0:1], 0.0, 0.999)
    for l in range(DEPTH):
        xn = rmsnorm(h, norm_mix[l])
        proj = xn @ w_in[l] + b_in[l]
        (ml_q, ml_k, ml_v, ml_o, ml_i, ml_f, sb_q, sb_k, sb_v,
         hg_q, hg_f, hg_i, hg_g, rg_x, rg_y, gate_pre) = split_cols(proj)
        branches = (
            mlstm(ml_q, ml_k, ml_v, ml_o, ml_i, ml_f, ml_norm[l]),
            stick_breaking(sb_q, sb_k, sb_v, sb_q_norm[l], sb_k_norm[l]),
            hgrn2(hg_q, hg_f, hg_i, hg_g, lower_bounds[l], hg_norm[l]),
            rglru(rg_x, rg_y, rg_conv_w[l], rg_conv_b[l], rg_wa[l], rg_ba[l],
                  rg_wx[l], rg_bx[l], rg_lambda[l]),
        )
        gates = jax.nn.sigmoid(gate_pre.reshape(B, L, N_BRANCH, D_MODEL))
        merged = sum(gates[:, :, kb] * (branches[kb] @ w_up[l, kb]) for kb in range(N_BRANCH))
        h = h + merged @ w_out[l]
        hn = rmsnorm(h, norm_ffn[l])
        h = h + (jax.nn.silu(hn @ w_ffn_gate[l]) * (hn @ w_ffn_up[l])) @ w_ffn_down[l]
    return h[:, N_META:]
```

```python
import functools

import jax
import jax.numpy as jnp
from jax import lax
from jax.experimental import pallas as pl
from jax.experimental.pallas import tpu as pltpu

F32 = jnp.float32
BF16 = jnp.bfloat16

D_MODEL = 4096
N_META = 16
N_BRANCH = 4
MIX_W = D_MODEL // 4
ML_HEADS = 4
ML_HD = MIX_W // ML_HEADS
SB_HEADS = 8
SB_HD = MIX_W // SB_HEADS
HG_HEADS = 8
HG_HD = MIX_W // HG_HEADS
RG_BLOCKS = 16
RG_BD = MIX_W // RG_BLOCKS
RG_CONV = 4
RG_C = 8.0
EPS = 1e-6
NEG = -1e30

LANES = 128
PAD = LANES - N_META
ML_CHUNK = 128
SB_BLOCK = 128
HG_CHUNK = 64
HG_SUB = 8
RG_GROUP = 256
RG_T = 128
VMEM_LIMIT = 56 * 1024 * 1024

_MLQ, _MLK, _MLV, _MLO = 0, 8, 16, 24
_SBQ, _SBK, _SBV = 32, 40, 48
_HGQ, _HGF, _HGI, _HGG = 56, 64, 72, 80
_RGX, _RGY = 88, 96
N_MIX = 13 * MIX_W


def _sigmoid(x):
    return 1.0 / (1.0 + jnp.exp(-x))


def _log_sigmoid(x):
    return jnp.minimum(x, 0.0) - jnp.log1p(jnp.exp(-jnp.abs(x)))


def _split3(x):
    hi = x.astype(BF16)
    r1 = x - hi.astype(F32)
    mid = r1.astype(BF16)
    lo = (r1 - mid.astype(F32)).astype(BF16)
    return hi, mid, lo


def _dot(a, b):
    return jnp.dot(a, b, preferred_element_type=F32)


def _dot_nt(a, b):
    return lax.dot_general(a, b, (((1,), (1,)), ((), ())), preferred_element_type=F32)


def _dot_tn(a, b):
    return lax.dot_general(a, b, (((0,), (0,)), ((), ())), preferred_element_type=F32)


def _tri_left(x, tri):
    hi, mid, lo = _split3(x)
    return _dot(tri, hi) + _dot(tri, mid) + _dot(tri, lo)


def _tri_right(x, tri):
    hi, mid, lo = _split3(x)
    return _dot(hi, tri) + _dot(mid, tri) + _dot(lo, tri)


def _iota(shape, dim):
    return lax.broadcasted_iota(jnp.int32, shape, dim)


def _rmsnorm_kernel(x_ref, g_ref, o_ref):
    x = x_ref[...]
    y = x * lax.rsqrt(jnp.mean(x * x, axis=-1, keepdims=True) + EPS)
    o_ref[...] = (y * g_ref[...]).astype(o_ref.dtype)


def _rmsnorm(x, g, tm=256):
    m, d = x.shape
    return pl.pallas_call(
        _rmsnorm_kernel,
        out_shape=jax.ShapeDtypeStruct((m, d), BF16),
        grid=(m // tm,),
        in_specs=[pl.BlockSpec((tm, d), lambda i: (i, 0)),
                  pl.BlockSpec((1, d), lambda i: (0, 0))],
        out_specs=pl.BlockSpec((tm, d), lambda i: (i, 0)),
        compiler_params=pltpu.CompilerParams(dimension_semantics=("parallel",)),
        name="rmsnorm",
    )(x, g.reshape(1, d))


def _linear_bias_kernel(x_ref, w_ref, b_ref, o_ref):
    o_ref[...] = _dot(x_ref[...], w_ref[...]) + b_ref[...]


def _linear_bias(x, w, b, tm, tn, name):
    m, k = x.shape
    n = w.shape[1]
    return pl.pallas_call(
        _linear_bias_kernel,
        out_shape=jax.ShapeDtypeStruct((m, n), F32),
        grid=(m // tm, n // tn),
        in_specs=[pl.BlockSpec((tm, k), lambda i, j: (i, 0)),
                  pl.BlockSpec((k, tn), lambda i, j: (0, j)),
                  pl.BlockSpec((1, tn), lambda i, j: (0, j))],
        out_specs=pl.BlockSpec((tm, tn), lambda i, j: (i, j)),
        compiler_params=pltpu.CompilerParams(
            dimension_semantics=("parallel", "parallel"), vmem_limit_bytes=VMEM_LIMIT),
        name=name,
    )(x, w, b)


def _linear_res_kernel(x_ref, w_ref, r_ref, o_ref, *, nk):
    part = _dot(x_ref[...], w_ref[...])
    if nk == 1:
        o_ref[...] = r_ref[...] + part
    else:
        k = pl.program_id(2)

        @pl.when(k == 0)
        def _():
            o_ref[...] = r_ref[...] + part

        @pl.when(k > 0)
        def _():
            o_ref[...] += part


def _linear_res(x, w, res, tm, tn, tk, name):
    m, k = x.shape
    n = w.shape[1]
    nk = k // tk
    return pl.pallas_call(
        functools.partial(_linear_res_kernel, nk=nk),
        out_shape=jax.ShapeDtypeStruct((m, n), F32),
        grid=(m // tm, n // tn, nk),
        in_specs=[pl.BlockSpec((tm, tk), lambda i, j, kk: (i, kk)),
                  pl.BlockSpec((tk, tn), lambda i, j, kk: (kk, j)),
                  pl.BlockSpec((tm, tn), lambda i, j, kk: (i, j))],
        out_specs=pl.BlockSpec((tm, tn), lambda i, j, kk: (i, j)),
        compiler_params=pltpu.CompilerParams(
            dimension_semantics=("parallel", "parallel", "arbitrary"), vmem_limit_bytes=VMEM_LIMIT),
        name=name,
    )(x, w, res)


def _merge_kernel(xn_ref, b0_ref, b1_ref, b2_ref, b3_ref, wg_ref, bg_ref, wu_ref, o_ref, *, tn):
    xn = xn_ref[...]
    acc = None
    for kb, br_ref in enumerate((b0_ref, b1_ref, b2_ref, b3_ref)):
        g = _dot(xn, wg_ref[:, kb * tn:(kb + 1) * tn]) + bg_ref[:, kb * tn:(kb + 1) * tn]
        u = _dot(br_ref[...], wu_ref[kb])
        t = _sigmoid(g) * u
        acc = t if acc is None else acc + t
    o_ref[...] = acc.astype(o_ref.dtype)


def _merge(xn, branches, wg_r, bg_r, wu_r, tm, tn):
    m, d = xn.shape
    nj = wg_r.shape[0]
    br_spec = pl.BlockSpec((tm, MIX_W), lambda i, j: (i, 0))
    return pl.pallas_call(
        functools.partial(_merge_kernel, tn=tn),
        out_shape=jax.ShapeDtypeStruct((m, nj * tn), BF16),
        grid=(m // tm, nj),
        in_specs=[pl.BlockSpec((tm, d), lambda i, j: (i, 0)),
                  br_spec, br_spec, br_spec, br_spec,
                  pl.BlockSpec((None, d, N_BRANCH * tn), lambda i, j: (j, 0, 0)),
                  pl.BlockSpec((None, 1, N_BRANCH * tn), lambda i, j: (j, 0, 0)),
                  pl.BlockSpec((None, N_BRANCH, MIX_W, tn), lambda i, j: (j, 0, 0, 0))],
        out_specs=pl.BlockSpec((tm, tn), lambda i, j: (i, j)),
        compiler_params=pltpu.CompilerParams(
            dimension_semantics=("parallel", "parallel"), vmem_limit_bytes=VMEM_LIMIT),
        name="merge",
    )(xn, *branches, wg_r, bg_r, wu_r)


def _ffn_gu_kernel(x_ref, wg_ref, wu_ref, o_ref):
    x = x_ref[...]
    g = _dot(x, wg_ref[...])
    u = _dot(x, wu_ref[...])
    o_ref[...] = (g * _sigmoid(g) * u).astype(o_ref.dtype)


def _ffn_gate_up(x, wg, wu, tm, tn):
    m, k = x.shape
    n = wg.shape[1]
    return pl.pallas_call(
        _ffn_gu_kernel,
        out_shape=jax.ShapeDtypeStruct((m, n), BF16),
        grid=(m // tm, n // tn),
        in_specs=[pl.BlockSpec((tm, k), lambda i, j: (i, 0)),
                  pl.BlockSpec((k, tn), lambda i, j: (0, j)),
                  pl.BlockSpec((k, tn), lambda i, j: (0, j))],
        out_specs=pl.BlockSpec((tm, tn), lambda i, j: (i, j)),
        compiler_params=pltpu.CompilerParams(
            dimension_semantics=("parallel", "parallel"), vmem_limit_bytes=VMEM_LIMIT),
        name="ffn_gate_up",
    )(x, wg, wu)


def _mlstm_kernel(q_ref, k_ref, v_ref, o_ref, gc_ref, gr_ref, gn_ref, out_ref, c_ref, n_ref, m_ref):
    h = pl.program_id(1)
    c = pl.program_id(2)
    T = ML_CHUNK

    @pl.when(c == 0)
    def _():
        c_ref[...] = jnp.zeros_like(c_ref)
        n_ref[...] = jnp.zeros_like(n_ref)
        m_ref[...] = jnp.zeros_like(m_ref)

    q = q_ref[...] * (ML_HD ** -0.5)
    k = k_ref[...]
    v = v_ref[...]

    gc = gc_ref[...]
    lane = _iota(gc.shape, 1)
    i_col = jnp.sum(jnp.where(lane == h, gc, 0.0), axis=1, keepdims=True)
    f_col = jnp.sum(jnp.where(lane == ML_HEADS + h, gc, 0.0), axis=1, keepdims=True)
    gr = gr_ref[...]
    sub = _iota(gr.shape, 0)
    i_row = jnp.sum(jnp.where(sub == h, gr, 0.0), axis=0, keepdims=True)
    f_row = jnp.sum(jnp.where(sub == ML_HEADS + h, gr, 0.0), axis=0, keepdims=True)

    valid_col = (c * T + _iota((T, 1), 0)) >= PAD
    valid_row = (c * T + _iota((1, T), 1)) >= PAD
    li_col = jnp.where(valid_col, i_col, NEG)
    li_row = jnp.where(valid_row, i_row, NEG)
    lf_col = jnp.where(valid_col, _log_sigmoid(f_col), 0.0)
    lf_row = jnp.where(valid_row, _log_sigmoid(f_row), 0.0)

    tt = _iota((T, T), 0)
    ss = _iota((T, T), 1)
    causal = ss <= tt
    tril = jnp.where(causal, 1.0, 0.0).astype(BF16)
    triu = jnp.where(tt <= ss, 1.0, 0.0).astype(BF16)
    b_col = _tri_left(jnp.broadcast_to(lf_col, (T, LANES)), tril)[:, :1]
    b_row = _tri_right(jnp.broadcast_to(lf_row, (8, T)), triu)[:1, :]

    m_prev = m_ref[...]
    dmat = jnp.where(causal, b_col - b_row + li_row, NEG)
    g_col = b_col + m_prev
    m_out = jnp.maximum(g_col, jnp.max(dmat, axis=1, keepdims=True))
    qb = q.astype(BF16)
    kb = k.astype(BF16)
    vb = v.astype(BF16)
    s = _dot_nt(qb, kb) * jnp.exp(dmat - m_out)
    inter = jnp.exp(g_col - m_out)
    num = _dot(s.astype(BF16), vb) + inter * _dot(qb, c_ref[...].astype(BF16))
    den = jnp.sum(s, axis=1, keepdims=True) + inter * jnp.sum(q * n_ref[...], axis=1, keepdims=True)
    hh = num / jnp.maximum(jnp.abs(den), jnp.exp(-m_out))

    b_end = b_row[:, T - 1:T]
    wlog = b_end - b_col + li_col
    m_new = jnp.maximum(b_end + m_prev, jnp.max(wlog, axis=0, keepdims=True))
    w = jnp.exp(wlog - m_new)
    decay = jnp.exp(b_end + m_prev - m_new)
    kw = k * w
    c_ref[...] = decay * c_ref[...] + _dot(kw.T.astype(BF16), vb)
    n_ref[...] = decay * n_ref[...] + jnp.sum(kw, axis=0, keepdims=True)
    m_ref[...] = m_new

    y = hh * lax.rsqrt(jnp.mean(hh * hh, axis=-1, keepdims=True) + EPS) * gn_ref[...]
    out_ref[...] = (y * _sigmoid(o_ref[...])).astype(out_ref.dtype)


def _mlstm(proj, gates, gates_t, gnorm):
    b, lp, _ = proj.shape
    T = ML_CHUNK
    w = ML_HD // LANES

    def col(off):
        return pl.BlockSpec((None, T, ML_HD), lambda bi, hi, ci: (bi, ci, off // w + hi))

    return pl.pallas_call(
        _mlstm_kernel,
        out_shape=jax.ShapeDtypeStruct((b, lp, MIX_W), BF16),
        grid=(b, ML_HEADS, lp // T),
        in_specs=[col(_MLQ), col(_MLK), col(_MLV), col(_MLO),
                  pl.BlockSpec((None, T, LANES), lambda bi, hi, ci: (bi, ci, 0)),
                  pl.BlockSpec((None, 8, T), lambda bi, hi, ci: (bi, 0, ci)),
                  pl.BlockSpec((None, 1, ML_HD), lambda bi, hi, ci: (hi, 0, 0))],
        out_specs=pl.BlockSpec((None, T, ML_HD), lambda bi, hi, ci: (bi, ci, hi)),
        scratch_shapes=[pltpu.VMEM((ML_HD, ML_HD), F32),
                        pltpu.VMEM((1, ML_HD), F32),
                        pltpu.VMEM((1, 1), F32)],
        compiler_params=pltpu.CompilerParams(dimension_semantics=("parallel", "parallel", "arbitrary")),
        name="mlstm",
    )(proj, proj, proj, proj, gates, gates_t, gnorm.reshape(ML_HEADS, 1, ML_HD))


def _sb_kernel(q_ref, k_ref, v_ref, gq_ref, gk_ref, out_ref, kn_ref, vb_ref):
    qi = pl.program_id(2)
    T = SB_BLOCK
    row0 = pl.multiple_of(qi * T, T)

    kblk = k_ref[pl.ds(row0, T), :]
    kn = kblk * lax.rsqrt(jnp.mean(kblk * kblk, axis=-1, keepdims=True) + EPS) * gk_ref[...]
    kn_ref[pl.ds(row0, T), :] = kn.astype(BF16)
    vb_ref[pl.ds(row0, T), :] = v_ref[pl.ds(row0, T), :].astype(BF16)

    q = q_ref[...]
    qn = (q * lax.rsqrt(jnp.mean(q * q, axis=-1, keepdims=True) + EPS) * gq_ref[...]).astype(BF16)

    jj = _iota((T, T), 0)
    ll = _iota((T, T), 1)
    tstrict = jnp.where(jj > ll, 1.0, 0.0).astype(BF16)
    t_pos = row0 + jj

    def body(it, carry):
        run, acc = carry
        j = qi - it
        col0 = pl.multiple_of(j * T, T)
        ks = kn_ref[pl.ds(col0, T), :]
        vs = vb_ref[pl.ds(col0, T), :]
        z = _dot_nt(qn, ks) * (SB_HD ** -0.5)
        ls = _log_sigmoid(z)
        s_pos = col0 + ll
        vis = jnp.logical_and(s_pos < t_pos, s_pos >= PAD)
        lk = jnp.where(vis, ls - z, 0.0)
        between = _tri_right(lk, tstrict) + run
        a = jnp.where(vis, jnp.exp(ls + between), 0.0)
        acc = acc + _dot(a.astype(BF16), vs)
        run = run + jnp.sum(lk, axis=1, keepdims=True)
        return run, acc

    run0 = jnp.zeros((T, 1), F32)
    acc0 = jnp.zeros((T, SB_HD), F32)
    _, acc = lax.fori_loop(0, qi + 1, body, (run0, acc0))
    out_ref[...] = acc.astype(out_ref.dtype)


def _stick_breaking(proj, gq, gk):
    b, lp, _ = proj.shape
    T = SB_BLOCK
    return pl.pallas_call(
        _sb_kernel,
        out_shape=jax.ShapeDtypeStruct((b, lp, MIX_W), BF16),
        grid=(b, SB_HEADS, lp // T),
        in_specs=[pl.BlockSpec((None, T, SB_HD), lambda bi, hi, qi: (bi, qi, _SBQ + hi)),
                  pl.BlockSpec((None, lp, SB_HD), lambda bi, hi, qi: (bi, 0, _SBK + hi)),
                  pl.BlockSpec((None, lp, SB_HD), lambda bi, hi, qi: (bi, 0, _SBV + hi)),
                  pl.BlockSpec((1, SB_HD), lambda bi, hi, qi: (0, 0)),
                  pl.BlockSpec((1, SB_HD), lambda bi, hi, qi: (0, 0))],
        out_specs=pl.BlockSpec((None, T, SB_HD), lambda bi, hi, qi: (bi, qi, hi)),
        scratch_shapes=[pltpu.VMEM((lp, SB_HD), BF16), pltpu.VMEM((lp, SB_HD), BF16)],
        compiler_params=pltpu.CompilerParams(dimension_semantics=("parallel", "parallel", "arbitrary")),
        name="stick_breaking",
    )(proj, proj, proj, gq.reshape(1, SB_HD), gk.reshape(1, SB_HD))


def _hgrn2_kernel(q_ref, f_ref, i_ref, g_ref, lb_ref, gn_ref, out_ref, st_ref):
    c = pl.program_id(2)
    T = HG_CHUNK
    S8 = HG_SUB

    @pl.when(c == 0)
    def _():
        st_ref[...] = jnp.zeros_like(st_ref)

    lb = lb_ref[...]
    sig = _sigmoid(f_ref[...])
    valid = (c * T + _iota((T, 1), 0)) >= PAD
    logf = jnp.where(valid, jnp.log(lb + (1.0 - lb) * sig), 0.0)
    kk = jnp.where(valid, (1.0 - lb) * (1.0 - sig), 0.0)
    qr = q_ref[...]
    q = qr * _sigmoid(qr)
    iv = i_ref[...].astype(BF16)

    tt = _iota((T, T), 0)
    ss = _iota((T, T), 1)
    tril = jnp.where(ss <= tt, 1.0, 0.0).astype(BF16)
    b = _tri_left(logf, tril)

    lane8 = _iota((S8, T), 1)
    row8 = _iota((S8, T), 0)
    blocks = []
    for blk in range(T // S8):
        r0 = blk * S8
        qb = q[r0:r0 + S8]
        bb = b[r0:r0 + S8]
        diag = jnp.zeros((S8, T), F32)
        for sl in range(S8):
            s_idx = r0 + sl
            p = qb * kk[s_idx:s_idx + 1] * jnp.exp(jnp.minimum(bb - b[s_idx:s_idx + 1], 0.0))
            diag = jnp.where(lane8 == s_idx, jnp.sum(p, axis=1, keepdims=True), diag)
        diag = jnp.where(lane8 - r0 <= row8, diag, 0.0)
        if blk == 0:
            blocks.append(diag)
            continue
        bref = b[r0 - 1:r0]
        qs = (qb * jnp.exp(bb - bref)).astype(BF16)
        ks = (kk * jnp.exp(jnp.minimum(bref - b, 0.0))).astype(BF16)
        off = _dot_nt(qs, ks)
        blocks.append(jnp.where(lane8 < r0, off, diag))
    att = jnp.concatenate(blocks, axis=0)

    st = st_ref[...]
    o = _dot(att.astype(BF16), iv) + _dot_nt((q * jnp.exp(b)).astype(BF16), st.astype(BF16))
    b_end = b[T - 1:T]
    kd = (kk * jnp.exp(b_end - b)).astype(BF16)
    st_ref[...] = st * jnp.exp(b_end) + _dot_tn(iv, kd)

    y = o * lax.rsqrt(jnp.mean(o * o, axis=-1, keepdims=True) + EPS) * gn_ref[...]
    out_ref[...] = (y * _sigmoid(g_ref[...])).astype(out_ref.dtype)


def _hgrn2(proj, lb, gnorm):
    b, lp, _ = proj.shape
    T = HG_CHUNK

    def col(off):
        return pl.BlockSpec((None, T, HG_HD), lambda bi, hi, ci: (bi, ci, off + hi))

    vec = pl.BlockSpec((None, 1, HG_HD), lambda bi, hi, ci: (hi, 0, 0))
    return pl.pallas_call(
        _hgrn2_kernel,
        out_shape=jax.ShapeDtypeStruct((b, lp, MIX_W), BF16),
        grid=(b, HG_HEADS, lp // T),
        in_specs=[col(_HGQ), col(_HGF), col(_HGI), col(_HGG), vec, vec],
        out_specs=pl.BlockSpec((None, T, HG_HD), lambda bi, hi, ci: (bi, ci, hi)),
        scratch_shapes=[pltpu.VMEM((HG_HD, HG_HD), F32)],
        compiler_params=pltpu.CompilerParams(dimension_semantics=("parallel", "parallel", "arbitrary")),
        name="hgrn2",
    )(proj, proj, proj, proj, lb.reshape(HG_HEADS, 1, HG_HD), gnorm.reshape(HG_HEADS, 1, HG_HD))


def _rglru_kernel(x_ref, y_ref, cw_ref, cb_ref, w_ref, ba_ref, bx_ref, lam_ref, out_ref, xbuf_ref, h_ref):
    c = pl.program_id(2)
    T = RG_T
    G = RG_GROUP

    @pl.when(c == 0)
    def _():
        xbuf_ref[0:8, :] = jnp.zeros((8, G), F32)
        h_ref[...] = jnp.zeros_like(h_ref)

    valid = (c * T + _iota((T, 1), 0)) >= PAD
    x = jnp.where(valid, x_ref[...], 0.0)
    xbuf_ref[8:8 + T, :] = x
    xc = cb_ref[...] + cw_ref[RG_CONV - 1:RG_CONV, :] * x
    for j in range(1, RG_CONV):
        xc = xc + cw_ref[RG_CONV - 1 - j:RG_CONV - j, :] * xbuf_ref[8 - j:8 - j + T, :]
    xbuf_ref[0:8, :] = x[T - 8:T]

    pre = _dot(xc.astype(BF16), w_ref[...])
    r = _sigmoid(pre[:, :G] + ba_ref[...])
    ig = _sigmoid(pre[:, G:] + bx_ref[...])
    lam = lam_ref[...]
    softplus_neg = jnp.maximum(-lam, 0.0) + jnp.log1p(jnp.exp(-jnp.abs(lam)))
    log_a = -RG_C * r * softplus_neg
    a = jnp.exp(log_a)
    th = jnp.tanh(log_a)
    u = jnp.where(valid, jnp.sqrt(-2.0 * th / (1.0 - th)) * (ig * xc), 0.0)

    row = _iota((T, G), 0)
    shift = 1
    while shift < T:
        keep = row >= shift
        u_s = pltpu.roll(u, shift, axis=0)
        a_s = pltpu.roll(a, shift, axis=0)
        u = jnp.where(keep, a * u_s + u, u)
        a = jnp.where(keep, a * a_s, a)
        shift *= 2
    hseq = a * h_ref[...] + u
    h_ref[...] = hseq[T - 1:T]

    y = y_ref[...]
    gelu = 0.5 * y * (1.0 + jnp.tanh(0.7978845608028654 * (y + 0.044715 * (y * y * y))))
    out_ref[...] = (hseq * gelu).astype(out_ref.dtype)


def _rglru(proj, conv_w, conv_b, w_bd, ba, bx, lam):
    b, lp, _ = proj.shape
    T = RG_T
    G = RG_GROUP
    ng = MIX_W // G
    gw = G // LANES

    def col(off):
        return pl.BlockSpec((None, T, G), lambda bi, gi, ci: (bi, ci, off // gw + gi))

    vec = pl.BlockSpec((1, G), lambda bi, gi, ci: (0, gi))
    return pl.pallas_call(
        _rglru_kernel,
        out_shape=jax.ShapeDtypeStruct((b, lp, MIX_W), BF16),
        grid=(b, ng, lp // T),
        in_specs=[col(_RGX), col(_RGY),
                  pl.BlockSpec((RG_CONV, G), lambda bi, gi, ci: (0, gi)),
                  vec,
                  pl.BlockSpec((None, G, 2 * G), lambda bi, gi, ci: (gi, 0, 0)),
                  vec, vec, vec],
        out_specs=pl.BlockSpec((None, T, G), lambda bi, gi, ci: (bi, ci, gi)),
        scratch_shapes=[pltpu.VMEM((T + 8, G), F32), pltpu.VMEM((1, G), F32)],
        compiler_params=pltpu.CompilerParams(dimension_semantics=("parallel", "parallel", "arbitrary")),
        name="rglru",
    )(proj, proj, conv_w, conv_b.reshape(1, MIX_W), w_bd, ba.reshape(1, MIX_W), bx.reshape(1, MIX_W),
      lam.reshape(1, MIX_W))


def _block_diag_gates(wa, wx):
    per = RG_GROUP // RG_BD
    ng = RG_BLOCKS // per
    eye = jnp.eye(per, dtype=F32)

    def bd(w):
        w4 = w.reshape(ng, per, RG_BD, RG_BD)
        return jnp.einsum("gaij,ab->gaibj", w4, eye).reshape(ng, RG_GROUP, RG_GROUP)

    return jnp.concatenate([bd(wa), bd(wx)], axis=-1).astype(BF16)


def _layer(h, l, p, lower_bounds, tiles):
    b, lp = p["b"], p["lp"]
    m = b * lp
    tm = tiles["tm"]

    w_in = p["w_in"][l]
    b_in = p["b_in"][l]
    mix_cols = 4 * MIX_W
    if_lo = mix_cols
    if_hi = mix_cols + 2 * ML_HEADS
    gate_lo = if_hi + 9 * MIX_W
    w_mix = jnp.concatenate([w_in[:, :mix_cols], w_in[:, if_hi:gate_lo]], axis=1).astype(BF16)
    b_mix = jnp.concatenate([b_in[:mix_cols], b_in[if_hi:gate_lo]]).reshape(1, N_MIX)
    w_if = jnp.pad(w_in[:, if_lo:if_hi], ((0, 0), (0, LANES - 2 * ML_HEADS))).astype(BF16)
    b_if = jnp.pad(b_in[if_lo:if_hi], (0, LANES - 2 * ML_HEADS)).reshape(1, LANES)
    tn_g = tiles["tn_merge"]
    nj = D_MODEL // tn_g
    wg_r = (w_in[:, gate_lo:].astype(BF16).reshape(D_MODEL, N_BRANCH, nj, tn_g)
            .transpose(2, 0, 1, 3).reshape(nj, D_MODEL, N_BRANCH * tn_g))
    bg_r = (b_in[gate_lo:].reshape(N_BRANCH, nj, tn_g).transpose(1, 0, 2).reshape(nj, 1, N_BRANCH * tn_g))
    wu_r = (p["w_up"][l].astype(BF16).reshape(N_BRANCH, MIX_W, nj, tn_g).transpose(2, 0, 1, 3))

    xn = _rmsnorm(h, p["norm_mix"][l])
    proj = _linear_bias(xn, w_mix, b_mix, tm, tiles["tn_mix"], "in_proj_mix").reshape(b, lp, N_MIX)
    gates = _linear_bias(xn, w_if, b_if, tm, LANES, "in_proj_gates").reshape(b, lp, LANES)
    gates_t = jnp.swapaxes(gates[:, :, :8], 1, 2)

    branches = (
        _mlstm(proj, gates, gates_t, p["ml_norm"][l]),
        _stick_breaking(proj, p["sb_q_norm"][l], p["sb_k_norm"][l]),
        _hgrn2(proj, lower_bounds[l], p["hg_norm"][l]),
        _rglru(proj, p["rg_conv_w"][l], p["rg_conv_b"][l], _block_diag_gates(p["rg_wa"][l], p["rg_wx"][l]),
               p["rg_ba"][l], p["rg_bx"][l], p["rg_lambda"][l]),
    )
    branches = tuple(br.reshape(m, MIX_W) for br in branches)
    merged = _merge(xn, branches, wg_r, bg_r, wu_r, tiles["tm_merge"], tn_g)
    h = _linear_res(merged, p["w_out"][l].astype(BF16), h, tm, tiles["tn_out"], D_MODEL, "out_proj")

    d_ff = p["w_ffn_gate"].shape[-1]
    ffp = tiles["ff_pad"]
    hn = _rmsnorm(h, p["norm_ffn"][l])
    wfg = jnp.pad(p["w_ffn_gate"][l], ((0, 0), (0, ffp - d_ff))).astype(BF16)
    wfu = jnp.pad(p["w_ffn_up"][l], ((0, 0), (0, ffp - d_ff))).astype(BF16)
    wfd = jnp.pad(p["w_ffn_down"][l], ((0, ffp - d_ff), (0, 0))).astype(BF16)
    act = _ffn_gate_up(hn, wfg, wfu, tm, tiles["tn_ff"])
    h = _linear_res(act, wfd, h, tm, tiles["tn_down"], tiles["tk_down"], "ffn_down")
    return h


def _forward(x, meta, params, tiles):
    b, seq, d = x.shape
    lp = PAD + N_META + seq
    depth = params["w_in"].shape[0]
    hg_lb = params["hg_lb"]
    p_lb = jax.nn.softmax(hg_lb.astype(F32), axis=0)
    lower_bounds = jnp.clip(jnp.cumsum(p_lb, axis=0) - p_lb[0:1], 0.0, 0.999)

    front = jnp.concatenate([jnp.zeros((PAD, d), x.dtype), meta.astype(x.dtype)], axis=0)
    h = jnp.concatenate([jnp.broadcast_to(front[None], (b, PAD + N_META, d)), x], axis=1)
    h = h.reshape(b * lp, d)
    p = dict(params, b=b, lp=lp)
    for l in range(depth):
        h = _layer(h, l, p, lower_bounds, tiles)
    return h.reshape(b, lp, d)[:, PAD + N_META:]


_TILES = dict(tm=1056, tn_mix=1024, tm_merge=768, tn_merge=256, tn_out=512,
              ff_pad=11264, tn_ff=512, tn_down=512, tk_down=5632)


def kernel(x, meta, norm_mix, norm_ffn, w_in, b_in, ml_norm, sb_q_norm, sb_k_norm, hg_lb, hg_norm,
           rg_conv_w, rg_conv_b, rg_wa, rg_ba, rg_wx, rg_bx, rg_lambda, w_up, w_out,
           w_ffn_gate, w_ffn_up, w_ffn_down):
    params = dict(norm_mix=norm_mix, norm_ffn=norm_ffn, w_in=w_in, b_in=b_in, ml_norm=ml_norm,
                  sb_q_norm=sb_q_norm, sb_k_norm=sb_k_norm, hg_lb=hg_lb, hg_norm=hg_norm,
                  rg_conv_w=rg_conv_w, rg_conv_b=rg_conv_b, rg_wa=rg_wa, rg_ba=rg_ba, rg_wx=rg_wx,
                  rg_bx=rg_bx, rg_lambda=rg_lambda, w_up=w_up, w_out=w_out,
                  w_ffn_gate=w_ffn_gate, w_ffn_up=w_ffn_up, w_ffn_down=w_ffn_down)
    return _forward(x, meta, params, _TILES)
```

```python
import functools

import jax
import jax.numpy as jnp
from jax import lax
from jax.experimental import pallas as pl
from jax.experimental.pallas import tpu as pltpu

F32 = jnp.float32
BF16 = jnp.bfloat16

D_MODEL = 4096
N_META = 16
N_BRANCH = 4
MIX_W = D_MODEL // 4
ML_HEADS = 4
ML_HD = MIX_W // ML_HEADS
SB_HEADS = 8
SB_HD = MIX_W // SB_HEADS
HG_HEADS = 8
HG_HD = MIX_W // HG_HEADS
RG_BLOCKS = 16
RG_BD = MIX_W // RG_BLOCKS
RG_CONV = 4
RG_C = 8.0
EPS = 1e-6
NEG = -1e30

LANES = 128
PAD = LANES - N_META
ML_CHUNK = 128
SB_BLOCK = 128
SB_SPAN = 256
SB_HPS = 4
HG_CHUNK = 64
HG_SUB = 8
HG_HPS = 4
RG_GROUP = 256
RG_T = 128
VMEM_LIMIT = 56 * 1024 * 1024

_SBQ, _SBK, _SBV = 0, 8, 16
_HGQ, _HGF, _HGI, _HGG = 24, 32, 40, 48
_RGX, _RGY = 56, 64
N_ML = 4 * MIX_W
N_REST = 9 * MIX_W


def _sigmoid(x):
    return 1.0 / (1.0 + jnp.exp(-x))


def _log_sigmoid(x):
    return jnp.minimum(x, 0.0) - jnp.log1p(jnp.exp(-jnp.abs(x)))


def _split3(x):
    hi = x.astype(BF16)
    r1 = x - hi.astype(F32)
    mid = r1.astype(BF16)
    lo = (r1 - mid.astype(F32)).astype(BF16)
    return hi, mid, lo


def _dot(a, b):
    return jnp.dot(a, b, preferred_element_type=F32)


def _dot_nt(a, b):
    return lax.dot_general(a, b, (((1,), (1,)), ((), ())), preferred_element_type=F32)


def _dot_tn(a, b):
    return lax.dot_general(a, b, (((0,), (0,)), ((), ())), preferred_element_type=F32)


def _tri_left(x, tri):
    hi, mid, lo = _split3(x)
    return _dot(tri, hi) + _dot(tri, mid) + _dot(tri, lo)


def _tri_right(x, tri):
    hi, mid, lo = _split3(x)
    return _dot(hi, tri) + _dot(mid, tri) + _dot(lo, tri)


def _iota(shape, dim):
    return lax.broadcasted_iota(jnp.int32, shape, dim)


def _rmsnorm_kernel(x_ref, g_ref, o_ref):
    x = x_ref[...]
    y = x * lax.rsqrt(jnp.mean(x * x, axis=-1, keepdims=True) + EPS)
    o_ref[...] = (y * g_ref[...]).astype(o_ref.dtype)


def _rmsnorm(x, g, tm=256):
    m, d = x.shape
    return pl.pallas_call(
        _rmsnorm_kernel,
        out_shape=jax.ShapeDtypeStruct((m, d), BF16),
        grid=(m // tm,),
        in_specs=[pl.BlockSpec((tm, d), lambda i: (i, 0)),
                  pl.BlockSpec((1, d), lambda i: (0, 0))],
        out_specs=pl.BlockSpec((tm, d), lambda i: (i, 0)),
        compiler_params=pltpu.CompilerParams(dimension_semantics=("parallel",)),
        name="rmsnorm",
    )(x, g.reshape(1, d))


def _linear_bias_kernel(x_ref, w_ref, b_ref, o_ref):
    o_ref[...] = _dot(x_ref[...], w_ref[...]) + b_ref[...]


def _linear_bias(x, w, b, tm, tn, name):
    m, k = x.shape
    n = w.shape[1]
    return pl.pallas_call(
        _linear_bias_kernel,
        out_shape=jax.ShapeDtypeStruct((m, n), F32),
        grid=(m // tm, n // tn),
        in_specs=[pl.BlockSpec((tm, k), lambda i, j: (i, 0)),
                  pl.BlockSpec((k, tn), lambda i, j: (0, j)),
                  pl.BlockSpec((1, tn), lambda i, j: (0, j))],
        out_specs=pl.BlockSpec((tm, tn), lambda i, j: (i, j)),
        compiler_params=pltpu.CompilerParams(
            dimension_semantics=("parallel", "parallel"), vmem_limit_bytes=VMEM_LIMIT),
        name=name,
    )(x, w, b)


def _linear_res_kernel(x_ref, w_ref, r_ref, o_ref, *, nk):
    part = _dot(x_ref[...], w_ref[...])
    if nk == 1:
        o_ref[...] = r_ref[...] + part
    else:
        k = pl.program_id(2)

        @pl.when(k == 0)
        def _():
            o_ref[...] = r_ref[...] + part

        @pl.when(k > 0)
        def _():
            o_ref[...] += part


def _linear_res(x, w, res, tm, tn, tk, name):
    m, k = x.shape
    n = w.shape[1]
    nk = k // tk
    return pl.pallas_call(
        functools.partial(_linear_res_kernel, nk=nk),
        out_shape=jax.ShapeDtypeStruct((m, n), F32),
        grid=(m // tm, n // tn, nk),
        in_specs=[pl.BlockSpec((tm, tk), lambda i, j, kk: (i, kk)),
                  pl.BlockSpec((tk, tn), lambda i, j, kk: (kk, j)),
                  pl.BlockSpec((tm, tn), lambda i, j, kk: (i, j))],
        out_specs=pl.BlockSpec((tm, tn), lambda i, j, kk: (i, j)),
        compiler_params=pltpu.CompilerParams(
            dimension_semantics=("parallel", "parallel", "arbitrary"), vmem_limit_bytes=VMEM_LIMIT),
        name=name,
    )(x, w, res)


def _merge_kernel(xn_ref, b0_ref, b1_ref, b2_ref, b3_ref, g0_ref, g1_ref, g2_ref, g3_ref,
                  c0_ref, c1_ref, c2_ref, c3_ref, wu_ref, o_ref):
    xn = xn_ref[...]
    acc = None
    branches = (b0_ref, b1_ref, b2_ref, b3_ref)
    gate_w = (g0_ref, g1_ref, g2_ref, g3_ref)
    gate_b = (c0_ref, c1_ref, c2_ref, c3_ref)
    for kb in range(N_BRANCH):
        g = _dot(xn, gate_w[kb][...]) + gate_b[kb][...]
        u = _dot(branches[kb][...], wu_ref[kb])
        t = _sigmoid(g) * u
        acc = t if acc is None else acc + t
    o_ref[...] = acc.astype(o_ref.dtype)


def _merge(xn, branches, w_gate, b_gate, w_up, tm, tn):
    m, d = xn.shape
    nj = d // tn
    br_spec = pl.BlockSpec((tm, MIX_W), lambda i, j: (i, 0))

    def gw(kb):
        return pl.BlockSpec((d, tn), lambda i, j: (0, kb * nj + j))

    def gb(kb):
        return pl.BlockSpec((1, tn), lambda i, j: (0, kb * nj + j))

    return pl.pallas_call(
        _merge_kernel,
        out_shape=jax.ShapeDtypeStruct((m, d), BF16),
        grid=(m // tm, nj),
        in_specs=[pl.BlockSpec((tm, d), lambda i, j: (i, 0)),
                  br_spec, br_spec, br_spec, br_spec,
                  gw(0), gw(1), gw(2), gw(3), gb(0), gb(1), gb(2), gb(3),
                  pl.BlockSpec((N_BRANCH, MIX_W, tn), lambda i, j: (0, 0, j))],
        out_specs=pl.BlockSpec((tm, tn), lambda i, j: (i, j)),
        compiler_params=pltpu.CompilerParams(
            dimension_semantics=("parallel", "parallel"), vmem_limit_bytes=VMEM_LIMIT),
        name="merge",
    )(xn, *branches, w_gate, w_gate, w_gate, w_gate, b_gate, b_gate, b_gate, b_gate, w_up)


def _ffn_gu_kernel(x_ref, wg_ref, wu_ref, o_ref):
    x = x_ref[...]
    g = _dot(x, wg_ref[...])
    u = _dot(x, wu_ref[...])
    o_ref[...] = (g * _sigmoid(g) * u).astype(o_ref.dtype)


def _ffn_gate_up(x, wg, wu, tm, tn):
    m, k = x.shape
    n = wg.shape[1]
    return pl.pallas_call(
        _ffn_gu_kernel,
        out_shape=jax.ShapeDtypeStruct((m, n), BF16),
        grid=(m // tm, n // tn),
        in_specs=[pl.BlockSpec((tm, k), lambda i, j: (i, 0)),
                  pl.BlockSpec((k, tn), lambda i, j: (0, j)),
                  pl.BlockSpec((k, tn), lambda i, j: (0, j))],
        out_specs=pl.BlockSpec((tm, tn), lambda i, j: (i, j)),
        compiler_params=pltpu.CompilerParams(
            dimension_semantics=("parallel", "parallel"), vmem_limit_bytes=VMEM_LIMIT),
        name="ffn_gate_up",
    )(x, wg, wu)


def _mlstm_kernel(q_ref, k_ref, v_ref, o_ref, gc_ref, gr_ref, gn_ref, out_ref, c_ref, n_ref, m_ref):
    c = pl.program_id(1)
    T = ML_CHUNK

    @pl.when(c == 0)
    def _():
        c_ref[...] = jnp.zeros_like(c_ref)
        n_ref[...] = jnp.zeros_like(n_ref)
        m_ref[...] = jnp.zeros_like(m_ref)

    valid_col = (c * T + _iota((T, 1), 0)) >= PAD
    valid_row = (c * T + _iota((1, T), 1)) >= PAD
    gc = gc_ref[...]
    gr = gr_ref[...]
    tt = _iota((T, T), 0)
    ss = _iota((T, T), 1)
    causal = ss <= tt
    tril = jnp.where(causal, 1.0, 0.0).astype(BF16)
    triu = jnp.where(tt <= ss, 1.0, 0.0).astype(BF16)
    bc_all = _tri_left(jnp.where(valid_col, _log_sigmoid(gc), 0.0), tril)
    br_all = _tri_right(jnp.where(valid_row, _log_sigmoid(gr), 0.0), triu)

    heads = range(ML_HEADS)
    sl = [slice(h * ML_HD, (h + 1) * ML_HD) for h in heads]
    q = [q_ref[:, sl[h]] * (ML_HD ** -0.5) for h in heads]
    k = [k_ref[:, sl[h]] for h in heads]
    qb = [q[h].astype(BF16) for h in heads]
    vb = [v_ref[:, sl[h]].astype(BF16) for h in heads]
    qk = [_dot_nt(qb[h], k[h].astype(BF16)) for h in heads]
    c_prev = [c_ref[h] for h in heads]
    qc = [_dot(qb[h], c_prev[h].astype(BF16)) for h in heads]

    li_col = [jnp.where(valid_col, gc[:, h:h + 1], NEG) for h in heads]
    li_row = [jnp.where(valid_row, gr[h:h + 1, :], NEG) for h in heads]
    b_col = [bc_all[:, ML_HEADS + h:ML_HEADS + h + 1] for h in heads]
    b_row = [br_all[ML_HEADS + h:ML_HEADS + h + 1, :] for h in heads]
    m_prev = [m_ref[:, h:h + 1] for h in heads]
    dmat = [jnp.where(causal, b_col[h] - b_row[h] + li_row[h], NEG) for h in heads]
    g_col = [b_col[h] + m_prev[h] for h in heads]
    m_out = [jnp.maximum(g_col[h], jnp.max(dmat[h], axis=1, keepdims=True)) for h in heads]

    b_end = [b_row[h][:, T - 1:T] for h in heads]
    wlog = [b_end[h] - b_col[h] + li_col[h] for h in heads]
    m_new = [jnp.maximum(b_end[h] + m_prev[h], jnp.max(wlog[h], axis=0, keepdims=True)) for h in heads]
    decay = [jnp.exp(b_end[h] + m_prev[h] - m_new[h]) for h in heads]
    kw = [k[h] * jnp.exp(wlog[h] - m_new[h]) for h in heads]
    kv = [_dot(kw[h].T.astype(BF16), vb[h]) for h in heads]

    s = [qk[h] * jnp.exp(dmat[h] - m_out[h]) for h in heads]
    inter = [jnp.exp(g_col[h] - m_out[h]) for h in heads]
    sv = [_dot(s[h].astype(BF16), vb[h]) for h in heads]

    for h in heads:
        num = sv[h] + inter[h] * qc[h]
        den = (jnp.sum(s[h], axis=1, keepdims=True)
               + inter[h] * jnp.sum(q[h] * n_ref[h], axis=1, keepdims=True))
        hh = num / jnp.maximum(jnp.abs(den), jnp.exp(-m_out[h]))
        y = hh * lax.rsqrt(jnp.mean(hh * hh, axis=-1, keepdims=True) + EPS) * gn_ref[:, sl[h]]
        out_ref[:, sl[h]] = (y * _sigmoid(o_ref[:, sl[h]])).astype(out_ref.dtype)
        c_ref[h] = decay[h] * c_prev[h] + kv[h]
        n_ref[h] = decay[h] * n_ref[h] + jnp.sum(kw[h], axis=0, keepdims=True)
        m_ref[:, h:h + 1] = m_new[h]


def _mlstm(proj, gates, gates_t, gnorm):
    b, lp, _ = proj.shape
    T = ML_CHUNK

    def col(idx):
        return pl.BlockSpec((None, T, MIX_W), lambda bi, ci: (bi, ci, idx))

    return pl.pallas_call(
        _mlstm_kernel,
        out_shape=jax.ShapeDtypeStruct((b, lp, MIX_W), BF16),
        grid=(b, lp // T),
        in_specs=[col(0), col(1), col(2), col(3),
                  pl.BlockSpec((None, T, LANES), lambda bi, ci: (bi, ci, 0)),
                  pl.BlockSpec((None, 8, T), lambda bi, ci: (bi, 0, ci)),
                  pl.BlockSpec((1, MIX_W), lambda bi, ci: (0, 0))],
        out_specs=pl.BlockSpec((None, T, MIX_W), lambda bi, ci: (bi, ci, 0)),
        scratch_shapes=[pltpu.VMEM((ML_HEADS, ML_HD, ML_HD), F32),
                        pltpu.VMEM((ML_HEADS, 1, ML_HD), F32),
                        pltpu.VMEM((1, LANES), F32)],
        compiler_params=pltpu.CompilerParams(dimension_semantics=("parallel", "arbitrary")),
        name="mlstm",
    )(proj, proj, proj, proj, gates, gates_t, gnorm.reshape(1, MIX_W))


def _sb_kernel(q_ref, k_ref, v_ref, gq_ref, gk_ref, out_ref, kn_ref, vb_ref, acc_ref, run_ref):
    qi = pl.program_id(2)
    T = SB_BLOCK
    W = SB_SPAN
    lp = k_ref.shape[0]
    row0 = pl.multiple_of(qi * T, T)

    @pl.when(jnp.bitwise_and(qi, 1) == 0)
    def _():
        kn_ref[pl.ds(row0 + T, T), :] = jnp.zeros((T, kn_ref.shape[1]), BF16)
        vb_ref[pl.ds(row0 + T, T), :] = jnp.zeros((T, vb_ref.shape[1]), BF16)

    vb_ref[pl.ds(row0, T), :] = v_ref[pl.ds(row0, T), :].astype(BF16)
    qn = []
    for h in range(SB_HPS):
        lo, hi = h * SB_HD, (h + 1) * SB_HD
        kh = k_ref[pl.ds(row0, T), lo:hi]
        kn = kh * lax.rsqrt(jnp.mean(kh * kh, axis=-1, keepdims=True) + EPS) * gk_ref[...]
        kn_ref[pl.ds(row0, T), lo:hi] = kn.astype(BF16)
        qh = q_ref[:, lo:hi]
        qn.append((qh * lax.rsqrt(jnp.mean(qh * qh, axis=-1, keepdims=True) + EPS) * gq_ref[...]).astype(BF16))

    acc_ref[...] = jnp.zeros_like(acc_ref)
    run_ref[...] = jnp.zeros_like(run_ref)
    tstrict = jnp.where(_iota((W, W), 0) > _iota((W, W), 1), 1.0, 0.0).astype(BF16)
    t_pos = row0 + _iota((T, W), 0)
    lane = _iota((T, W), 1)
    top = lax.shift_right_logical(qi, 1)

    def body(it, carry):
        col0 = pl.multiple_of((top - it) * W, W)
        s_pos = col0 + lane
        vis = jnp.logical_and(s_pos < t_pos, s_pos >= PAD)
        heads = range(SB_HPS)
        sl = [slice(h * SB_HD, (h + 1) * SB_HD) for h in heads]
        z = [_dot_nt(qn[h], kn_ref[pl.ds(col0, W), sl[h]]) * (SB_HD ** -0.5) for h in heads]
        l1p = [jnp.log(1.0 + jnp.exp(-jnp.abs(z[h]))) for h in heads]
        lk = [jnp.where(vis, -jnp.maximum(z[h], 0.0) - l1p[h], 0.0) for h in heads]
        lk_hi = [lk[h].astype(BF16) for h in heads]
        lk_lo = [(lk[h] - lk_hi[h].astype(F32)).astype(BF16) for h in heads]
        cs = [_dot(lk_hi[h], tstrict) + _dot(lk_lo[h], tstrict) for h in heads]
        a = [jnp.where(vis, jnp.exp(jnp.minimum(z[h], 0.0) - l1p[h] + cs[h] + run_ref[h]), 0.0).astype(BF16)
             for h in heads]
        for h in heads:
            acc_ref[h] += _dot(a[h], vb_ref[pl.ds(col0, W), sl[h]])
            run_ref[h] += jnp.sum(lk[h], axis=1, keepdims=True)
        return carry

    lax.fori_loop(0, top + 1, body, 0)
    for h in range(SB_HPS):
        out_ref[:, h * SB_HD:(h + 1) * SB_HD] = acc_ref[h].astype(out_ref.dtype)


def _stick_breaking(proj, gq, gk):
    b, lp, _ = proj.shape
    T = SB_BLOCK
    wid = SB_HPS * SB_HD
    ng = SB_HEADS // SB_HPS
    per = wid // LANES
    return pl.pallas_call(
        _sb_kernel,
        out_shape=jax.ShapeDtypeStruct((b, lp, MIX_W), BF16),
        grid=(b, ng, lp // T),
        in_specs=[pl.BlockSpec((None, T, wid), lambda bi, gi, qi: (bi, qi, _SBQ // per + gi)),
                  pl.BlockSpec((None, lp, wid), lambda bi, gi, qi: (bi, 0, _SBK // per + gi),
                               pipeline_mode=pl.Buffered(1)),
                  pl.BlockSpec((None, lp, wid), lambda bi, gi, qi: (bi, 0, _SBV // per + gi),
                               pipeline_mode=pl.Buffered(1)),
                  pl.BlockSpec((1, SB_HD), lambda bi, gi, qi: (0, 0)),
                  pl.BlockSpec((1, SB_HD), lambda bi, gi, qi: (0, 0))],
        out_specs=pl.BlockSpec((None, T, wid), lambda bi, gi, qi: (bi, qi, gi)),
        scratch_shapes=[pltpu.VMEM((lp + SB_SPAN - T, wid), BF16),
                        pltpu.VMEM((lp + SB_SPAN - T, wid), BF16),
                        pltpu.VMEM((SB_HPS, T, SB_HD), F32),
                        pltpu.VMEM((SB_HPS, T, 1), F32)],
        compiler_params=pltpu.CompilerParams(
            dimension_semantics=("parallel", "parallel", "arbitrary"), vmem_limit_bytes=VMEM_LIMIT),
        name="stick_breaking",
    )(proj, proj, proj, gq.reshape(1, SB_HD), gk.reshape(1, SB_HD))


def _hgrn2_kernel(q_ref, f_ref, i_ref, g_ref, lb_ref, gn_ref, out_ref, st_ref):
    c = pl.program_id(2)
    T = HG_CHUNK
    S8 = HG_SUB

    @pl.when(c == 0)
    def _():
        st_ref[...] = jnp.zeros_like(st_ref)

    valid = (c * T + _iota((T, 1), 0)) >= PAD
    tril = jnp.where(_iota((T, T), 1) <= _iota((T, T), 0), 1.0, 0.0).astype(BF16)
    lane8 = _iota((S8, T), 1)
    row8 = _iota((S8, T), 0)

    heads = range(HG_HPS)
    hs = [slice(h * HG_HD, (h + 1) * HG_HD) for h in heads]
    lb = [lb_ref[:, hs[h]] for h in heads]
    sig = [_sigmoid(f_ref[:, hs[h]]) for h in heads]
    logf = [jnp.where(valid, jnp.log(lb[h] + (1.0 - lb[h]) * sig[h]), 0.0) for h in heads]
    kk = [jnp.where(valid, (1.0 - lb[h]) * (1.0 - sig[h]), 0.0) for h in heads]
    q = [q_ref[:, hs[h]] * _sigmoid(q_ref[:, hs[h]]) for h in heads]
    iv = [i_ref[:, hs[h]].astype(BF16) for h in heads]
    b = [_tri_left(logf[h], tril) for h in heads]

    st = [st_ref[h] for h in heads]
    b_end = [b[h][T - 1:T] for h in heads]
    o_inter = [_dot_nt((q[h] * jnp.exp(b[h])).astype(BF16), st[h].astype(BF16)) for h in heads]
    st_add = [_dot_tn(iv[h], (kk[h] * jnp.exp(b_end[h] - b[h])).astype(BF16)) for h in heads]

    def diag_block(h, blk):
        r0 = blk * S8
        qb = q[h][r0:r0 + S8]
        bb = b[h][r0:r0 + S8]
        diag = jnp.zeros((S8, T), F32)
        for j in range(S8):
            s_idx = r0 + j
            p = qb * kk[h][s_idx:s_idx + 1] * jnp.exp(jnp.minimum(bb - b[h][s_idx:s_idx + 1], 0.0))
            diag = jnp.where(lane8 == s_idx, jnp.sum(p, axis=1, keepdims=True), diag)
        return jnp.where(lane8 - r0 <= row8, diag, 0.0)

    def off_block(h, blk):
        r0 = blk * S8
        bref = b[h][r0 - 1:r0]
        qs = (q[h][r0:r0 + S8] * jnp.exp(b[h][r0:r0 + S8] - bref)).astype(BF16)
        ks = (kk[h] * jnp.exp(jnp.minimum(bref - b[h], 0.0))).astype(BF16)
        return _dot_nt(qs, ks)

    nblk = T // S8
    off = [[off_block(h, blk) for h in heads] for blk in range(1, nblk)]
    att = []
    for h in heads:
        rows = [diag_block(h, 0)]
        for blk in range(1, nblk):
            rows.append(jnp.where(lane8 < blk * S8, off[blk - 1][h], diag_block(h, blk)))
        att.append(jnp.concatenate(rows, axis=0).astype(BF16))
    o_intra = [_dot(att[h], iv[h]) for h in heads]

    for h in heads:
        o = o_intra[h] + o_inter[h]
        st_ref[h] = st[h] * jnp.exp(b_end[h]) + st_add[h]
        y = o * lax.rsqrt(jnp.mean(o * o, axis=-1, keepdims=True) + EPS) * gn_ref[:, hs[h]]
        out_ref[:, hs[h]] = (y * _sigmoid(g_ref[:, hs[h]])).astype(out_ref.dtype)


def _hgrn2(proj, lb, gnorm):
    b, lp, _ = proj.shape
    T = HG_CHUNK
    wid = HG_HPS * HG_HD
    ng = HG_HEADS // HG_HPS
    per = wid // LANES

    def col(off):
        return pl.BlockSpec((None, T, wid), lambda bi, gi, ci: (bi, ci, off // per + gi))

    vec = pl.BlockSpec((1, wid), lambda bi, gi, ci: (0, gi))
    return pl.pallas_call(
        _hgrn2_kernel,
        out_shape=jax.ShapeDtypeStruct((b, lp, MIX_W), BF16),
        grid=(b, ng, lp // T),
        in_specs=[col(_HGQ), col(_HGF), col(_HGI), col(_HGG), vec, vec],
        out_specs=pl.BlockSpec((None, T, wid), lambda bi, gi, ci: (bi, ci, gi)),
        scratch_shapes=[pltpu.VMEM((HG_HPS, HG_HD, HG_HD), F32)],
        compiler_params=pltpu.CompilerParams(dimension_semantics=("parallel", "parallel", "arbitrary")),
        name="hgrn2",
    )(proj, proj, proj, proj, lb.reshape(1, MIX_W), gnorm.reshape(1, MIX_W))


def _rglru_kernel(x_ref, y_ref, cw_ref, cb_ref, w_ref, ba_ref, bx_ref, lam_ref, out_ref, xbuf_ref, h_ref):
    c = pl.program_id(2)
    T = RG_T
    G = RG_GROUP

    @pl.when(c == 0)
    def _():
        xbuf_ref[0:8, :] = jnp.zeros((8, G), F32)
        h_ref[...] = jnp.zeros_like(h_ref)

    valid = (c * T + _iota((T, 1), 0)) >= PAD
    x = jnp.where(valid, x_ref[...], 0.0)
    xbuf_ref[8:8 + T, :] = x
    xc = cb_ref[...] + cw_ref[RG_CONV - 1:RG_CONV, :] * x
    for j in range(1, RG_CONV):
        xc = xc + cw_ref[RG_CONV - 1 - j:RG_CONV - j, :] * xbuf_ref[8 - j:8 - j + T, :]
    xbuf_ref[0:8, :] = x[T - 8:T]

    pre = _dot(xc.astype(BF16), w_ref[...])
    r = _sigmoid(pre[:, :G] + ba_ref[...])
    ig = _sigmoid(pre[:, G:] + bx_ref[...])
    lam = lam_ref[...]
    softplus_neg = jnp.maximum(-lam, 0.0) + jnp.log1p(jnp.exp(-jnp.abs(lam)))
    log_a = -RG_C * r * softplus_neg
    a = jnp.exp(log_a)
    th = jnp.tanh(log_a)
    u = jnp.where(valid, jnp.sqrt(-2.0 * th / (1.0 - th)) * (ig * xc), 0.0)

    row = _iota((T, G), 0)
    shift = 1
    while shift < T:
        keep = row >= shift
        u_s = pltpu.roll(u, shift, axis=0)
        a_s = pltpu.roll(a, shift, axis=0)
        u = jnp.where(keep, a * u_s + u, u)
        a = jnp.where(keep, a * a_s, a)
        shift *= 2
    hseq = a * h_ref[...] + u
    h_ref[...] = hseq[T - 1:T]

    y = y_ref[...]
    gelu = 0.5 * y * (1.0 + jnp.tanh(0.7978845608028654 * (y + 0.044715 * (y * y * y))))
    out_ref[...] = (hseq * gelu).astype(out_ref.dtype)


def _rglru(proj, conv_w, conv_b, w_bd, ba, bx, lam):
    b, lp, _ = proj.shape
    T = RG_T
    G = RG_GROUP
    ng = MIX_W // G
    per = G // LANES

    def col(off):
        return pl.BlockSpec((None, T, G), lambda bi, gi, ci: (bi, ci, off // per + gi))

    vec = pl.BlockSpec((1, G), lambda bi, gi, ci: (0, gi))
    return pl.pallas_call(
        _rglru_kernel,
        out_shape=jax.ShapeDtypeStruct((b, lp, MIX_W), BF16),
        grid=(b, ng, lp // T),
        in_specs=[col(_RGX), col(_RGY),
                  pl.BlockSpec((RG_CONV, G), lambda bi, gi, ci: (0, gi)),
                  vec,
                  pl.BlockSpec((None, G, 2 * G), lambda bi, gi, ci: (gi, 0, 0)),
                  vec, vec, vec],
        out_specs=pl.BlockSpec((None, T, G), lambda bi, gi, ci: (bi, ci, gi)),
        scratch_shapes=[pltpu.VMEM((T + 8, G), F32), pltpu.VMEM((1, G), F32)],
        compiler_params=pltpu.CompilerParams(dimension_semantics=("parallel", "parallel", "arbitrary")),
        name="rglru",
    )(proj, proj, conv_w, conv_b.reshape(1, MIX_W), w_bd, ba.reshape(1, MIX_W), bx.reshape(1, MIX_W),
      lam.reshape(1, MIX_W))


def _block_diag_gates(wa, wx):
    per = RG_GROUP // RG_BD
    ng = RG_BLOCKS // per
    eye = jnp.eye(per, dtype=F32)

    def bd(w):
        w4 = w.reshape(ng, per, RG_BD, RG_BD)
        return jnp.einsum("gaij,ab->gaibj", w4, eye).reshape(ng, RG_GROUP, RG_GROUP)

    return jnp.concatenate([bd(wa), bd(wx)], axis=-1).astype(BF16)


def _layer(h, l, p, lower_bounds, tiles):
    b, lp = p["b"], p["lp"]
    m = b * lp
    tm = tiles["tm"]

    w_in = p["w_in"][l]
    b_in = p["b_in"][l]
    if_lo = N_ML
    if_hi = N_ML + 2 * ML_HEADS
    gate_lo = if_hi + N_REST
    w_ml = w_in[:, :N_ML].astype(BF16)
    w_rest = w_in[:, if_hi:gate_lo].astype(BF16)
    w_gate = w_in[:, gate_lo:].astype(BF16)
    w_if = jnp.pad(w_in[:, if_lo:if_hi], ((0, 0), (0, LANES - 2 * ML_HEADS))).astype(BF16)
    b_if = jnp.pad(b_in[if_lo:if_hi], (0, LANES - 2 * ML_HEADS)).reshape(1, LANES)

    xn = _rmsnorm(h, p["norm_mix"][l])
    proj_ml = _linear_bias(xn, w_ml, b_in[:N_ML].reshape(1, N_ML), tm, tiles["tn_mix"], "in_proj_ml")
    proj_rest = _linear_bias(xn, w_rest, b_in[if_hi:gate_lo].reshape(1, N_REST), tm, tiles["tn_mix"],
                             "in_proj_rest")
    proj_ml = proj_ml.reshape(b, lp, N_ML)
    proj_rest = proj_rest.reshape(b, lp, N_REST)
    gates = _linear_bias(xn, w_if, b_if, tm, LANES, "in_proj_gates").reshape(b, lp, LANES)
    gates_t = jnp.swapaxes(gates[:, :, :8], 1, 2)

    branches = (
        _mlstm(proj_ml, gates, gates_t, p["ml_norm"][l]),
        _stick_breaking(proj_rest, p["sb_q_norm"][l], p["sb_k_norm"][l]),
        _hgrn2(proj_rest, lower_bounds[l], p["hg_norm"][l]),
        _rglru(proj_rest, p["rg_conv_w"][l], p["rg_conv_b"][l],
               _block_diag_gates(p["rg_wa"][l], p["rg_wx"][l]),
               p["rg_ba"][l], p["rg_bx"][l], p["rg_lambda"][l]),
    )
    branches = tuple(br.reshape(m, MIX_W) for br in branches)
    merged = _merge(xn, branches, w_gate, b_in[gate_lo:].reshape(1, N_BRANCH * D_MODEL),
                    p["w_up"][l].astype(BF16), tiles["tm_merge"], tiles["tn_merge"])
    h = _linear_res(merged, p["w_out"][l].astype(BF16), h, tm, tiles["tn_out"], D_MODEL, "out_proj")

    d_ff = p["w_ffn_gate"].shape[-1]
    hn = _rmsnorm(h, p["norm_ffn"][l])
    act = _ffn_gate_up(hn, p["w_ffn_gate"][l].astype(BF16), p["w_ffn_up"][l].astype(BF16),
                       tiles["tm_ff"], tiles["tn_ff"])
    h = _linear_res(act, p["w_ffn_down"][l].astype(BF16), h, tm, tiles["tn_down"], d_ff // tiles["nk_down"],
                    "ffn_down")
    return h


def _forward(x, meta, params, tiles):
    b, seq, d = x.shape
    lp = PAD + N_META + seq
    depth = params["w_in"].shape[0]
    hg_lb = params["hg_lb"]
    p_lb = jax.nn.softmax(hg_lb.astype(F32), axis=0)
    lower_bounds = jnp.clip(jnp.cumsum(p_lb, axis=0) - p_lb[0:1], 0.0, 0.999)

    front = jnp.concatenate([jnp.zeros((PAD, d), x.dtype), meta.astype(x.dtype)], axis=0)
    h = jnp.concatenate([jnp.broadcast_to(front[None], (b, PAD + N_META, d)), x], axis=1)
    h = h.reshape(b * lp, d)
    p = dict(params, b=b, lp=lp)
    for l in range(depth):
        h = _layer(h, l, p, lower_bounds, tiles)
    return h.reshape(b, lp, d)[:, PAD + N_META:]


_TILES = dict(tm=1056, tn_mix=1024, tm_merge=768, tn_merge=256, tn_out=512,
              tm_ff=2112, tn_ff=256, tn_down=512, nk_down=2)


def kernel(x, meta, norm_mix, norm_ffn, w_in, b_in, ml_norm, sb_q_norm, sb_k_norm, hg_lb, hg_norm,
           rg_conv_w, rg_conv_b, rg_wa, rg_ba, rg_wx, rg_bx, rg_lambda, w_up, w_out,
           w_ffn_gate, w_ffn_up, w_ffn_down):
    params = dict(norm_mix=norm_mix, norm_ffn=norm_ffn, w_in=w_in, b_in=b_in, ml_norm=ml_norm,
                  sb_q_norm=sb_q_norm, sb_k_norm=sb_k_norm, hg_lb=hg_lb, hg_norm=hg_norm,
                  rg_conv_w=rg_conv_w, rg_conv_b=rg_conv_b, rg_wa=rg_wa, rg_ba=rg_ba, rg_wx=rg_wx,
                  rg_bx=rg_bx, rg_lambda=rg_lambda, w_up=w_up, w_out=w_out,
                  w_ffn_gate=w_ffn_gate, w_ffn_up=w_ffn_up, w_ffn_down=w_ffn_down)
    return _forward(x, meta, params, _TILES)
```

```python
import functools

import jax
import jax.numpy as jnp
from jax import lax
from jax.experimental import pallas as pl
from jax.experimental.pallas import tpu as pltpu

F32 = jnp.float32
BF16 = jnp.bfloat16

D_MODEL = 4096
N_META = 16
N_BRANCH = 4
MIX_W = D_MODEL // 4
ML_HEADS = 4
ML_HD = MIX_W // ML_HEADS
SB_HEADS = 8
SB_HD = MIX_W // SB_HEADS
HG_HEADS = 8
HG_HD = MIX_W // HG_HEADS
RG_BLOCKS = 16
RG_BD = MIX_W // RG_BLOCKS
RG_CONV = 4
RG_C = 8.0
EPS = 1e-6
NEG = -1e30

LANES = 128
PAD = LANES - N_META
ML_CHUNK = 128
SB_BLOCK = 128
SB_SPAN = 256
SB_HPS = 4
HG_CHUNK = 64
HG_SUB = 8
HG_HPS = 4
RG_GROUP = 256
RG_T = 128
VMEM_LIMIT = 56 * 1024 * 1024

_SBQ, _SBK, _SBV = 0, 8, 16
_HGQ, _HGF, _HGI, _HGG = 24, 32, 40, 48
_RGX, _RGY = 56, 64
N_ML = 4 * MIX_W
N_REST = 9 * MIX_W


def _sigmoid(x):
    return 1.0 / (1.0 + jnp.exp(-x))


def _log_sigmoid(x):
    return jnp.minimum(x, 0.0) - jnp.log1p(jnp.exp(-jnp.abs(x)))


def _split3(x):
    hi = x.astype(BF16)
    r1 = x - hi.astype(F32)
    mid = r1.astype(BF16)
    lo = (r1 - mid.astype(F32)).astype(BF16)
    return hi, mid, lo


def _dot(a, b):
    return jnp.dot(a, b, preferred_element_type=F32)


def _dot_nt(a, b):
    return lax.dot_general(a, b, (((1,), (1,)), ((), ())), preferred_element_type=F32)


def _dot_tn(a, b):
    return lax.dot_general(a, b, (((0,), (0,)), ((), ())), preferred_element_type=F32)


def _tri_left(x, tri):
    hi, mid, lo = _split3(x)
    return _dot(tri, hi) + _dot(tri, mid) + _dot(tri, lo)


def _tri_right(x, tri):
    hi, mid, lo = _split3(x)
    return _dot(hi, tri) + _dot(mid, tri) + _dot(lo, tri)


def _iota(shape, dim):
    return lax.broadcasted_iota(jnp.int32, shape, dim)


def _rmsnorm_kernel(x_ref, g_ref, o_ref):
    x = x_ref[...]
    y = x * lax.rsqrt(jnp.mean(x * x, axis=-1, keepdims=True) + EPS)
    o_ref[...] = (y * g_ref[...]).astype(o_ref.dtype)


def _rmsnorm(x, g, tm=256):
    m, d = x.shape
    return pl.pallas_call(
        _rmsnorm_kernel,
        out_shape=jax.ShapeDtypeStruct((m, d), BF16),
        grid=(m // tm,),
        in_specs=[pl.BlockSpec((tm, d), lambda i: (i, 0)),
                  pl.BlockSpec((1, d), lambda i: (0, 0))],
        out_specs=pl.BlockSpec((tm, d), lambda i: (i, 0)),
        compiler_params=pltpu.CompilerParams(dimension_semantics=("parallel",)),
        name="rmsnorm",
    )(x, g.reshape(1, d))


def _linear_bias_kernel(x_ref, w_ref, b_ref, o_ref):
    o_ref[...] = _dot(x_ref[...], w_ref[...]) + b_ref[...]


def _linear_bias(x, w, b, tm, tn, name):
    m, k = x.shape
    n = w.shape[1]
    return pl.pallas_call(
        _linear_bias_kernel,
        out_shape=jax.ShapeDtypeStruct((m, n), F32),
        grid=(m // tm, n // tn),
        in_specs=[pl.BlockSpec((tm, k), lambda i, j: (i, 0)),
                  pl.BlockSpec((k, tn), lambda i, j: (0, j)),
                  pl.BlockSpec((1, tn), lambda i, j: (0, j))],
        out_specs=pl.BlockSpec((tm, tn), lambda i, j: (i, j)),
        compiler_params=pltpu.CompilerParams(
            dimension_semantics=("parallel", "parallel"), vmem_limit_bytes=VMEM_LIMIT),
        name=name,
    )(x, w, b)


def _linear_res_kernel(x_ref, w_ref, r_ref, o_ref, *, nk):
    part = _dot(x_ref[...], w_ref[...])
    if nk == 1:
        o_ref[...] = r_ref[...] + part
    else:
        k = pl.program_id(2)

        @pl.when(k == 0)
        def _():
            o_ref[...] = r_ref[...] + part

        @pl.when(k > 0)
        def _():
            o_ref[...] += part


def _linear_res(x, w, res, tm, tn, tk, name):
    m, k = x.shape
    n = w.shape[1]
    nk = k // tk
    return pl.pallas_call(
        functools.partial(_linear_res_kernel, nk=nk),
        out_shape=jax.ShapeDtypeStruct((m, n), F32),
        grid=(m // tm, n // tn, nk),
        in_specs=[pl.BlockSpec((tm, tk), lambda i, j, kk: (i, kk)),
                  pl.BlockSpec((tk, tn), lambda i, j, kk: (kk, j)),
                  pl.BlockSpec((tm, tn), lambda i, j, kk: (i, j))],
        out_specs=pl.BlockSpec((tm, tn), lambda i, j, kk: (i, j)),
        compiler_params=pltpu.CompilerParams(
            dimension_semantics=("parallel", "parallel", "arbitrary"), vmem_limit_bytes=VMEM_LIMIT),
        name=name,
    )(x, w, res)


def _merge_kernel(xn_ref, b0_ref, b1_ref, b2_ref, b3_ref, g0_ref, g1_ref, g2_ref, g3_ref,
                  c0_ref, c1_ref, c2_ref, c3_ref, wu_ref, o_ref):
    xn = xn_ref[...]
    acc = None
    branches = (b0_ref, b1_ref, b2_ref, b3_ref)
    gate_w = (g0_ref, g1_ref, g2_ref, g3_ref)
    gate_b = (c0_ref, c1_ref, c2_ref, c3_ref)
    for kb in range(N_BRANCH):
        g = _dot(xn, gate_w[kb][...]) + gate_b[kb][...]
        u = _dot(branches[kb][...], wu_ref[kb])
        t = _sigmoid(g) * u
        acc = t if acc is None else acc + t
    o_ref[...] = acc.astype(o_ref.dtype)


def _merge(xn, branches, w_gate, b_gate, w_up, tm, tn):
    m, d = xn.shape
    nj = d // tn
    br_spec = pl.BlockSpec((tm, MIX_W), lambda i, j: (i, 0))

    def gw(kb):
        return pl.BlockSpec((d, tn), lambda i, j: (0, kb * nj + j))

    def gb(kb):
        return pl.BlockSpec((1, tn), lambda i, j: (0, kb * nj + j))

    return pl.pallas_call(
        _merge_kernel,
        out_shape=jax.ShapeDtypeStruct((m, d), BF16),
        grid=(m // tm, nj),
        in_specs=[pl.BlockSpec((tm, d), lambda i, j: (i, 0)),
                  br_spec, br_spec, br_spec, br_spec,
                  gw(0), gw(1), gw(2), gw(3), gb(0), gb(1), gb(2), gb(3),
                  pl.BlockSpec((N_BRANCH, MIX_W, tn), lambda i, j: (0, 0, j))],
        out_specs=pl.BlockSpec((tm, tn), lambda i, j: (i, j)),
        compiler_params=pltpu.CompilerParams(
            dimension_semantics=("parallel", "parallel"), vmem_limit_bytes=VMEM_LIMIT),
        name="merge",
    )(xn, *branches, w_gate, w_gate, w_gate, w_gate, b_gate, b_gate, b_gate, b_gate, w_up)


def _ffn_gu_kernel(x_ref, wg_ref, wu_ref, o_ref):
    x = x_ref[...]
    g = _dot(x, wg_ref[...])
    u = _dot(x, wu_ref[...])
    o_ref[...] = (g * _sigmoid(g) * u).astype(o_ref.dtype)


def _ffn_gate_up(x, wg, wu, tm, tn):
    m, k = x.shape
    n = wg.shape[1]
    return pl.pallas_call(
        _ffn_gu_kernel,
        out_shape=jax.ShapeDtypeStruct((m, n), BF16),
        grid=(m // tm, n // tn),
        in_specs=[pl.BlockSpec((tm, k), lambda i, j: (i, 0)),
                  pl.BlockSpec((k, tn), lambda i, j: (0, j)),
                  pl.BlockSpec((k, tn), lambda i, j: (0, j))],
        out_specs=pl.BlockSpec((tm, tn), lambda i, j: (i, j)),
        compiler_params=pltpu.CompilerParams(
            dimension_semantics=("parallel", "parallel"), vmem_limit_bytes=VMEM_LIMIT),
        name="ffn_gate_up",
    )(x, wg, wu)


def _mlstm_kernel(q_ref, k_ref, v_ref, o_ref, gc_ref, gr_ref, gn_ref, out_ref, c_ref, n_ref, m_ref):
    c = pl.program_id(1)
    T = ML_CHUNK

    @pl.when(c == 0)
    def _():
        c_ref[...] = jnp.zeros_like(c_ref)
        n_ref[...] = jnp.zeros_like(n_ref)
        m_ref[...] = jnp.zeros_like(m_ref)

    valid_col = (c * T + _iota((T, 1), 0)) >= PAD
    valid_row = (c * T + _iota((1, T), 1)) >= PAD
    gc = gc_ref[...]
    gr = gr_ref[...]
    tt = _iota((T, T), 0)
    ss = _iota((T, T), 1)
    causal = ss <= tt
    tril = jnp.where(causal, 1.0, 0.0).astype(BF16)
    triu = jnp.where(tt <= ss, 1.0, 0.0).astype(BF16)
    bc_all = _tri_left(jnp.where(valid_col, _log_sigmoid(gc), 0.0), tril)
    br_all = _tri_right(jnp.where(valid_row, _log_sigmoid(gr), 0.0), triu)

    heads = range(ML_HEADS)
    sl = [slice(h * ML_HD, (h + 1) * ML_HD) for h in heads]
    q = [q_ref[:, sl[h]] * (ML_HD ** -0.5) for h in heads]
    k = [k_ref[:, sl[h]] for h in heads]
    qb = [q[h].astype(BF16) for h in heads]
    vb = [v_ref[:, sl[h]].astype(BF16) for h in heads]
    qk = [_dot_nt(qb[h], k[h].astype(BF16)) for h in heads]
    c_prev = [c_ref[h] for h in heads]
    qc = [_dot(qb[h], c_prev[h].astype(BF16)) for h in heads]

    li_col = [jnp.where(valid_col, gc[:, h:h + 1], NEG) for h in heads]
    li_row = [jnp.where(valid_row, gr[h:h + 1, :], NEG) for h in heads]
    b_col = [bc_all[:, ML_HEADS + h:ML_HEADS + h + 1] for h in heads]
    b_row = [br_all[ML_HEADS + h:ML_HEADS + h + 1, :] for h in heads]
    m_prev = [m_ref[:, h:h + 1] for h in heads]
    dmat = [jnp.where(causal, b_col[h] - b_row[h] + li_row[h], NEG) for h in heads]
    g_col = [b_col[h] + m_prev[h] for h in heads]
    m_out = [jnp.maximum(g_col[h], jnp.max(dmat[h], axis=1, keepdims=True)) for h in heads]

    b_end = [b_row[h][:, T - 1:T] for h in heads]
    wlog = [b_end[h] - b_col[h] + li_col[h] for h in heads]
    m_new = [jnp.maximum(b_end[h] + m_prev[h], jnp.max(wlog[h], axis=0, keepdims=True)) for h in heads]
    decay = [jnp.exp(b_end[h] + m_prev[h] - m_new[h]) for h in heads]
    kw = [k[h] * jnp.exp(wlog[h] - m_new[h]) for h in heads]
    kv = [_dot(kw[h].T.astype(BF16), vb[h]) for h in heads]

    s = [qk[h] * jnp.exp(dmat[h] - m_out[h]) for h in heads]
    inter = [jnp.exp(g_col[h] - m_out[h]) for h in heads]
    sv = [_dot(s[h].astype(BF16), vb[h]) for h in heads]

    for h in heads:
        num = sv[h] + inter[h] * qc[h]
        den = (jnp.sum(s[h], axis=1, keepdims=True)
               + inter[h] * jnp.sum(q[h] * n_ref[h], axis=1, keepdims=True))
        hh = num / jnp.maximum(jnp.abs(den), jnp.exp(-m_out[h]))
        y = hh * lax.rsqrt(jnp.mean(hh * hh, axis=-1, keepdims=True) + EPS) * gn_ref[:, sl[h]]
        out_ref[:, sl[h]] = (y * _sigmoid(o_ref[:, sl[h]])).astype(out_ref.dtype)
        c_ref[h] = decay[h] * c_prev[h] + kv[h]
        n_ref[h] = decay[h] * n_ref[h] + jnp.sum(kw[h], axis=0, keepdims=True)
        m_ref[:, h:h + 1] = m_new[h]


def _mlstm(proj, gates, gates_t, gnorm):
    b, lp, _ = proj.shape
    T = ML_CHUNK

    def col(idx):
        return pl.BlockSpec((None, T, MIX_W), lambda bi, ci: (bi, ci, idx))

    return pl.pallas_call(
        _mlstm_kernel,
        out_shape=jax.ShapeDtypeStruct((b, lp, MIX_W), BF16),
        grid=(b, lp // T),
        in_specs=[col(0), col(1), col(2), col(3),
                  pl.BlockSpec((None, T, LANES), lambda bi, ci: (bi, ci, 0)),
                  pl.BlockSpec((None, 8, T), lambda bi, ci: (bi, 0, ci)),
                  pl.BlockSpec((1, MIX_W), lambda bi, ci: (0, 0))],
        out_specs=pl.BlockSpec((None, T, MIX_W), lambda bi, ci: (bi, ci, 0)),
        scratch_shapes=[pltpu.VMEM((ML_HEADS, ML_HD, ML_HD), F32),
                        pltpu.VMEM((ML_HEADS, 1, ML_HD), F32),
                        pltpu.VMEM((1, LANES), F32)],
        compiler_params=pltpu.CompilerParams(dimension_semantics=("parallel", "arbitrary")),
        name="mlstm",
    )(proj, proj, proj, proj, gates, gates_t, gnorm.reshape(1, MIX_W))


def _sb_kernel(q_ref, k_ref, v_ref, gq_ref, gk_ref, out_ref, kn_ref, vb_ref, acc_ref, run_ref, qn_ref, tri_ref,
               z_ref, a_ref):
    qi = pl.program_id(2)
    T = SB_BLOCK
    W = SB_SPAN
    row0 = pl.multiple_of(qi * T, T)

    @pl.when(jnp.bitwise_and(qi, 1) == 0)
    def _():
        kn_ref[pl.ds(row0 + T, T), :] = jnp.zeros((T, kn_ref.shape[1]), BF16)
        vb_ref[pl.ds(row0 + T, T), :] = jnp.zeros((T, vb_ref.shape[1]), BF16)

    @pl.when(qi == 0)
    def _():
        jj = jnp.bitwise_and(_iota((2 * W, W), 0), W - 1)
        tri_ref[...] = jnp.where(jj > _iota((2 * W, W), 1), 1.0, 0.0).astype(BF16)

    heads = range(SB_HPS)
    sl = [slice(h * SB_HD, (h + 1) * SB_HD) for h in heads]
    vb_ref[pl.ds(row0, T), :] = v_ref[pl.ds(row0, T), :].astype(BF16)
    for h in heads:
        kh = k_ref[pl.ds(row0, T), sl[h]]
        kn = kh * lax.rsqrt(jnp.mean(kh * kh, axis=-1, keepdims=True) + EPS) * gk_ref[...]
        kn_ref[pl.ds(row0, T), sl[h]] = kn.astype(BF16)
        qh = q_ref[:, sl[h]]
        qs = qh * lax.rsqrt(jnp.mean(qh * qh, axis=-1, keepdims=True) + EPS) * (gq_ref[...] * SB_HD ** -0.5)
        qn_ref[:, sl[h]] = qs.astype(BF16)

    acc_ref[...] = jnp.zeros_like(acc_ref)
    run_ref[...] = jnp.zeros_like(run_ref)
    t_pos = row0 + _iota((T, W), 0)
    lane = _iota((T, W), 1)
    top = lax.shift_right_logical(qi, 1)

    def scores(col0, slot):
        for h in heads:
            z_ref[slot, h] = _dot_nt(qn_ref[:, sl[h]], kn_ref[pl.ds(col0, W), sl[h]])

    def weighted_values(col0):
        for h in heads:
            acc_ref[h] += _dot(a_ref[h], vb_ref[pl.ds(col0, W), sl[h]])

    def span_step(it, masked, has_prev):
        col0 = pl.multiple_of((top - it) * W, W)
        slot = it & 1
        z = [z_ref[slot, h] for h in heads]
        if has_prev:
            weighted_values(pl.multiple_of(col0 + W, W))
        l1p = [jnp.log(1.0 + jnp.exp(-jnp.abs(z[h]))) for h in heads]
        ls = [jnp.minimum(z[h], 0.0) - l1p[h] for h in heads]
        lk = [ls[h] - z[h] for h in heads]
        if masked:
            s_pos = col0 + lane
            vis = jnp.logical_and(s_pos < t_pos, s_pos >= PAD)
            lk = [jnp.where(vis, lk[h], 0.0) for h in heads]
        lk_hi = [lk[h].astype(BF16) for h in heads]
        lk_lo = [(lk[h] - lk_hi[h].astype(F32)).astype(BF16) for h in heads]
        cs = [_dot(jnp.concatenate([lk_hi[h], lk_lo[h]], axis=1), tri_ref[...]) for h in heads]
        scores(pl.multiple_of(jnp.maximum(top - it - 1, 0) * W, W), 1 - slot)
        a = [jnp.exp(ls[h] + cs[h] + run_ref[h]) for h in heads]
        if masked:
            a = [jnp.where(vis, a[h], 0.0) for h in heads]
        for h in heads:
            a_ref[h] = a[h].astype(BF16)
            run_ref[h] += jnp.sum(lk[h], axis=1, keepdims=True)

    scores(pl.multiple_of(top * W, W), 0)
    span_step(0, True, False)

    def body(it, carry):
        span_step(it, False, True)
        return carry

    lax.fori_loop(1, top, body, 0)

    @pl.when(top > 0)
    def _():
        span_step(top, True, True)

    weighted_values(0)
    for h in heads:
        out_ref[:, sl[h]] = acc_ref[h].astype(out_ref.dtype)


def _stick_breaking(proj, gq, gk):
    b, lp, _ = proj.shape
    T = SB_BLOCK
    wid = SB_HPS * SB_HD
    ng = SB_HEADS // SB_HPS
    per = wid // LANES
    return pl.pallas_call(
        _sb_kernel,
        out_shape=jax.ShapeDtypeStruct((b, lp, MIX_W), BF16),
        grid=(b, ng, lp // T),
        in_specs=[pl.BlockSpec((None, T, wid), lambda bi, gi, qi: (bi, qi, _SBQ // per + gi)),
                  pl.BlockSpec((None, lp, wid), lambda bi, gi, qi: (bi, 0, _SBK // per + gi),
                               pipeline_mode=pl.Buffered(1)),
                  pl.BlockSpec((None, lp, wid), lambda bi, gi, qi: (bi, 0, _SBV // per + gi),
                               pipeline_mode=pl.Buffered(1)),
                  pl.BlockSpec((1, SB_HD), lambda bi, gi, qi: (0, 0)),
                  pl.BlockSpec((1, SB_HD), lambda bi, gi, qi: (0, 0))],
        out_specs=pl.BlockSpec((None, T, wid), lambda bi, gi, qi: (bi, qi, gi)),
        scratch_shapes=[pltpu.VMEM((lp + SB_SPAN - T, wid), BF16),
                        pltpu.VMEM((lp + SB_SPAN - T, wid), BF16),
                        pltpu.VMEM((SB_HPS, T, SB_HD), F32),
                        pltpu.VMEM((SB_HPS, T, 1), F32),
                        pltpu.VMEM((T, wid), BF16),
                        pltpu.VMEM((2 * SB_SPAN, SB_SPAN), BF16),
                        pltpu.VMEM((2, SB_HPS, T, SB_SPAN), F32),
                        pltpu.VMEM((SB_HPS, T, SB_SPAN), BF16)],
        compiler_params=pltpu.CompilerParams(
            dimension_semantics=("parallel", "parallel", "arbitrary"), vmem_limit_bytes=VMEM_LIMIT),
        name="stick_breaking",
    )(proj, proj, proj, gq.reshape(1, SB_HD), gk.reshape(1, SB_HD))


def _hgrn2_kernel(q_ref, f_ref, i_ref, g_ref, lb_ref, gn_ref, out_ref, st_ref):
    c = pl.program_id(2)
    T = HG_CHUNK
    S8 = HG_SUB

    @pl.when(c == 0)
    def _():
        st_ref[...] = jnp.zeros_like(st_ref)

    valid = (c * T + _iota((T, 1), 0)) >= PAD
    tril = jnp.where(_iota((T, T), 1) <= _iota((T, T), 0), 1.0, 0.0).astype(BF16)
    lane8 = _iota((S8, T), 1)
    row8 = _iota((S8, T), 0)

    heads = range(HG_HPS)
    hs = [slice(h * HG_HD, (h + 1) * HG_HD) for h in heads]
    lb = [lb_ref[:, hs[h]] for h in heads]
    sig = [_sigmoid(f_ref[:, hs[h]]) for h in heads]
    logf = [jnp.where(valid, jnp.log(lb[h] + (1.0 - lb[h]) * sig[h]), 0.0) for h in heads]
    kk = [jnp.where(valid, (1.0 - lb[h]) * (1.0 - sig[h]), 0.0) for h in heads]
    q = [q_ref[:, hs[h]] * _sigmoid(q_ref[:, hs[h]]) for h in heads]
    iv = [i_ref[:, hs[h]].astype(BF16) for h in heads]
    b = [_tri_left(logf[h], tril) for h in heads]

    st = [st_ref[h] for h in heads]
    b_end = [b[h][T - 1:T] for h in heads]
    o_inter = [_dot_nt((q[h] * jnp.exp(b[h])).astype(BF16), st[h].astype(BF16)) for h in heads]
    st_add = [_dot_tn(iv[h], (kk[h] * jnp.exp(b_end[h] - b[h])).astype(BF16)) for h in heads]

    def diag_block(h, blk):
        r0 = blk * S8
        qb = q[h][r0:r0 + S8]
        bb = b[h][r0:r0 + S8]
        diag = jnp.zeros((S8, T), F32)
        for j in range(S8):
            s_idx = r0 + j
            p = qb * kk[h][s_idx:s_idx + 1] * jnp.exp(jnp.minimum(bb - b[h][s_idx:s_idx + 1], 0.0))
            diag = jnp.where(lane8 == s_idx, jnp.sum(p, axis=1, keepdims=True), diag)
        return jnp.where(lane8 - r0 <= row8, diag, 0.0)

    def off_block(h, blk):
        r0 = blk * S8
        bref = b[h][r0 - 1:r0]
        qs = (q[h][r0:r0 + S8] * jnp.exp(b[h][r0:r0 + S8] - bref)).astype(BF16)
        ks = (kk[h] * jnp.exp(jnp.minimum(bref - b[h], 0.0))).astype(BF16)
        return _dot_nt(qs, ks)

    nblk = T // S8
    off = [[off_block(h, blk) for h in heads] for blk in range(1, nblk)]
    att = []
    for h in heads:
        rows = [diag_block(h, 0)]
        for blk in range(1, nblk):
            rows.append(jnp.where(lane8 < blk * S8, off[blk - 1][h], diag_block(h, blk)))
        att.append(jnp.concatenate(rows, axis=0).astype(BF16))
    o_intra = [_dot(att[h], iv[h]) for h in heads]

    for h in heads:
        o = o_intra[h] + o_inter[h]
        st_ref[h] = st[h] * jnp.exp(b_end[h]) + st_add[h]
        y = o * lax.rsqrt(jnp.mean(o * o, axis=-1, keepdims=True) + EPS) * gn_ref[:, hs[h]]
        out_ref[:, hs[h]] = (y * _sigmoid(g_ref[:, hs[h]])).astype(out_ref.dtype)


def _hgrn2(proj, lb, gnorm):
    b, lp, _ = proj.shape
    T = HG_CHUNK
    wid = HG_HPS * HG_HD
    ng = HG_HEADS // HG_HPS
    per = wid // LANES

    def col(off):
        return pl.BlockSpec((None, T, wid), lambda bi, gi, ci: (bi, ci, off // per + gi))

    vec = pl.BlockSpec((1, wid), lambda bi, gi, ci: (0, gi))
    return pl.pallas_call(
        _hgrn2_kernel,
        out_shape=jax.ShapeDtypeStruct((b, lp, MIX_W), BF16),
        grid=(b, ng, lp // T),
        in_specs=[col(_HGQ), col(_HGF), col(_HGI), col(_HGG), vec, vec],
        out_specs=pl.BlockSpec((None, T, wid), lambda bi, gi, ci: (bi, ci, gi)),
        scratch_shapes=[pltpu.VMEM((HG_HPS, HG_HD, HG_HD), F32)],
        compiler_params=pltpu.CompilerParams(dimension_semantics=("parallel", "parallel", "arbitrary")),
        name="hgrn2",
    )(proj, proj, proj, proj, lb.reshape(1, MIX_W), gnorm.reshape(1, MIX_W))


def _rglru_kernel(x_ref, y_ref, cw_ref, cb_ref, w_ref, ba_ref, bx_ref, lam_ref, out_ref, xbuf_ref, h_ref):
    c = pl.program_id(2)
    T = RG_T
    G = RG_GROUP

    @pl.when(c == 0)
    def _():
        xbuf_ref[0:8, :] = jnp.zeros((8, G), F32)
        h_ref[...] = jnp.zeros_like(h_ref)

    valid = (c * T + _iota((T, 1), 0)) >= PAD
    x = jnp.where(valid, x_ref[...], 0.0)
    xbuf_ref[8:8 + T, :] = x
    xc = cb_ref[...] + cw_ref[RG_CONV - 1:RG_CONV, :] * x
    for j in range(1, RG_CONV):
        xc = xc + cw_ref[RG_CONV - 1 - j:RG_CONV - j, :] * xbuf_ref[8 - j:8 - j + T, :]
    xbuf_ref[0:8, :] = x[T - 8:T]

    pre = _dot(xc.astype(BF16), w_ref[...])
    r = _sigmoid(pre[:, :G] + ba_ref[...])
    ig = _sigmoid(pre[:, G:] + bx_ref[...])
    lam = lam_ref[...]
    softplus_neg = jnp.maximum(-lam, 0.0) + jnp.log1p(jnp.exp(-jnp.abs(lam)))
    log_a = -RG_C * r * softplus_neg
    a = jnp.exp(log_a)
    th = jnp.tanh(log_a)
    u = jnp.where(valid, jnp.sqrt(-2.0 * th / (1.0 - th)) * (ig * xc), 0.0)

    row = _iota((T, G), 0)
    shift = 1
    while shift < T:
        keep = row >= shift
        u_s = pltpu.roll(u, shift, axis=0)
        a_s = pltpu.roll(a, shift, axis=0)
        u = jnp.where(keep, a * u_s + u, u)
        a = jnp.where(keep, a * a_s, a)
        shift *= 2
    hseq = a * h_ref[...] + u
    h_ref[...] = hseq[T - 1:T]

    y = y_ref[...]
    gelu = 0.5 * y * (1.0 + jnp.tanh(0.7978845608028654 * (y + 0.044715 * (y * y * y))))
    out_ref[...] = (hseq * gelu).astype(out_ref.dtype)


def _rglru(proj, conv_w, conv_b, w_bd, ba, bx, lam):
    b, lp, _ = proj.shape
    T = RG_T
    G = RG_GROUP
    ng = MIX_W // G
    per = G // LANES

    def col(off):
        return pl.BlockSpec((None, T, G), lambda bi, gi, ci: (bi, ci, off // per + gi))

    vec = pl.BlockSpec((1, G), lambda bi, gi, ci: (0, gi))
    return pl.pallas_call(
        _rglru_kernel,
        out_shape=jax.ShapeDtypeStruct((b, lp, MIX_W), BF16),
        grid=(b, ng, lp // T),
        in_specs=[col(_RGX), col(_RGY),
                  pl.BlockSpec((RG_CONV, G), lambda bi, gi, ci: (0, gi)),
                  vec,
                  pl.BlockSpec((None, G, 2 * G), lambda bi, gi, ci: (gi, 0, 0)),
                  vec, vec, vec],
        out_specs=pl.BlockSpec((None, T, G), lambda bi, gi, ci: (bi, ci, gi)),
        scratch_shapes=[pltpu.VMEM((T + 8, G), F32), pltpu.VMEM((1, G), F32)],
        compiler_params=pltpu.CompilerParams(dimension_semantics=("parallel", "parallel", "arbitrary")),
        name="rglru",
    )(proj, proj, conv_w, conv_b.reshape(1, MIX_W), w_bd, ba.reshape(1, MIX_W), bx.reshape(1, MIX_W),
      lam.reshape(1, MIX_W))


def _block_diag_gates(wa, wx):
    per = RG_GROUP // RG_BD
    ng = RG_BLOCKS // per
    eye = jnp.eye(per, dtype=F32)

    def bd(w):
        w4 = w.reshape(ng, per, RG_BD, RG_BD)
        return jnp.einsum("gaij,ab->gaibj", w4, eye).reshape(ng, RG_GROUP, RG_GROUP)

    return jnp.concatenate([bd(wa), bd(wx)], axis=-1).astype(BF16)


def _assemble_kernel(x_ref, front_ref, o_ref):
    r = pl.program_id(1)

    @pl.when(r == 0)
    def _():
        o_ref[...] = front_ref[...]

    @pl.when(r > 0)
    def _():
        o_ref[...] = x_ref[...]


def _assemble(x, front):
    b, seq, d = x.shape
    tr = PAD + N_META
    return pl.pallas_call(
        _assemble_kernel,
        out_shape=jax.ShapeDtypeStruct((b, tr + seq, d), x.dtype),
        grid=(b, 1 + seq // tr),
        in_specs=[pl.BlockSpec((None, tr, d), lambda bi, r: (bi, jnp.maximum(r - 1, 0), 0)),
                  pl.BlockSpec((tr, d), lambda bi, r: (0, 0))],
        out_specs=pl.BlockSpec((None, tr, d), lambda bi, r: (bi, r, 0)),
        compiler_params=pltpu.CompilerParams(dimension_semantics=("parallel", "parallel")),
        name="assemble",
    )(x, front)


def _layer(h, l, p, lower_bounds, tiles):
    b, lp = p["b"], p["lp"]
    m = b * lp
    tm = tiles["tm"]

    w_in = p["w_in"][l]
    b_in = p["b_in"][l]
    if_lo = N_ML
    if_hi = N_ML + 2 * ML_HEADS
    gate_lo = if_hi + N_REST
    w_ml = w_in[:, :N_ML].astype(BF16)
    w_rest = w_in[:, if_hi:gate_lo].astype(BF16)
    w_gate = w_in[:, gate_lo:].astype(BF16)
    w_if = jnp.pad(w_in[:, if_lo:if_hi], ((0, 0), (0, LANES - 2 * ML_HEADS))).astype(BF16)
    b_if = jnp.pad(b_in[if_lo:if_hi], (0, LANES - 2 * ML_HEADS)).reshape(1, LANES)

    xn = _rmsnorm(h, p["norm_mix"][l])
    proj_ml = _linear_bias(xn, w_ml, b_in[:N_ML].reshape(1, N_ML), tm, tiles["tn_mix"], "in_proj_ml")
    proj_rest = _linear_bias(xn, w_rest, b_in[if_hi:gate_lo].reshape(1, N_REST), tm, tiles["tn_mix"],
                             "in_proj_rest")
    proj_ml = proj_ml.reshape(b, lp, N_ML)
    proj_rest = proj_rest.reshape(b, lp, N_REST)
    gates = _linear_bias(xn, w_if, b_if, tm, LANES, "in_proj_gates").reshape(b, lp, LANES)
    gates_t = jnp.swapaxes(gates[:, :, :8], 1, 2)

    branches = (
        _mlstm(proj_ml, gates, gates_t, p["ml_norm"][l]),
        _stick_breaking(proj_rest, p["sb_q_norm"][l], p["sb_k_norm"][l]),
        _hgrn2(proj_rest, lower_bounds[l], p["hg_norm"][l]),
        _rglru(proj_rest, p["rg_conv_w"][l], p["rg_conv_b"][l],
               _block_diag_gates(p["rg_wa"][l], p["rg_wx"][l]),
               p["rg_ba"][l], p["rg_bx"][l], p["rg_lambda"][l]),
    )
    branches = tuple(br.reshape(m, MIX_W) for br in branches)
    merged = _merge(xn, branches, w_gate, b_in[gate_lo:].reshape(1, N_BRANCH * D_MODEL),
                    p["w_up"][l].astype(BF16), tiles["tm_merge"], tiles["tn_merge"])
    h = _linear_res(merged, p["w_out"][l].astype(BF16), h, tm, tiles["tn_out"], D_MODEL, "out_proj")

    d_ff = p["w_ffn_gate"].shape[-1]
    hn = _rmsnorm(h, p["norm_ffn"][l])
    act = _ffn_gate_up(hn, p["w_ffn_gate"][l].astype(BF16), p["w_ffn_up"][l].astype(BF16),
                       tiles["tm_ff"], tiles["tn_ff"])
    h = _linear_res(act, p["w_ffn_down"][l].astype(BF16), h, tm, tiles["tn_down"], d_ff // tiles["nk_down"],
                    "ffn_down")
    return h


def _forward(x, meta, params, tiles):
    b, seq, d = x.shape
    lp = PAD + N_META + seq
    depth = params["w_in"].shape[0]
    hg_lb = params["hg_lb"]
    p_lb = jax.nn.softmax(hg_lb.astype(F32), axis=0)
    lower_bounds = jnp.clip(jnp.cumsum(p_lb, axis=0) - p_lb[0:1], 0.0, 0.999)

    front = jnp.concatenate([jnp.zeros((PAD, d), x.dtype), meta.astype(x.dtype)], axis=0)
    h = _assemble(x, front).reshape(b * lp, d)
    p = dict(params, b=b, lp=lp)
    for l in range(depth):
        h = _layer(h, l, p, lower_bounds, tiles)
    return h.reshape(b, lp, d)[:, PAD + N_META:]


_TILES = dict(tm=1056, tn_mix=1024, tm_merge=768, tn_merge=256, tn_out=512,
              tm_ff=2112, tn_ff=256, tn_down=512, nk_down=2)


def kernel(x, meta, norm_mix, norm_ffn, w_in, b_in, ml_norm, sb_q_norm, sb_k_norm, hg_lb, hg_norm,
           rg_conv_w, rg_conv_b, rg_wa, rg_ba, rg_wx, rg_bx, rg_lambda, w_up, w_out,
           w_ffn_gate, w_ffn_up, w_ffn_down):
    params = dict(norm_mix=norm_mix, norm_ffn=norm_ffn, w_in=w_in, b_in=b_in, ml_norm=ml_norm,
                  sb_q_norm=sb_q_norm, sb_k_norm=sb_k_norm, hg_lb=hg_lb, hg_norm=hg_norm,
                  rg_conv_w=rg_conv_w, rg_conv_b=rg_conv_b, rg_wa=rg_wa, rg_ba=rg_ba, rg_wx=rg_wx,
                  rg_bx=rg_bx, rg_lambda=rg_lambda, w_up=w_up, w_out=w_out,
                  w_ffn_gate=w_ffn_gate, w_ffn_up=w_ffn_up, w_ffn_down=w_ffn_down)
    return _forward(x, meta, params, _TILES)
```

```python
import functools

import jax
import jax.numpy as jnp
from jax import lax
from jax.experimental import pallas as pl
from jax.experimental.pallas import tpu as pltpu

F32 = jnp.float32
BF16 = jnp.bfloat16

D_MODEL = 4096
N_META = 16
N_BRANCH = 4
MIX_W = D_MODEL // 4
ML_HEADS = 4
ML_HD = MIX_W // ML_HEADS
SB_HEADS = 8
SB_HD = MIX_W // SB_HEADS
HG_HEADS = 8
HG_HD = MIX_W // HG_HEADS
RG_BLOCKS = 16
RG_BD = MIX_W // RG_BLOCKS
RG_CONV = 4
RG_C = 8.0
EPS = 1e-6
NEG = -1e30

LANES = 128
PAD = LANES - N_META
ML_CHUNK = 128
SB_BLOCK = 128
SB_SPAN = 256
SB_HPS = 4
HG_CHUNK = 64
HG_SUB = 8
HG_HPS = 4
RG_GROUP = 256
RG_T = 128
VMEM_LIMIT = 56 * 1024 * 1024

_SBQ, _SBK, _SBV = 0, 8, 16
_HGQ, _HGF, _HGI, _HGG = 24, 32, 40, 48
_RGX, _RGY = 56, 64
N_ML = 4 * MIX_W
N_REST = 9 * MIX_W


def _sigmoid(x):
    return 1.0 / (1.0 + jnp.exp(-x))


def _log_sigmoid(x):
    return jnp.minimum(x, 0.0) - jnp.log1p(jnp.exp(-jnp.abs(x)))


def _split3(x):
    hi = x.astype(BF16)
    r1 = x - hi.astype(F32)
    mid = r1.astype(BF16)
    lo = (r1 - mid.astype(F32)).astype(BF16)
    return hi, mid, lo


def _dot(a, b):
    return jnp.dot(a, b, preferred_element_type=F32)


def _dot_nt(a, b):
    return lax.dot_general(a, b, (((1,), (1,)), ((), ())), preferred_element_type=F32)


def _dot_tn(a, b):
    return lax.dot_general(a, b, (((0,), (0,)), ((), ())), preferred_element_type=F32)


def _tri_left(x, tri):
    hi, mid, lo = _split3(x)
    return _dot(tri, hi) + _dot(tri, mid) + _dot(tri, lo)


def _tri_right(x, tri):
    hi, mid, lo = _split3(x)
    return _dot(hi, tri) + _dot(mid, tri) + _dot(lo, tri)


def _iota(shape, dim):
    return lax.broadcasted_iota(jnp.int32, shape, dim)


def _rmsnorm_kernel(x_ref, g_ref, o_ref):
    x = x_ref[...]
    y = x * lax.rsqrt(jnp.mean(x * x, axis=-1, keepdims=True) + EPS)
    o_ref[...] = (y * g_ref[...]).astype(o_ref.dtype)


def _rmsnorm(x, g, tm=256):
    m, d = x.shape
    return pl.pallas_call(
        _rmsnorm_kernel,
        out_shape=jax.ShapeDtypeStruct((m, d), BF16),
        grid=(m // tm,),
        in_specs=[pl.BlockSpec((tm, d), lambda i: (i, 0)),
                  pl.BlockSpec((1, d), lambda i: (0, 0))],
        out_specs=pl.BlockSpec((tm, d), lambda i: (i, 0)),
        compiler_params=pltpu.CompilerParams(dimension_semantics=("parallel",)),
        name="rmsnorm",
    )(x, g.reshape(1, d))


def _linear_bias_kernel(x_ref, w_ref, b_ref, o_ref):
    o_ref[...] = _dot(x_ref[...], w_ref[...]) + b_ref[...]


def _linear_bias(x, w, l, b, tm, tn, name):
    m, k = x.shape
    n = b.shape[1]
    return pl.pallas_call(
        _linear_bias_kernel,
        out_shape=jax.ShapeDtypeStruct((m, n), F32),
        grid=(m // tm, n // tn),
        in_specs=[pl.BlockSpec((tm, k), lambda i, j: (i, 0)),
                  pl.BlockSpec((None, k, tn), lambda i, j: (l, 0, j)),
                  pl.BlockSpec((1, tn), lambda i, j: (0, j))],
        out_specs=pl.BlockSpec((tm, tn), lambda i, j: (i, j)),
        compiler_params=pltpu.CompilerParams(
            dimension_semantics=("parallel", "parallel"), vmem_limit_bytes=VMEM_LIMIT),
        name=name,
    )(x, w, b)


def _linear_res_kernel(x_ref, w_ref, r_ref, o_ref, *, nk):
    part = _dot(x_ref[...], w_ref[...])
    if nk == 1:
        o_ref[...] = r_ref[...] + part
    else:
        k = pl.program_id(2)

        @pl.when(k == 0)
        def _():
            o_ref[...] = r_ref[...] + part

        @pl.when(k > 0)
        def _():
            o_ref[...] += part


def _linear_res(x, w, l, res, tm, tn, tk, name):
    m, k = x.shape
    n = w.shape[2]
    nk = k // tk
    return pl.pallas_call(
        functools.partial(_linear_res_kernel, nk=nk),
        out_shape=jax.ShapeDtypeStruct((m, n), F32),
        grid=(m // tm, n // tn, nk),
        in_specs=[pl.BlockSpec((tm, tk), lambda i, j, kk: (i, kk)),
                  pl.BlockSpec((None, tk, tn), lambda i, j, kk: (l, kk, j)),
                  pl.BlockSpec((tm, tn), lambda i, j, kk: (i, j))],
        out_specs=pl.BlockSpec((tm, tn), lambda i, j, kk: (i, j)),
        compiler_params=pltpu.CompilerParams(
            dimension_semantics=("parallel", "parallel", "arbitrary"), vmem_limit_bytes=VMEM_LIMIT),
        name=name,
    )(x, w, res)


def _merge_kernel(xn_ref, b0_ref, b1_ref, b2_ref, b3_ref, g0_ref, g1_ref, g2_ref, g3_ref,
                  c0_ref, c1_ref, c2_ref, c3_ref, wu_ref, o_ref):
    xn = xn_ref[...]
    acc = None
    branches = (b0_ref, b1_ref, b2_ref, b3_ref)
    gate_w = (g0_ref, g1_ref, g2_ref, g3_ref)
    gate_b = (c0_ref, c1_ref, c2_ref, c3_ref)
    for kb in range(N_BRANCH):
        g = _dot(xn, gate_w[kb][...]) + gate_b[kb][...]
        u = _dot(branches[kb][...], wu_ref[kb])
        t = _sigmoid(g) * u
        acc = t if acc is None else acc + t
    o_ref[...] = acc.astype(o_ref.dtype)


def _merge(xn, branches, w_gate, b_gate, w_up, l, tm, tn):
    m, d = xn.shape
    nj = d // tn
    br_spec = pl.BlockSpec((tm, MIX_W), lambda i, j: (i, 0))

    def gw(kb):
        return pl.BlockSpec((d, tn), lambda i, j: (0, kb * nj + j))

    def gb(kb):
        return pl.BlockSpec((1, tn), lambda i, j: (0, kb * nj + j))

    return pl.pallas_call(
        _merge_kernel,
        out_shape=jax.ShapeDtypeStruct((m, d), BF16),
        grid=(m // tm, nj),
        in_specs=[pl.BlockSpec((tm, d), lambda i, j: (i, 0)),
                  br_spec, br_spec, br_spec, br_spec,
                  gw(0), gw(1), gw(2), gw(3), gb(0), gb(1), gb(2), gb(3),
                  pl.BlockSpec((None, N_BRANCH, MIX_W, tn), lambda i, j: (l, 0, 0, j))],
        out_specs=pl.BlockSpec((tm, tn), lambda i, j: (i, j)),
        compiler_params=pltpu.CompilerParams(
            dimension_semantics=("parallel", "parallel"), vmem_limit_bytes=VMEM_LIMIT),
        name="merge",
    )(xn, *branches, w_gate, w_gate, w_gate, w_gate, b_gate, b_gate, b_gate, b_gate, w_up)


def _ffn_gu_kernel(x_ref, wg_ref, wu_ref, o_ref):
    x = x_ref[...]
    g = _dot(x, wg_ref[...])
    u = _dot(x, wu_ref[...])
    o_ref[...] = (g * _sigmoid(g) * u).astype(o_ref.dtype)


def _ffn_gate_up(x, wg, wu, l, tm, tn):
    m, k = x.shape
    n = wg.shape[2]
    return pl.pallas_call(
        _ffn_gu_kernel,
        out_shape=jax.ShapeDtypeStruct((m, n), BF16),
        grid=(m // tm, n // tn),
        in_specs=[pl.BlockSpec((tm, k), lambda i, j: (i, 0)),
                  pl.BlockSpec((None, k, tn), lambda i, j: (l, 0, j)),
                  pl.BlockSpec((None, k, tn), lambda i, j: (l, 0, j))],
        out_specs=pl.BlockSpec((tm, tn), lambda i, j: (i, j)),
        compiler_params=pltpu.CompilerParams(
            dimension_semantics=("parallel", "parallel"), vmem_limit_bytes=VMEM_LIMIT),
        name="ffn_gate_up",
    )(x, wg, wu)


def _mlstm_kernel(q_ref, k_ref, v_ref, o_ref, gc_ref, gr_ref, gn_ref, out_ref, c_ref, n_ref, m_ref):
    c = pl.program_id(1)
    T = ML_CHUNK

    @pl.when(c == 0)
    def _():
        c_ref[...] = jnp.zeros_like(c_ref)
        n_ref[...] = jnp.zeros_like(n_ref)
        m_ref[...] = jnp.zeros_like(m_ref)

    valid_col = (c * T + _iota((T, 1), 0)) >= PAD
    valid_row = (c * T + _iota((1, T), 1)) >= PAD
    gc = gc_ref[...]
    gr = gr_ref[...]
    tt = _iota((T, T), 0)
    ss = _iota((T, T), 1)
    causal = ss <= tt
    tril = jnp.where(causal, 1.0, 0.0).astype(BF16)
    triu = jnp.where(tt <= ss, 1.0, 0.0).astype(BF16)
    bc_all = _tri_left(jnp.where(valid_col, _log_sigmoid(gc), 0.0), tril)
    br_all = _tri_right(jnp.where(valid_row, _log_sigmoid(gr), 0.0), triu)

    heads = range(ML_HEADS)
    sl = [slice(h * ML_HD, (h + 1) * ML_HD) for h in heads]
    q = [q_ref[:, sl[h]] * (ML_HD ** -0.5) for h in heads]
    k = [k_ref[:, sl[h]] for h in heads]
    qb = [q[h].astype(BF16) for h in heads]
    vb = [v_ref[:, sl[h]].astype(BF16) for h in heads]
    qk = [_dot_nt(qb[h], k[h].astype(BF16)) for h in heads]
    c_prev = [c_ref[h] for h in heads]
    qc = [_dot(qb[h], c_prev[h].astype(BF16)) for h in heads]

    li_col = [jnp.where(valid_col, gc[:, h:h + 1], NEG) for h in heads]
    li_row = [jnp.where(valid_row, gr[h:h + 1, :], NEG) for h in heads]
    b_col = [bc_all[:, ML_HEADS + h:ML_HEADS + h + 1] for h in heads]
    b_row = [br_all[ML_HEADS + h:ML_HEADS + h + 1, :] for h in heads]
    m_prev = [m_ref[:, h:h + 1] for h in heads]
    dmat = [jnp.where(causal, b_col[h] - b_row[h] + li_row[h], NEG) for h in heads]
    g_col = [b_col[h] + m_prev[h] for h in heads]
    m_out = [jnp.maximum(g_col[h], jnp.max(dmat[h], axis=1, keepdims=True)) for h in heads]

    b_end = [b_row[h][:, T - 1:T] for h in heads]
    wlog = [b_end[h] - b_col[h] + li_col[h] for h in heads]
    m_new = [jnp.maximum(b_end[h] + m_prev[h], jnp.max(wlog[h], axis=0, keepdims=True)) for h in heads]
    decay = [jnp.exp(b_end[h] + m_prev[h] - m_new[h]) for h in heads]
    kw = [k[h] * jnp.exp(wlog[h] - m_new[h]) for h in heads]
    kv = [_dot(kw[h].T.astype(BF16), vb[h]) for h in heads]

    s = [qk[h] * jnp.exp(dmat[h] - m_out[h]) for h in heads]
    inter = [jnp.exp(g_col[h] - m_out[h]) for h in heads]
    sv = [_dot(s[h].astype(BF16), vb[h]) for h in heads]

    for h in heads:
        num = sv[h] + inter[h] * qc[h]
        den = (jnp.sum(s[h], axis=1, keepdims=True)
               + inter[h] * jnp.sum(q[h] * n_ref[h], axis=1, keepdims=True))
        hh = num / jnp.maximum(jnp.abs(den), jnp.exp(-m_out[h]))
        y = hh * lax.rsqrt(jnp.mean(hh * hh, axis=-1, keepdims=True) + EPS) * gn_ref[:, sl[h]]
        out_ref[:, sl[h]] = (y * _sigmoid(o_ref[:, sl[h]])).astype(out_ref.dtype)
        c_ref[h] = decay[h] * c_prev[h] + kv[h]
        n_ref[h] = decay[h] * n_ref[h] + jnp.sum(kw[h], axis=0, keepdims=True)
        m_ref[:, h:h + 1] = m_new[h]


def _mlstm(proj, gates, gates_t, gnorm):
    b, lp, _ = proj.shape
    T = ML_CHUNK

    def col(idx):
        return pl.BlockSpec((None, T, MIX_W), lambda bi, ci: (bi, ci, idx))

    return pl.pallas_call(
        _mlstm_kernel,
        out_shape=jax.ShapeDtypeStruct((b, lp, MIX_W), BF16),
        grid=(b, lp // T),
        in_specs=[col(0), col(1), col(2), col(3),
                  pl.BlockSpec((None, T, LANES), lambda bi, ci: (bi, ci, 0)),
                  pl.BlockSpec((None, 8, T), lambda bi, ci: (bi, 0, ci)),
                  pl.BlockSpec((1, MIX_W), lambda bi, ci: (0, 0))],
        out_specs=pl.BlockSpec((None, T, MIX_W), lambda bi, ci: (bi, ci, 0)),
        scratch_shapes=[pltpu.VMEM((ML_HEADS, ML_HD, ML_HD), F32),
                        pltpu.VMEM((ML_HEADS, 1, ML_HD), F32),
                        pltpu.VMEM((1, LANES), F32)],
        compiler_params=pltpu.CompilerParams(dimension_semantics=("parallel", "arbitrary")),
        name="mlstm",
    )(proj, proj, proj, proj, gates, gates_t, gnorm.reshape(1, MIX_W))


def _sb_kernel(q_ref, k_ref, v_ref, gq_ref, gk_ref, out_ref, kn_ref, vb_ref, acc_ref, run_ref, qn_ref, tri_ref,
               z_ref, a_ref):
    qi = pl.program_id(2)
    T = SB_BLOCK
    W = SB_SPAN
    row0 = pl.multiple_of(qi * T, T)

    @pl.when(jnp.bitwise_and(qi, 1) == 0)
    def _():
        kn_ref[pl.ds(row0 + T, T), :] = jnp.zeros((T, kn_ref.shape[1]), BF16)
        vb_ref[pl.ds(row0 + T, T), :] = jnp.zeros((T, vb_ref.shape[1]), BF16)

    @pl.when(qi == 0)
    def _():
        jj = jnp.bitwise_and(_iota((2 * W, W), 0), W - 1)
        tri_ref[...] = jnp.where(jj > _iota((2 * W, W), 1), 1.0, 0.0).astype(BF16)

    heads = range(SB_HPS)
    sl = [slice(h * SB_HD, (h + 1) * SB_HD) for h in heads]
    vb_ref[pl.ds(row0, T), :] = v_ref[pl.ds(row0, T), :].astype(BF16)
    for h in heads:
        kh = k_ref[pl.ds(row0, T), sl[h]]
        kn = kh * lax.rsqrt(jnp.mean(kh * kh, axis=-1, keepdims=True) + EPS) * gk_ref[...]
        kn_ref[pl.ds(row0, T), sl[h]] = kn.astype(BF16)
        qh = q_ref[:, sl[h]]
        qs = qh * lax.rsqrt(jnp.mean(qh * qh, axis=-1, keepdims=True) + EPS) * (gq_ref[...] * SB_HD ** -0.5)
        qn_ref[:, sl[h]] = qs.astype(BF16)

    acc_ref[...] = jnp.zeros_like(acc_ref)
    run_ref[...] = jnp.zeros_like(run_ref)
    t_pos = row0 + _iota((T, W), 0)
    lane = _iota((T, W), 1)
    top = lax.shift_right_logical(qi, 1)

    def scores(col0, slot):
        for h in heads:
            z_ref[slot, h] = _dot_nt(qn_ref[:, sl[h]], kn_ref[pl.ds(col0, W), sl[h]])

    def weighted_values(col0):
        for h in heads:
            acc_ref[h] += _dot(a_ref[h], vb_ref[pl.ds(col0, W), sl[h]])

    def span_step(it, masked, has_prev):
        col0 = pl.multiple_of((top - it) * W, W)
        slot = it & 1
        z = [z_ref[slot, h] for h in heads]
        if has_prev:
            weighted_values(pl.multiple_of(col0 + W, W))
        l1p = [jnp.log(1.0 + jnp.exp(-jnp.abs(z[h]))) for h in heads]
        ls = [jnp.minimum(z[h], 0.0) - l1p[h] for h in heads]
        lk = [ls[h] - z[h] for h in heads]
        if masked:
            s_pos = col0 + lane
            vis = jnp.logical_and(s_pos < t_pos, s_pos >= PAD)
            lk = [jnp.where(vis, lk[h], 0.0) for h in heads]
        lk_hi = [lk[h].astype(BF16) for h in heads]
        lk_lo = [(lk[h] - lk_hi[h].astype(F32)).astype(BF16) for h in heads]
        cs = [_dot(jnp.concatenate([lk_hi[h], lk_lo[h]], axis=1), tri_ref[...]) for h in heads]
        scores(pl.multiple_of(jnp.maximum(top - it - 1, 0) * W, W), 1 - slot)
        a = [jnp.exp(ls[h] + cs[h] + run_ref[h]) for h in heads]
        if masked:
            a = [jnp.where(vis, a[h], 0.0) for h in heads]
        for h in heads:
            a_ref[h] = a[h].astype(BF16)
            run_ref[h] += jnp.sum(lk[h], axis=1, keepdims=True)

    scores(pl.multiple_of(top * W, W), 0)
    span_step(0, True, False)

    def body(it, carry):
        span_step(it, False, True)
        return carry

    lax.fori_loop(1, top, body, 0)

    @pl.when(top > 0)
    def _():
        span_step(top, True, True)

    weighted_values(0)
    for h in heads:
        out_ref[:, sl[h]] = acc_ref[h].astype(out_ref.dtype)


def _stick_breaking(proj, gq, gk):
    b, lp, _ = proj.shape
    T = SB_BLOCK
    wid = SB_HPS * SB_HD
    ng = SB_HEADS // SB_HPS
    per = wid // LANES
    return pl.pallas_call(
        _sb_kernel,
        out_shape=jax.ShapeDtypeStruct((b, lp, MIX_W), BF16),
        grid=(b, ng, lp // T),
        in_specs=[pl.BlockSpec((None, T, wid), lambda bi, gi, qi: (bi, qi, _SBQ // per + gi)),
                  pl.BlockSpec((None, lp, wid), lambda bi, gi, qi: (bi, 0, _SBK // per + gi),
                               pipeline_mode=pl.Buffered(1)),
                  pl.BlockSpec((None, lp, wid), lambda bi, gi, qi: (bi, 0, _SBV // per + gi),
                               pipeline_mode=pl.Buffered(1)),
                  pl.BlockSpec((1, SB_HD), lambda bi, gi, qi: (0, 0)),
                  pl.BlockSpec((1, SB_HD), lambda bi, gi, qi: (0, 0))],
        out_specs=pl.BlockSpec((None, T, wid), lambda bi, gi, qi: (bi, qi, gi)),
        scratch_shapes=[pltpu.VMEM((lp + SB_SPAN - T, wid), BF16),
                        pltpu.VMEM((lp + SB_SPAN - T, wid), BF16),
                        pltpu.VMEM((SB_HPS, T, SB_HD), F32),
                        pltpu.VMEM((SB_HPS, T, 1), F32),
                        pltpu.VMEM((T, wid), BF16),
                        pltpu.VMEM((2 * SB_SPAN, SB_SPAN), BF16),
                        pltpu.VMEM((2, SB_HPS, T, SB_SPAN), F32),
                        pltpu.VMEM((SB_HPS, T, SB_SPAN), BF16)],
        compiler_params=pltpu.CompilerParams(
            dimension_semantics=("parallel", "parallel", "arbitrary"), vmem_limit_bytes=VMEM_LIMIT),
        name="stick_breaking",
    )(proj, proj, proj, gq.reshape(1, SB_HD), gk.reshape(1, SB_HD))


def _hgrn2_kernel(q_ref, f_ref, i_ref, g_ref, lb_ref, gn_ref, out_ref, st_ref):
    c = pl.program_id(2)
    T = HG_CHUNK
    S8 = HG_SUB

    @pl.when(c == 0)
    def _():
        st_ref[...] = jnp.zeros_like(st_ref)

    valid = (c * T + _iota((T, 1), 0)) >= PAD
    tril = jnp.where(_iota((T, T), 1) <= _iota((T, T), 0), 1.0, 0.0).astype(BF16)
    lane8 = _iota((S8, T), 1)
    row8 = _iota((S8, T), 0)

    heads = range(HG_HPS)
    hs = [slice(h * HG_HD, (h + 1) * HG_HD) for h in heads]
    lb = [lb_ref[:, hs[h]] for h in heads]
    sig = [_sigmoid(f_ref[:, hs[h]]) for h in heads]
    logf = [jnp.where(valid, jnp.log(lb[h] + (1.0 - lb[h]) * sig[h]), 0.0) for h in heads]
    kk = [jnp.where(valid, (1.0 - lb[h]) * (1.0 - sig[h]), 0.0) for h in heads]
    q = [q_ref[:, hs[h]] * _sigmoid(q_ref[:, hs[h]]) for h in heads]
    iv = [i_ref[:, hs[h]].astype(BF16) for h in heads]
    b = [_tri_left(logf[h], tril) for h in heads]

    st = [st_ref[h] for h in heads]
    b_end = [b[h][T - 1:T] for h in heads]
    o_inter = [_dot_nt((q[h] * jnp.exp(b[h])).astype(BF16), st[h].astype(BF16)) for h in heads]
    st_add = [_dot_tn(iv[h], (kk[h] * jnp.exp(b_end[h] - b[h])).astype(BF16)) for h in heads]

    def diag_block(h, blk):
        r0 = blk * S8
        qb = q[h][r0:r0 + S8]
        bb = b[h][r0:r0 + S8]
        diag = jnp.zeros((S8, T), F32)
        for j in range(S8):
            s_idx = r0 + j
            p = qb * kk[h][s_idx:s_idx + 1] * jnp.exp(jnp.minimum(bb - b[h][s_idx:s_idx + 1], 0.0))
            diag = jnp.where(lane8 == s_idx, jnp.sum(p, axis=1, keepdims=True), diag)
        return jnp.where(lane8 - r0 <= row8, diag, 0.0)

    def off_block(h, blk):
        r0 = blk * S8
        bref = b[h][r0 - 1:r0]
        qs = (q[h][r0:r0 + S8] * jnp.exp(b[h][r0:r0 + S8] - bref)).astype(BF16)
        ks = (kk[h] * jnp.exp(jnp.minimum(bref - b[h], 0.0))).astype(BF16)
        return _dot_nt(qs, ks)

    nblk = T // S8
    off = [[off_block(h, blk) for h in heads] for blk in range(1, nblk)]
    att = []
    for h in heads:
        rows = [diag_block(h, 0)]
        for blk in range(1, nblk):
            rows.append(jnp.where(lane8 < blk * S8, off[blk - 1][h], diag_block(h, blk)))
        att.append(jnp.concatenate(rows, axis=0).astype(BF16))
    o_intra = [_dot(att[h], iv[h]) for h in heads]

    for h in heads:
        o = o_intra[h] + o_inter[h]
        st_ref[h] = st[h] * jnp.exp(b_end[h]) + st_add[h]
        y = o * lax.rsqrt(jnp.mean(o * o, axis=-1, keepdims=True) + EPS) * gn_ref[:, hs[h]]
        out_ref[:, hs[h]] = (y * _sigmoid(g_ref[:, hs[h]])).astype(out_ref.dtype)


def _hgrn2(proj, lb, gnorm):
    b, lp, _ = proj.shape
    T = HG_CHUNK
    wid = HG_HPS * HG_HD
    ng = HG_HEADS // HG_HPS
    per = wid // LANES

    def col(off):
        return pl.BlockSpec((None, T, wid), lambda bi, gi, ci: (bi, ci, off // per + gi))

    vec = pl.BlockSpec((1, wid), lambda bi, gi, ci: (0, gi))
    return pl.pallas_call(
        _hgrn2_kernel,
        out_shape=jax.ShapeDtypeStruct((b, lp, MIX_W), BF16),
        grid=(b, ng, lp // T),
        in_specs=[col(_HGQ), col(_HGF), col(_HGI), col(_HGG), vec, vec],
        out_specs=pl.BlockSpec((None, T, wid), lambda bi, gi, ci: (bi, ci, gi)),
        scratch_shapes=[pltpu.VMEM((HG_HPS, HG_HD, HG_HD), F32)],
        compiler_params=pltpu.CompilerParams(dimension_semantics=("parallel", "parallel", "arbitrary")),
        name="hgrn2",
    )(proj, proj, proj, proj, lb.reshape(1, MIX_W), gnorm.reshape(1, MIX_W))


def _rglru_kernel(x_ref, y_ref, cw_ref, cb_ref, w_ref, ba_ref, bx_ref, lam_ref, out_ref, xbuf_ref, h_ref):
    c = pl.program_id(2)
    T = RG_T
    G = RG_GROUP

    @pl.when(c == 0)
    def _():
        xbuf_ref[0:8, :] = jnp.zeros((8, G), F32)
        h_ref[...] = jnp.zeros_like(h_ref)

    valid = (c * T + _iota((T, 1), 0)) >= PAD
    x = jnp.where(valid, x_ref[...], 0.0)
    xbuf_ref[8:8 + T, :] = x
    xc = cb_ref[...] + cw_ref[RG_CONV - 1:RG_CONV, :] * x
    for j in range(1, RG_CONV):
        xc = xc + cw_ref[RG_CONV - 1 - j:RG_CONV - j, :] * xbuf_ref[8 - j:8 - j + T, :]
    xbuf_ref[0:8, :] = x[T - 8:T]

    pre = _dot(xc.astype(BF16), w_ref[...])
    r = _sigmoid(pre[:, :G] + ba_ref[...])
    ig = _sigmoid(pre[:, G:] + bx_ref[...])
    lam = lam_ref[...]
    softplus_neg = jnp.maximum(-lam, 0.0) + jnp.log1p(jnp.exp(-jnp.abs(lam)))
    log_a = -RG_C * r * softplus_neg
    a = jnp.exp(log_a)
    th = jnp.tanh(log_a)
    u = jnp.where(valid, jnp.sqrt(-2.0 * th / (1.0 - th)) * (ig * xc), 0.0)

    row = _iota((T, G), 0)
    shift = 1
    while shift < T:
        keep = row >= shift
        u_s = pltpu.roll(u, shift, axis=0)
        a_s = pltpu.roll(a, shift, axis=0)
        u = jnp.where(keep, a * u_s + u, u)
        a = jnp.where(keep, a * a_s, a)
        shift *= 2
    hseq = a * h_ref[...] + u
    h_ref[...] = hseq[T - 1:T]

    y = y_ref[...]
    gelu = 0.5 * y * (1.0 + jnp.tanh(0.7978845608028654 * (y + 0.044715 * (y * y * y))))
    out_ref[...] = (hseq * gelu).astype(out_ref.dtype)


def _rglru(proj, conv_w, conv_b, w_bd, ba, bx, lam):
    b, lp, _ = proj.shape
    T = RG_T
    G = RG_GROUP
    ng = MIX_W // G
    per = G // LANES

    def col(off):
        return pl.BlockSpec((None, T, G), lambda bi, gi, ci: (bi, ci, off // per + gi))

    vec = pl.BlockSpec((1, G), lambda bi, gi, ci: (0, gi))
    return pl.pallas_call(
        _rglru_kernel,
        out_shape=jax.ShapeDtypeStruct((b, lp, MIX_W), BF16),
        grid=(b, ng, lp // T),
        in_specs=[col(_RGX), col(_RGY),
                  pl.BlockSpec((RG_CONV, G), lambda bi, gi, ci: (0, gi)),
                  vec,
                  pl.BlockSpec((None, G, 2 * G), lambda bi, gi, ci: (gi, 0, 0)),
                  vec, vec, vec],
        out_specs=pl.BlockSpec((None, T, G), lambda bi, gi, ci: (bi, ci, gi)),
        scratch_shapes=[pltpu.VMEM((T + 8, G), F32), pltpu.VMEM((1, G), F32)],
        compiler_params=pltpu.CompilerParams(dimension_semantics=("parallel", "parallel", "arbitrary")),
        name="rglru",
    )(proj, proj, conv_w, conv_b.reshape(1, MIX_W), w_bd, ba.reshape(1, MIX_W), bx.reshape(1, MIX_W),
      lam.reshape(1, MIX_W))


def _block_diag_gates(wa, wx):
    per = RG_GROUP // RG_BD
    ng = RG_BLOCKS // per
    eye = jnp.eye(per, dtype=F32)

    def bd(w):
        w4 = w.reshape(ng, per, RG_BD, RG_BD)
        return jnp.einsum("gaij,ab->gaibj", w4, eye).reshape(ng, RG_GROUP, RG_GROUP)

    return jnp.concatenate([bd(wa), bd(wx)], axis=-1).astype(BF16)


def _assemble_kernel(x_ref, front_ref, o_ref):
    r = pl.program_id(1)

    @pl.when(r == 0)
    def _():
        o_ref[...] = front_ref[...]

    @pl.when(r > 0)
    def _():
        o_ref[...] = x_ref[...]


def _assemble(x, front):
    b, seq, d = x.shape
    tr = PAD + N_META
    return pl.pallas_call(
        _assemble_kernel,
        out_shape=jax.ShapeDtypeStruct((b, tr + seq, d), x.dtype),
        grid=(b, 1 + seq // tr),
        in_specs=[pl.BlockSpec((None, tr, d), lambda bi, r: (bi, jnp.maximum(r - 1, 0), 0)),
                  pl.BlockSpec((tr, d), lambda bi, r: (0, 0))],
        out_specs=pl.BlockSpec((None, tr, d), lambda bi, r: (bi, r, 0)),
        compiler_params=pltpu.CompilerParams(dimension_semantics=("parallel", "parallel")),
        name="assemble",
    )(x, front)


def _layer(h, l, p, lower_bounds, tiles):
    b, lp = p["b"], p["lp"]
    m = b * lp
    tm = tiles["tm"]

    w_in = p["w_in"]
    b_in = p["b_in"][l]
    if_lo = N_ML
    if_hi = N_ML + 2 * ML_HEADS
    gate_lo = if_hi + N_REST
    w_rest = w_in[l, :, if_hi:gate_lo][None]
    w_gate = w_in[l, :, gate_lo:]
    w_if = jnp.pad(w_in[l, :, if_lo:if_hi], ((0, 0), (0, LANES - 2 * ML_HEADS)))[None]
    b_if = jnp.pad(b_in[if_lo:if_hi], (0, LANES - 2 * ML_HEADS)).reshape(1, LANES)

    xn = _rmsnorm(h, p["norm_mix"][l])
    proj_ml = _linear_bias(xn, w_in, l, b_in[:N_ML].reshape(1, N_ML), tm, tiles["tn_mix"], "in_proj_ml")
    proj_rest = _linear_bias(xn, w_rest, 0, b_in[if_hi:gate_lo].reshape(1, N_REST), tm, tiles["tn_mix"],
                             "in_proj_rest")
    proj_ml = proj_ml.reshape(b, lp, N_ML)
    proj_rest = proj_rest.reshape(b, lp, N_REST)
    gates = _linear_bias(xn, w_if, 0, b_if, tm, LANES, "in_proj_gates").reshape(b, lp, LANES)
    gates_t = jnp.swapaxes(gates[:, :, :8], 1, 2)

    branches = (
        _mlstm(proj_ml, gates, gates_t, p["ml_norm"][l]),
        _stick_breaking(proj_rest, p["sb_q_norm"][l], p["sb_k_norm"][l]),
        _hgrn2(proj_rest, lower_bounds[l], p["hg_norm"][l]),
        _rglru(proj_rest, p["rg_conv_w"][l], p["rg_conv_b"][l],
               _block_diag_gates(p["rg_wa"][l], p["rg_wx"][l]),
               p["rg_ba"][l], p["rg_bx"][l], p["rg_lambda"][l]),
    )
    branches = tuple(br.reshape(m, MIX_W) for br in branches)
    merged = _merge(xn, branches, w_gate, b_in[gate_lo:].reshape(1, N_BRANCH * D_MODEL),
                    p["w_up"], l, tiles["tm_merge"], tiles["tn_merge"])
    h = _linear_res(merged, p["w_out"], l, h, tm, tiles["tn_out"], D_MODEL, "out_proj")

    d_ff = p["w_ffn_gate"].shape[-1]
    hn = _rmsnorm(h, p["norm_ffn"][l])
    act = _ffn_gate_up(hn, p["w_ffn_gate"], p["w_ffn_up"], l, tiles["tm_ff"], tiles["tn_ff"])
    h = _linear_res(act, p["w_ffn_down"], l, h, tiles["tm_down"], tiles["tn_down"], d_ff // tiles["nk_down"],
                    "ffn_down")
    return h


def _forward(x, meta, params, tiles):
    b, seq, d = x.shape
    lp = PAD + N_META + seq
    depth = params["w_in"].shape[0]
    hg_lb = params["hg_lb"]
    p_lb = jax.nn.softmax(hg_lb.astype(F32), axis=0)
    lower_bounds = jnp.clip(jnp.cumsum(p_lb, axis=0) - p_lb[0:1], 0.0, 0.999)

    front = jnp.concatenate([jnp.zeros((PAD, d), x.dtype), meta.astype(x.dtype)], axis=0)
    h = _assemble(x, front).reshape(b * lp, d)
    p = dict(params, b=b, lp=lp)
    for name in ("w_in", "w_up", "w_out", "w_ffn_gate", "w_ffn_up", "w_ffn_down"):
        p[name] = params[name].astype(BF16)
    for l in range(depth):
        h = _layer(h, l, p, lower_bounds, tiles)
    return h.reshape(b, lp, d)[:, PAD + N_META:]


_TILES = dict(tm=1056, tn_mix=1024, tm_merge=768, tn_merge=256, tn_out=512,
              tm_ff=2112, tn_ff=256, tm_down=768, tn_down=256, nk_down=1)


def kernel(x, meta, norm_mix, norm_ffn, w_in, b_in, ml_norm, sb_q_norm, sb_k_norm, hg_lb, hg_norm,
           rg_conv_w, rg_conv_b, rg_wa, rg_ba, rg_wx, rg_bx, rg_lambda, w_up, w_out,
           w_ffn_gate, w_ffn_up, w_ffn_down):
    params = dict(norm_mix=norm_mix, norm_ffn=norm_ffn, w_in=w_in, b_in=b_in, ml_norm=ml_norm,
                  sb_q_norm=sb_q_norm, sb_k_norm=sb_k_norm, hg_lb=hg_lb, hg_norm=hg_norm,
                  rg_conv_w=rg_conv_w, rg_conv_b=rg_conv_b, rg_wa=rg_wa, rg_ba=rg_ba, rg_wx=rg_wx,
                  rg_bx=rg_bx, rg_lambda=rg_lambda, w_up=w_up, w_out=w_out,
                  w_ffn_gate=w_ffn_gate, w_ffn_up=w_ffn_up, w_ffn_down=w_ffn_down)
    return _forward(x, meta, params, _TILES)
```

```python
import functools

import jax
import jax.numpy as jnp
from jax import lax
from jax.experimental import pallas as pl
from jax.experimental.pallas import tpu as pltpu

F32 = jnp.float32
BF16 = jnp.bfloat16

D_MODEL = 4096
N_META = 16
N_BRANCH = 4
MIX_W = D_MODEL // 4
ML_HEADS = 4
ML_HD = MIX_W // ML_HEADS
SB_HEADS = 8
SB_HD = MIX_W // SB_HEADS
HG_HEADS = 8
HG_HD = MIX_W // HG_HEADS
RG_BLOCKS = 16
RG_BD = MIX_W // RG_BLOCKS
RG_CONV = 4
RG_C = 8.0
EPS = 1e-6
NEG = -1e30

LANES = 128
PAD = LANES - N_META
ML_CHUNK = 128
SB_BLOCK = 128
SB_SPAN = 256
SB_HPS = 4
HG_CHUNK = 64
HG_SUB = 8
HG_HPS = 4
RG_GROUP = 256
RG_T = 128
VMEM_LIMIT = 56 * 1024 * 1024

_SBQ, _SBK, _SBV = 0, 8, 16
_HGQ, _HGF, _HGI, _HGG = 24, 32, 40, 48
_RGX, _RGY = 56, 64
N_ML = 4 * MIX_W
N_REST = 9 * MIX_W


def _sigmoid(x):
    return 1.0 / (1.0 + jnp.exp(-x))


def _log_sigmoid(x):
    return jnp.minimum(x, 0.0) - jnp.log1p(jnp.exp(-jnp.abs(x)))


def _split3(x):
    hi = x.astype(BF16)
    r1 = x - hi.astype(F32)
    mid = r1.astype(BF16)
    lo = (r1 - mid.astype(F32)).astype(BF16)
    return hi, mid, lo


def _dot(a, b):
    return jnp.dot(a, b, preferred_element_type=F32)


def _dot_nt(a, b):
    return lax.dot_general(a, b, (((1,), (1,)), ((), ())), preferred_element_type=F32)


def _dot_tn(a, b):
    return lax.dot_general(a, b, (((0,), (0,)), ((), ())), preferred_element_type=F32)


def _tri_left(x, tri):
    hi, mid, lo = _split3(x)
    return _dot(tri, hi) + _dot(tri, mid) + _dot(tri, lo)


def _tri_right(x, tri):
    hi, mid, lo = _split3(x)
    return _dot(hi, tri) + _dot(mid, tri) + _dot(lo, tri)


def _iota(shape, dim):
    return lax.broadcasted_iota(jnp.int32, shape, dim)


def _cast_specs(casts, grid):
    steps = 1
    for g in grid:
        steps *= g

    def step_of(*idx):
        s = idx[0]
        for g, i in zip(grid[1:], idx[1:]):
            s = s * g + i
        return s

    in_specs, out_specs, out_shapes, operands = [], [], [], []
    for w, layer, rows in casts:
        _, r, c = w.shape
        nb = r // rows
        assert nb * rows == r and nb <= steps, (w.shape, rows, steps)
        in_specs.append(pl.BlockSpec(
            (None, rows, c), lambda *idx, layer=layer, nb=nb: (layer, jnp.minimum(step_of(*idx), nb - 1), 0)))
        out_specs.append(pl.BlockSpec(
            (rows, c), lambda *idx, nb=nb: (jnp.minimum(step_of(*idx), nb - 1), 0)))
        out_shapes.append(jax.ShapeDtypeStruct((r, c), BF16))
        operands.append(w)
    return in_specs, out_specs, out_shapes, operands


def _with_casts(body, n_in, n_out, n_cast):
    def kernel(*refs):
        ins = refs[:n_in]
        cast_in = refs[n_in:n_in + n_cast]
        outs = refs[n_in + n_cast:n_in + n_cast + n_out]
        cast_out = refs[n_in + n_cast + n_out:n_in + 2 * n_cast + n_out]
        scratch = refs[n_in + 2 * n_cast + n_out:]
        for src, dst in zip(cast_in, cast_out):
            dst[...] = src[...].astype(BF16)
        body(*ins, *outs, *scratch)
    return kernel


def _rmsnorm_kernel(x_ref, g_ref, o_ref):
    x = x_ref[...]
    y = x * lax.rsqrt(jnp.mean(x * x, axis=-1, keepdims=True) + EPS)
    o_ref[...] = (y * g_ref[...]).astype(o_ref.dtype)


def _rmsnorm(x, g, tm=256):
    m, d = x.shape
    return pl.pallas_call(
        _rmsnorm_kernel,
        out_shape=jax.ShapeDtypeStruct((m, d), BF16),
        grid=(m // tm,),
        in_specs=[pl.BlockSpec((tm, d), lambda i: (i, 0)),
                  pl.BlockSpec((1, d), lambda i: (0, 0))],
        out_specs=pl.BlockSpec((tm, d), lambda i: (i, 0)),
        compiler_params=pltpu.CompilerParams(dimension_semantics=("parallel",)),
        name="rmsnorm",
    )(x, g.reshape(1, d))


def _linear_bias_kernel(x_ref, w_ref, b_ref, o_ref):
    o_ref[...] = _dot(x_ref[...], w_ref[...]) + b_ref[...]


def _linear_bias(x, w, l, b, tm, tn, name):
    m, k = x.shape
    n = b.shape[1]
    return pl.pallas_call(
        _linear_bias_kernel,
        out_shape=jax.ShapeDtypeStruct((m, n), F32),
        grid=(m // tm, n // tn),
        in_specs=[pl.BlockSpec((tm, k), lambda i, j: (i, 0)),
                  pl.BlockSpec((None, k, tn), lambda i, j: (l, 0, j)),
                  pl.BlockSpec((1, tn), lambda i, j: (0, j))],
        out_specs=pl.BlockSpec((tm, tn), lambda i, j: (i, j)),
        compiler_params=pltpu.CompilerParams(
            dimension_semantics=("parallel", "parallel"), vmem_limit_bytes=VMEM_LIMIT),
        name=name,
    )(x, w, b)


def _linear_res_kernel(x_ref, w_ref, r_ref, o_ref, *, nk):
    part = _dot(x_ref[...], w_ref[...])
    if nk == 1:
        o_ref[...] = r_ref[...] + part
    else:
        k = pl.program_id(2)

        @pl.when(k == 0)
        def _():
            o_ref[...] = r_ref[...] + part

        @pl.when(k > 0)
        def _():
            o_ref[...] += part


def _linear_res(x, w, l, res, tm, tn, tk, name):
    m, k = x.shape
    n = w.shape[2]
    nk = k // tk
    return pl.pallas_call(
        functools.partial(_linear_res_kernel, nk=nk),
        out_shape=jax.ShapeDtypeStruct((m, n), F32),
        grid=(m // tm, n // tn, nk),
        in_specs=[pl.BlockSpec((tm, tk), lambda i, j, kk: (i, kk)),
                  pl.BlockSpec((None, tk, tn), lambda i, j, kk: (l, kk, j)),
                  pl.BlockSpec((tm, tn), lambda i, j, kk: (i, j))],
        out_specs=pl.BlockSpec((tm, tn), lambda i, j, kk: (i, j)),
        compiler_params=pltpu.CompilerParams(
            dimension_semantics=("parallel", "parallel", "arbitrary"), vmem_limit_bytes=VMEM_LIMIT),
        name=name,
    )(x, w, res)


def _merge_kernel(xn_ref, b0_ref, b1_ref, b2_ref, b3_ref, g0_ref, g1_ref, g2_ref, g3_ref,
                  c0_ref, c1_ref, c2_ref, c3_ref, wu_ref, o_ref):
    xn = xn_ref[...]
    acc = None
    branches = (b0_ref, b1_ref, b2_ref, b3_ref)
    gate_w = (g0_ref, g1_ref, g2_ref, g3_ref)
    gate_b = (c0_ref, c1_ref, c2_ref, c3_ref)
    for kb in range(N_BRANCH):
        g = _dot(xn, gate_w[kb][...]) + gate_b[kb][...]
        u = _dot(branches[kb][...], wu_ref[kb])
        t = _sigmoid(g) * u
        acc = t if acc is None else acc + t
    o_ref[...] = acc.astype(o_ref.dtype)


def _merge(xn, branches, w_gate, b_gate, w_up, l, tm, tn):
    m, d = xn.shape
    nj = d // tn
    br_spec = pl.BlockSpec((tm, MIX_W), lambda i, j: (i, 0))

    def gw(kb):
        return pl.BlockSpec((d, tn), lambda i, j: (0, kb * nj + j))

    def gb(kb):
        return pl.BlockSpec((1, tn), lambda i, j: (0, kb * nj + j))

    return pl.pallas_call(
        _merge_kernel,
        out_shape=jax.ShapeDtypeStruct((m, d), BF16),
        grid=(m // tm, nj),
        in_specs=[pl.BlockSpec((tm, d), lambda i, j: (i, 0)),
                  br_spec, br_spec, br_spec, br_spec,
                  gw(0), gw(1), gw(2), gw(3), gb(0), gb(1), gb(2), gb(3),
                  pl.BlockSpec((None, N_BRANCH, MIX_W, tn), lambda i, j: (l, 0, 0, j))],
        out_specs=pl.BlockSpec((tm, tn), lambda i, j: (i, j)),
        compiler_params=pltpu.CompilerParams(
            dimension_semantics=("parallel", "parallel"), vmem_limit_bytes=VMEM_LIMIT),
        name="merge",
    )(xn, *branches, w_gate, w_gate, w_gate, w_gate, b_gate, b_gate, b_gate, b_gate, w_up)


def _ffn_gu_kernel(x_ref, wg_ref, wu_ref, o_ref):
    x = x_ref[...]
    g = _dot(x, wg_ref[...])
    u = _dot(x, wu_ref[...])
    o_ref[...] = (g * _sigmoid(g) * u).astype(o_ref.dtype)


def _ffn_gate_up(x, wg, wu, l, tm, tn):
    m, k = x.shape
    n = wg.shape[2]
    return pl.pallas_call(
        _ffn_gu_kernel,
        out_shape=jax.ShapeDtypeStruct((m, n), BF16),
        grid=(m // tm, n // tn),
        in_specs=[pl.BlockSpec((tm, k), lambda i, j: (i, 0)),
                  pl.BlockSpec((None, k, tn), lambda i, j: (l, 0, j)),
                  pl.BlockSpec((None, k, tn), lambda i, j: (l, 0, j))],
        out_specs=pl.BlockSpec((tm, tn), lambda i, j: (i, j)),
        compiler_params=pltpu.CompilerParams(
            dimension_semantics=("parallel", "parallel"), vmem_limit_bytes=VMEM_LIMIT),
        name="ffn_gate_up",
    )(x, wg, wu)


def _mlstm_kernel(q_ref, k_ref, v_ref, o_ref, gc_ref, gr_ref, gn_ref, out_ref, c_ref, n_ref, m_ref):
    c = pl.program_id(1)
    T = ML_CHUNK

    @pl.when(c == 0)
    def _():
        c_ref[...] = jnp.zeros_like(c_ref)
        n_ref[...] = jnp.zeros_like(n_ref)
        m_ref[...] = jnp.zeros_like(m_ref)

    valid_col = (c * T + _iota((T, 1), 0)) >= PAD
    valid_row = (c * T + _iota((1, T), 1)) >= PAD
    gc = gc_ref[...]
    gr = gr_ref[...]
    tt = _iota((T, T), 0)
    ss = _iota((T, T), 1)
    causal = ss <= tt
    tril = jnp.where(causal, 1.0, 0.0).astype(BF16)
    triu = jnp.where(tt <= ss, 1.0, 0.0).astype(BF16)
    bc_all = _tri_left(jnp.where(valid_col, _log_sigmoid(gc), 0.0), tril)
    br_all = _tri_right(jnp.where(valid_row, _log_sigmoid(gr), 0.0), triu)

    heads = range(ML_HEADS)
    sl = [slice(h * ML_HD, (h + 1) * ML_HD) for h in heads]
    q = [q_ref[:, sl[h]] * (ML_HD ** -0.5) for h in heads]
    k = [k_ref[:, sl[h]] for h in heads]
    qb = [q[h].astype(BF16) for h in heads]
    vb = [v_ref[:, sl[h]].astype(BF16) for h in heads]
    qk = [_dot_nt(qb[h], k[h].astype(BF16)) for h in heads]
    c_prev = [c_ref[h] for h in heads]
    qc = [_dot(qb[h], c_prev[h].astype(BF16)) for h in heads]

    li_col = [jnp.where(valid_col, gc[:, h:h + 1], NEG) for h in heads]
    li_row = [jnp.where(valid_row, gr[h:h + 1, :], NEG) for h in heads]
    b_col = [bc_all[:, ML_HEADS + h:ML_HEADS + h + 1] for h in heads]
    b_row = [br_all[ML_HEADS + h:ML_HEADS + h + 1, :] for h in heads]
    m_prev = [m_ref[:, h:h + 1] for h in heads]
    dmat = [jnp.where(causal, b_col[h] - b_row[h] + li_row[h], NEG) for h in heads]
    g_col = [b_col[h] + m_prev[h] for h in heads]
    m_out = [jnp.maximum(g_col[h], jnp.max(dmat[h], axis=1, keepdims=True)) for h in heads]

    b_end = [b_row[h][:, T - 1:T] for h in heads]
    wlog = [b_end[h] - b_col[h] + li_col[h] for h in heads]
    m_new = [jnp.maximum(b_end[h] + m_prev[h], jnp.max(wlog[h], axis=0, keepdims=True)) for h in heads]
    decay = [jnp.exp(b_end[h] + m_prev[h] - m_new[h]) for h in heads]
    kw = [k[h] * jnp.exp(wlog[h] - m_new[h]) for h in heads]
    kv = [_dot(kw[h].T.astype(BF16), vb[h]) for h in heads]

    s = [qk[h] * jnp.exp(dmat[h] - m_out[h]) for h in heads]
    inter = [jnp.exp(g_col[h] - m_out[h]) for h in heads]
    sv = [_dot(s[h].astype(BF16), vb[h]) for h in heads]

    for h in heads:
        num = sv[h] + inter[h] * qc[h]
        den = (jnp.sum(s[h], axis=1, keepdims=True)
               + inter[h] * jnp.sum(q[h] * n_ref[h], axis=1, keepdims=True))
        hh = num / jnp.maximum(jnp.abs(den), jnp.exp(-m_out[h]))
        y = hh * lax.rsqrt(jnp.mean(hh * hh, axis=-1, keepdims=True) + EPS) * gn_ref[:, sl[h]]
        out_ref[:, sl[h]] = (y * _sigmoid(o_ref[:, sl[h]])).astype(out_ref.dtype)
        c_ref[h] = decay[h] * c_prev[h] + kv[h]
        n_ref[h] = decay[h] * n_ref[h] + jnp.sum(kw[h], axis=0, keepdims=True)
        m_ref[:, h:h + 1] = m_new[h]


def _mlstm(proj, gates, gates_t, gnorm, casts=()):
    b, lp, _ = proj.shape
    T = ML_CHUNK
    grid = (b, lp // T)

    def col(idx):
        return pl.BlockSpec((None, T, MIX_W), lambda bi, ci: (bi, ci, idx))

    in_specs = [col(0), col(1), col(2), col(3),
                pl.BlockSpec((None, T, LANES), lambda bi, ci: (bi, ci, 0)),
                pl.BlockSpec((None, 8, T), lambda bi, ci: (bi, 0, ci)),
                pl.BlockSpec((1, MIX_W), lambda bi, ci: (0, 0))]
    c_in, c_out, c_shapes, c_ops = _cast_specs(casts, grid)
    return pl.pallas_call(
        _with_casts(_mlstm_kernel, len(in_specs), 1, len(casts)),
        out_shape=[jax.ShapeDtypeStruct((b, lp, MIX_W), BF16)] + c_shapes,
        grid=grid,
        in_specs=in_specs + c_in,
        out_specs=[pl.BlockSpec((None, T, MIX_W), lambda bi, ci: (bi, ci, 0))] + c_out,
        scratch_shapes=[pltpu.VMEM((ML_HEADS, ML_HD, ML_HD), F32),
                        pltpu.VMEM((ML_HEADS, 1, ML_HD), F32),
                        pltpu.VMEM((1, LANES), F32)],
        compiler_params=pltpu.CompilerParams(
            dimension_semantics=("arbitrary", "arbitrary"), vmem_limit_bytes=VMEM_LIMIT),
        name="mlstm",
    )(proj, proj, proj, proj, gates, gates_t, gnorm.reshape(1, MIX_W), *c_ops)


def _sb_kernel(q_ref, k_ref, v_ref, gq_ref, gk_ref, out_ref, kn_ref, vb_ref, acc_ref, run_ref, qn_ref, tri_ref,
               z_ref, a_ref):
    qi = pl.program_id(2)
    T = SB_BLOCK
    W = SB_SPAN
    row0 = pl.multiple_of(qi * T, T)

    @pl.when(jnp.bitwise_and(qi, 1) == 0)
    def _():
        kn_ref[pl.ds(row0 + T, T), :] = jnp.zeros((T, kn_ref.shape[1]), BF16)
        vb_ref[pl.ds(row0 + T, T), :] = jnp.zeros((T, vb_ref.shape[1]), BF16)

    @pl.when(qi == 0)
    def _():
        jj = jnp.bitwise_and(_iota((2 * W, W), 0), W - 1)
        tri_ref[...] = jnp.where(jj > _iota((2 * W, W), 1), 1.0, 0.0).astype(BF16)

    heads = range(SB_HPS)
    sl = [slice(h * SB_HD, (h + 1) * SB_HD) for h in heads]
    vb_ref[pl.ds(row0, T), :] = v_ref[pl.ds(row0, T), :].astype(BF16)
    for h in heads:
        kh = k_ref[pl.ds(row0, T), sl[h]]
        kn = kh * lax.rsqrt(jnp.mean(kh * kh, axis=-1, keepdims=True) + EPS) * gk_ref[...]
        kn_ref[pl.ds(row0, T), sl[h]] = kn.astype(BF16)
        qh = q_ref[:, sl[h]]
        qs = qh * lax.rsqrt(jnp.mean(qh * qh, axis=-1, keepdims=True) + EPS) * (gq_ref[...] * SB_HD ** -0.5)
        qn_ref[:, sl[h]] = qs.astype(BF16)

    acc_ref[...] = jnp.zeros_like(acc_ref)
    run_ref[...] = jnp.zeros_like(run_ref)
    t_pos = row0 + _iota((T, W), 0)
    lane = _iota((T, W), 1)
    top = lax.shift_right_logical(qi, 1)

    def scores(col0, slot):
        for h in heads:
            z_ref[slot, h] = _dot_nt(qn_ref[:, sl[h]], kn_ref[pl.ds(col0, W), sl[h]])

    def weighted_values(col0):
        for h in heads:
            acc_ref[h] += _dot(a_ref[h], vb_ref[pl.ds(col0, W), sl[h]])

    def span_step(it, masked, has_prev):
        col0 = pl.multiple_of((top - it) * W, W)
        slot = it & 1
        z = [z_ref[slot, h] for h in heads]
        if has_prev:
            weighted_values(pl.multiple_of(col0 + W, W))
        l1p = [jnp.log(1.0 + jnp.exp(-jnp.abs(z[h]))) for h in heads]
        ls = [jnp.minimum(z[h], 0.0) - l1p[h] for h in heads]
        lk = [ls[h] - z[h] for h in heads]
        if masked:
            s_pos = col0 + lane
            vis = jnp.logical_and(s_pos < t_pos, s_pos >= PAD)
            lk = [jnp.where(vis, lk[h], 0.0) for h in heads]
        lk_hi = [lk[h].astype(BF16) for h in heads]
        lk_lo = [(lk[h] - lk_hi[h].astype(F32)).astype(BF16) for h in heads]
        cs = [_dot(jnp.concatenate([lk_hi[h], lk_lo[h]], axis=1), tri_ref[...]) for h in heads]
        scores(pl.multiple_of(jnp.maximum(top - it - 1, 0) * W, W), 1 - slot)
        a = [jnp.exp(ls[h] + cs[h] + run_ref[h]) for h in heads]
        if masked:
            a = [jnp.where(vis, a[h], 0.0) for h in heads]
        for h in heads:
            a_ref[h] = a[h].astype(BF16)
            run_ref[h] += jnp.sum(lk[h], axis=1, keepdims=True)

    scores(pl.multiple_of(top * W, W), 0)
    span_step(0, True, False)

    def body(it, carry):
        span_step(it, False, True)
        return carry

    lax.fori_loop(1, top, body, 0)

    @pl.when(top > 0)
    def _():
        span_step(top, True, True)

    weighted_values(0)
    for h in heads:
        out_ref[:, sl[h]] = acc_ref[h].astype(out_ref.dtype)


def _stick_breaking(proj, gq, gk, casts=()):
    b, lp, _ = proj.shape
    T = SB_BLOCK
    wid = SB_HPS * SB_HD
    ng = SB_HEADS // SB_HPS
    per = wid // LANES
    grid = (b, ng, lp // T)
    in_specs = [pl.BlockSpec((None, T, wid), lambda bi, gi, qi: (bi, qi, _SBQ // per + gi)),
                pl.BlockSpec((None, lp, wid), lambda bi, gi, qi: (bi, 0, _SBK // per + gi),
                             pipeline_mode=pl.Buffered(1)),
                pl.BlockSpec((None, lp, wid), lambda bi, gi, qi: (bi, 0, _SBV // per + gi),
                             pipeline_mode=pl.Buffered(1)),
                pl.BlockSpec((1, SB_HD), lambda bi, gi, qi: (0, 0)),
                pl.BlockSpec((1, SB_HD), lambda bi, gi, qi: (0, 0))]
    c_in, c_out, c_shapes, c_ops = _cast_specs(casts, grid)
    return pl.pallas_call(
        _with_casts(_sb_kernel, len(in_specs), 1, len(casts)),
        out_shape=[jax.ShapeDtypeStruct((b, lp, MIX_W), BF16)] + c_shapes,
        grid=grid,
        in_specs=in_specs + c_in,
        out_specs=[pl.BlockSpec((None, T, wid), lambda bi, gi, qi: (bi, qi, gi))] + c_out,
        scratch_shapes=[pltpu.VMEM((lp + SB_SPAN - T, wid), BF16),
                        pltpu.VMEM((lp + SB_SPAN - T, wid), BF16),
                        pltpu.VMEM((SB_HPS, T, SB_HD), F32),
                        pltpu.VMEM((SB_HPS, T, 1), F32),
                        pltpu.VMEM((T, wid), BF16),
                        pltpu.VMEM((2 * SB_SPAN, SB_SPAN), BF16),
                        pltpu.VMEM((2, SB_HPS, T, SB_SPAN), F32),
                        pltpu.VMEM((SB_HPS, T, SB_SPAN), BF16)],
        compiler_params=pltpu.CompilerParams(
            dimension_semantics=("arbitrary", "arbitrary", "arbitrary"), vmem_limit_bytes=VMEM_LIMIT),
        name="stick_breaking",
    )(proj, proj, proj, gq.reshape(1, SB_HD), gk.reshape(1, SB_HD), *c_ops)


def _hgrn2_kernel(q_ref, f_ref, i_ref, g_ref, lb_ref, gn_ref, out_ref, st_ref):
    c = pl.program_id(2)
    T = HG_CHUNK
    S8 = HG_SUB

    @pl.when(c == 0)
    def _():
        st_ref[...] = jnp.zeros_like(st_ref)

    valid = (c * T + _iota((T, 1), 0)) >= PAD
    tril = jnp.where(_iota((T, T), 1) <= _iota((T, T), 0), 1.0, 0.0).astype(BF16)
    lane8 = _iota((S8, T), 1)
    row8 = _iota((S8, T), 0)

    heads = range(HG_HPS)
    hs = [slice(h * HG_HD, (h + 1) * HG_HD) for h in heads]
    lb = [lb_ref[:, hs[h]] for h in heads]
    sig = [_sigmoid(f_ref[:, hs[h]]) for h in heads]
    logf = [jnp.where(valid, jnp.log(lb[h] + (1.0 - lb[h]) * sig[h]), 0.0) for h in heads]
    kk = [jnp.where(valid, (1.0 - lb[h]) * (1.0 - sig[h]), 0.0) for h in heads]
    q = [q_ref[:, hs[h]] * _sigmoid(q_ref[:, hs[h]]) for h in heads]
    iv = [i_ref[:, hs[h]].astype(BF16) for h in heads]
    b = [_tri_left(logf[h], tril) for h in heads]

    st = [st_ref[h] for h in heads]
    b_end = [b[h][T - 1:T] for h in heads]
    o_inter = [_dot_nt((q[h] * jnp.exp(b[h])).astype(BF16), st[h].astype(BF16)) for h in heads]
    st_add = [_dot_tn(iv[h], (kk[h] * jnp.exp(b_end[h] - b[h])).astype(BF16)) for h in heads]

    def diag_block(h, blk):
        r0 = blk * S8
        qb = q[h][r0:r0 + S8]
        bb = b[h][r0:r0 + S8]
        diag = jnp.zeros((S8, T), F32)
        for j in range(S8):
            s_idx = r0 + j
            p = qb * kk[h][s_idx:s_idx + 1] * jnp.exp(jnp.minimum(bb - b[h][s_idx:s_idx + 1], 0.0))
            diag = jnp.where(lane8 == s_idx, jnp.sum(p, axis=1, keepdims=True), diag)
        return jnp.where(lane8 - r0 <= row8, diag, 0.0)

    def off_block(h, blk):
        r0 = blk * S8
        bref = b[h][r0 - 1:r0]
        qs = (q[h][r0:r0 + S8] * jnp.exp(b[h][r0:r0 + S8] - bref)).astype(BF16)
        ks = (kk[h] * jnp.exp(jnp.minimum(bref - b[h], 0.0))).astype(BF16)
        return _dot_nt(qs, ks)

    nblk = T // S8
    off = [[off_block(h, blk) for h in heads] for blk in range(1, nblk)]
    att = []
    for h in heads:
        rows = [diag_block(h, 0)]
        for blk in range(1, nblk):
            rows.append(jnp.where(lane8 < blk * S8, off[blk - 1][h], diag_block(h, blk)))
        att.append(jnp.concatenate(rows, axis=0).astype(BF16))
    o_intra = [_dot(att[h], iv[h]) for h in heads]

    for h in heads:
        o = o_intra[h] + o_inter[h]
        st_ref[h] = st[h] * jnp.exp(b_end[h]) + st_add[h]
        y = o * lax.rsqrt(jnp.mean(o * o, axis=-1, keepdims=True) + EPS) * gn_ref[:, hs[h]]
        out_ref[:, hs[h]] = (y * _sigmoid(g_ref[:, hs[h]])).astype(out_ref.dtype)


def _hgrn2(proj, lb, gnorm, casts=()):
    b, lp, _ = proj.shape
    T = HG_CHUNK
    wid = HG_HPS * HG_HD
    ng = HG_HEADS // HG_HPS
    per = wid // LANES
    grid = (b, ng, lp // T)

    def col(off):
        return pl.BlockSpec((None, T, wid), lambda bi, gi, ci: (bi, ci, off // per + gi))

    vec = pl.BlockSpec((1, wid), lambda bi, gi, ci: (0, gi))
    in_specs = [col(_HGQ), col(_HGF), col(_HGI), col(_HGG), vec, vec]
    c_in, c_out, c_shapes, c_ops = _cast_specs(casts, grid)
    return pl.pallas_call(
        _with_casts(_hgrn2_kernel, len(in_specs), 1, len(casts)),
        out_shape=[jax.ShapeDtypeStruct((b, lp, MIX_W), BF16)] + c_shapes,
        grid=grid,
        in_specs=in_specs + c_in,
        out_specs=[pl.BlockSpec((None, T, wid), lambda bi, gi, ci: (bi, ci, gi))] + c_out,
        scratch_shapes=[pltpu.VMEM((HG_HPS, HG_HD, HG_HD), F32)],
        compiler_params=pltpu.CompilerParams(
            dimension_semantics=("arbitrary", "arbitrary", "arbitrary"), vmem_limit_bytes=VMEM_LIMIT),
        name="hgrn2",
    )(proj, proj, proj, proj, lb.reshape(1, MIX_W), gnorm.reshape(1, MIX_W), *c_ops)


def _rglru_kernel(x_ref, y_ref, cw_ref, cb_ref, w_ref, ba_ref, bx_ref, lam_ref, out_ref, xbuf_ref, h_ref):
    c = pl.program_id(2)
    T = RG_T
    G = RG_GROUP

    @pl.when(c == 0)
    def _():
        xbuf_ref[0:8, :] = jnp.zeros((8, G), F32)
        h_ref[...] = jnp.zeros_like(h_ref)

    valid = (c * T + _iota((T, 1), 0)) >= PAD
    x = jnp.where(valid, x_ref[...], 0.0)
    xbuf_ref[8:8 + T, :] = x
    xc = cb_ref[...] + cw_ref[RG_CONV - 1:RG_CONV, :] * x
    for j in range(1, RG_CONV):
        xc = xc + cw_ref[RG_CONV - 1 - j:RG_CONV - j, :] * xbuf_ref[8 - j:8 - j + T, :]
    xbuf_ref[0:8, :] = x[T - 8:T]

    pre = _dot(xc.astype(BF16), w_ref[...])
    r = _sigmoid(pre[:, :G] + ba_ref[...])
    ig = _sigmoid(pre[:, G:] + bx_ref[...])
    lam = lam_ref[...]
    softplus_neg = jnp.maximum(-lam, 0.0) + jnp.log1p(jnp.exp(-jnp.abs(lam)))
    log_a = -RG_C * r * softplus_neg
    a = jnp.exp(log_a)
    th = jnp.tanh(log_a)
    u = jnp.where(valid, jnp.sqrt(-2.0 * th / (1.0 - th)) * (ig * xc), 0.0)

    row = _iota((T, G), 0)
    shift = 1
    while shift < T:
        keep = row >= shift
        u_s = pltpu.roll(u, shift, axis=0)
        a_s = pltpu.roll(a, shift, axis=0)
        u = jnp.where(keep, a * u_s + u, u)
        a = jnp.where(keep, a * a_s, a)
        shift *= 2
    hseq = a * h_ref[...] + u
    h_ref[...] = hseq[T - 1:T]

    y = y_ref[...]
    gelu = 0.5 * y * (1.0 + jnp.tanh(0.7978845608028654 * (y + 0.044715 * (y * y * y))))
    out_ref[...] = (hseq * gelu).astype(out_ref.dtype)


def _rglru(proj, conv_w, conv_b, w_bd, ba, bx, lam, casts=()):
    b, lp, _ = proj.shape
    T = RG_T
    G = RG_GROUP
    ng = MIX_W // G
    per = G // LANES
    grid = (b, ng, lp // T)

    def col(off):
        return pl.BlockSpec((None, T, G), lambda bi, gi, ci: (bi, ci, off // per + gi))

    vec = pl.BlockSpec((1, G), lambda bi, gi, ci: (0, gi))
    in_specs = [col(_RGX), col(_RGY),
                pl.BlockSpec((RG_CONV, G), lambda bi, gi, ci: (0, gi)),
                vec,
                pl.BlockSpec((None, G, 2 * G), lambda bi, gi, ci: (gi, 0, 0)),
                vec, vec, vec]
    c_in, c_out, c_shapes, c_ops = _cast_specs(casts, grid)
    return pl.pallas_call(
        _with_casts(_rglru_kernel, len(in_specs), 1, len(casts)),
        out_shape=[jax.ShapeDtypeStruct((b, lp, MIX_W), BF16)] + c_shapes,
        grid=grid,
        in_specs=in_specs + c_in,
        out_specs=[pl.BlockSpec((None, T, G), lambda bi, gi, ci: (bi, ci, gi))] + c_out,
        scratch_shapes=[pltpu.VMEM((T + 8, G), F32), pltpu.VMEM((1, G), F32)],
        compiler_params=pltpu.CompilerParams(
            dimension_semantics=("arbitrary", "arbitrary", "arbitrary"), vmem_limit_bytes=VMEM_LIMIT),
        name="rglru",
    )(proj, proj, conv_w, conv_b.reshape(1, MIX_W), w_bd, ba.reshape(1, MIX_W), bx.reshape(1, MIX_W),
      lam.reshape(1, MIX_W), *c_ops)


def _block_diag_gates(wa, wx):
    per = RG_GROUP // RG_BD
    ng = RG_BLOCKS // per
    eye = jnp.eye(per, dtype=F32)

    def bd(w):
        w4 = w.reshape(ng, per, RG_BD, RG_BD)
        return jnp.einsum("gaij,ab->gaibj", w4, eye).reshape(ng, RG_GROUP, RG_GROUP)

    return jnp.concatenate([bd(wa), bd(wx)], axis=-1).astype(BF16)


def _assemble_kernel(x_ref, front_ref, o_ref):
    r = pl.program_id(1)

    @pl.when(r == 0)
    def _():
        o_ref[...] = front_ref[...]

    @pl.when(r > 0)
    def _():
        o_ref[...] = x_ref[...]


def _assemble(x, front):
    b, seq, d = x.shape
    tr = PAD + N_META
    return pl.pallas_call(
        _assemble_kernel,
        out_shape=jax.ShapeDtypeStruct((b, tr + seq, d), x.dtype),
        grid=(b, 1 + seq // tr),
        in_specs=[pl.BlockSpec((None, tr, d), lambda bi, r: (bi, jnp.maximum(r - 1, 0), 0)),
                  pl.BlockSpec((tr, d), lambda bi, r: (0, 0))],
        out_specs=pl.BlockSpec((None, tr, d), lambda bi, r: (bi, r, 0)),
        compiler_params=pltpu.CompilerParams(dimension_semantics=("parallel", "parallel")),
        name="assemble",
    )(x, front)


def _layer(h, l, w_in_bf, p, lower_bounds, tiles):
    b, lp = p["b"], p["lp"]
    m = b * lp
    tm = tiles["tm"]

    depth = p["w_in"].shape[0]
    b_in = p["b_in"][l]
    if_lo = N_ML
    if_hi = N_ML + 2 * ML_HEADS
    gate_lo = if_hi + N_REST
    w_rest = w_in_bf[:, if_hi:gate_lo][None]
    w_gate = w_in_bf[:, gate_lo:]
    w_if = jnp.pad(w_in_bf[:, if_lo:if_hi], ((0, 0), (0, LANES - 2 * ML_HEADS)))[None]
    b_if = jnp.pad(b_in[if_lo:if_hi], (0, LANES - 2 * ML_HEADS)).reshape(1, LANES)

    xn = _rmsnorm(h, p["norm_mix"][l])
    proj_ml = _linear_bias(xn, w_in_bf[None], 0, b_in[:N_ML].reshape(1, N_ML), tm, tiles["tn_mix"],
                           "in_proj_ml")
    proj_rest = _linear_bias(xn, w_rest, 0, b_in[if_hi:gate_lo].reshape(1, N_REST), tm, tiles["tn_mix"],
                             "in_proj_rest")
    proj_ml = proj_ml.reshape(b, lp, N_ML)
    proj_rest = proj_rest.reshape(b, lp, N_REST)
    gates = _linear_bias(xn, w_if, 0, b_if, tm, LANES, "in_proj_gates").reshape(b, lp, LANES)
    gates_t = jnp.swapaxes(gates[:, :, :8], 1, 2)

    w_up_stack = p["w_up"].reshape(depth, N_BRANCH * MIX_W, D_MODEL)
    sb_casts = [(p["w_ffn_down"], l, 128)]
    if l + 1 < depth:
        sb_casts.append((p["w_in"], l + 1, 32))
    ml_out, w_out_bf, w_up_bf = _mlstm(proj_ml, gates, gates_t, p["ml_norm"][l],
                                       casts=[(p["w_out"], l, 64), (w_up_stack, l, 64)])
    sb_out, w_down_bf, *w_in_next = _stick_breaking(proj_rest, p["sb_q_norm"][l], p["sb_k_norm"][l],
                                                    casts=sb_casts)
    hg_out, w_fgate_bf = _hgrn2(proj_rest, lower_bounds[l], p["hg_norm"][l], casts=[(p["w_ffn_gate"], l, 16)])
    rg_out, w_fup_bf = _rglru(proj_rest, p["rg_conv_w"][l], p["rg_conv_b"][l],
                              _block_diag_gates(p["rg_wa"][l], p["rg_wx"][l]),
                              p["rg_ba"][l], p["rg_bx"][l], p["rg_lambda"][l], casts=[(p["w_ffn_up"], l, 16)])
    branches = tuple(br.reshape(m, MIX_W) for br in (ml_out, sb_out, hg_out, rg_out))
    merged = _merge(xn, branches, w_gate, b_in[gate_lo:].reshape(1, N_BRANCH * D_MODEL),
                    w_up_bf.reshape(1, N_BRANCH, MIX_W, D_MODEL), 0, tiles["tm_merge"], tiles["tn_merge"])
    h = _linear_res(merged, w_out_bf[None], 0, h, tm, tiles["tn_out"], D_MODEL, "out_proj")

    d_ff = p["w_ffn_gate"].shape[-1]
    hn = _rmsnorm(h, p["norm_ffn"][l])
    act = _ffn_gate_up(hn, w_fgate_bf[None], w_fup_bf[None], 0, tiles["tm_ff"], tiles["tn_ff"])
    h = _linear_res(act, w_down_bf[None], 0, h, tiles["tm_down"], tiles["tn_down"], d_ff // tiles["nk_down"],
                    "ffn_down")
    return h, (w_in_next[0] if w_in_next else None)


def _forward(x, meta, params, tiles):
    b, seq, d = x.shape
    lp = PAD + N_META + seq
    depth = params["w_in"].shape[0]
    hg_lb = params["hg_lb"]
    p_lb = jax.nn.softmax(hg_lb.astype(F32), axis=0)
    lower_bounds = jnp.clip(jnp.cumsum(p_lb, axis=0) - p_lb[0:1], 0.0, 0.999)

    front = jnp.concatenate([jnp.zeros((PAD, d), x.dtype), meta.astype(x.dtype)], axis=0)
    h = _assemble(x, front).reshape(b * lp, d)
    p = dict(params, b=b, lp=lp)
    w_in_bf = params["w_in"][0].astype(BF16)
    for l in range(depth):
        h, w_in_bf = _layer(h, l, w_in_bf, p, lower_bounds, tiles)
    return h.reshape(b, lp, d)[:, PAD + N_META:]


_TILES = dict(tm=1056, tn_mix=1024, tm_merge=768, tn_merge=256, tn_out=512,
              tm_ff=2112, tn_ff=256, tm_down=768, tn_down=256, nk_down=1)


def kernel(x, meta, norm_mix, norm_ffn, w_in, b_in, ml_norm, sb_q_norm, sb_k_norm, hg_lb, hg_norm,
           rg_conv_w, rg_conv_b, rg_wa, rg_ba, rg_wx, rg_bx, rg_lambda, w_up, w_out,
           w_ffn_gate, w_ffn_up, w_ffn_down):
    params = dict(norm_mix=norm_mix, norm_ffn=norm_ffn, w_in=w_in, b_in=b_in, ml_norm=ml_norm,
                  sb_q_norm=sb_q_norm, sb_k_norm=sb_k_norm, hg_lb=hg_lb, hg_norm=hg_norm,
                  rg_conv_w=rg_conv_w, rg_conv_b=rg_conv_b, rg_wa=rg_wa, rg_ba=rg_ba, rg_wx=rg_wx,
                  rg_bx=rg_bx, rg_lambda=rg_lambda, w_up=w_up, w_out=w_out,
                  w_ffn_gate=w_ffn_gate, w_ffn_up=w_ffn_up, w_ffn_down=w_ffn_down)
    return _forward(x, meta, params, _TILES)
```

```python
import functools

import jax
import jax.numpy as jnp
from jax import lax
from jax.experimental import pallas as pl
from jax.experimental.pallas import tpu as pltpu

F32 = jnp.float32
BF16 = jnp.bfloat16

D_MODEL = 4096
N_META = 16
N_BRANCH = 4
MIX_W = D_MODEL // 4
ML_HEADS = 4
ML_HD = MIX_W // ML_HEADS
SB_HEADS = 8
SB_HD = MIX_W // SB_HEADS
HG_HEADS = 8
HG_HD = MIX_W // HG_HEADS
RG_BLOCKS = 16
RG_BD = MIX_W // RG_BLOCKS
RG_CONV = 4
RG_C = 8.0
EPS = 1e-6
NEG = -1e30

LANES = 128
PAD = LANES - N_META
ML_CHUNK = 128
SB_BLOCK = 128
SB_SPAN = 256
SB_HPS = 4
HG_CHUNK = 64
HG_SUB = 8
HG_HPS = 4
RG_GROUP = 256
RG_T = 128
VMEM_LIMIT = 56 * 1024 * 1024

_SBQ, _SBK, _SBV = 0, 8, 16
_HGQ, _HGF, _HGI, _HGG = 24, 32, 40, 48
_RGX, _RGY = 56, 64
N_ML = 4 * MIX_W
N_REST = 9 * MIX_W


def _sigmoid(x):
    return 1.0 / (1.0 + jnp.exp(-x))


def _log_sigmoid(x):
    return jnp.minimum(x, 0.0) - jnp.log1p(jnp.exp(-jnp.abs(x)))


def _split3(x):
    hi = x.astype(BF16)
    r1 = x - hi.astype(F32)
    mid = r1.astype(BF16)
    lo = (r1 - mid.astype(F32)).astype(BF16)
    return hi, mid, lo


def _dot(a, b):
    return jnp.dot(a, b, preferred_element_type=F32)


def _dot_nt(a, b):
    return lax.dot_general(a, b, (((1,), (1,)), ((), ())), preferred_element_type=F32)


def _dot_tn(a, b):
    return lax.dot_general(a, b, (((0,), (0,)), ((), ())), preferred_element_type=F32)


def _tri_left(x, tri):
    hi, mid, lo = _split3(x)
    return _dot(tri, hi) + _dot(tri, mid) + _dot(tri, lo)


def _tri_right(x, tri):
    hi, mid, lo = _split3(x)
    return _dot(hi, tri) + _dot(mid, tri) + _dot(lo, tri)


def _iota(shape, dim):
    return lax.broadcasted_iota(jnp.int32, shape, dim)


def _cast_specs(casts, grid):
    steps = 1
    for g in grid:
        steps *= g

    def step_of(*idx):
        s = idx[0]
        for g, i in zip(grid[1:], idx[1:]):
            s = s * g + i
        return s

    in_specs, out_specs, out_shapes, operands = [], [], [], []
    for w, layer, rows in casts:
        _, r, c = w.shape
        nb = pl.cdiv(r, rows)
        assert nb <= steps, (w.shape, rows, steps)
        in_specs.append(pl.BlockSpec(
            (None, rows, c), lambda *idx, layer=layer, nb=nb: (layer, jnp.minimum(step_of(*idx), nb - 1), 0)))
        out_specs.append(pl.BlockSpec(
            (rows, c), lambda *idx, nb=nb: (jnp.minimum(step_of(*idx), nb - 1), 0)))
        out_shapes.append(jax.ShapeDtypeStruct((r, c), BF16))
        operands.append(w)
    return in_specs, out_specs, out_shapes, operands


def _with_casts(body, n_in, n_out, n_cast):
    def kernel(*refs):
        ins = refs[:n_in]
        cast_in = refs[n_in:n_in + n_cast]
        outs = refs[n_in + n_cast:n_in + n_cast + n_out]
        cast_out = refs[n_in + n_cast + n_out:n_in + 2 * n_cast + n_out]
        scratch = refs[n_in + 2 * n_cast + n_out:]
        for src, dst in zip(cast_in, cast_out):
            dst[...] = src[...].astype(BF16)
        body(*ins, *outs, *scratch)
    return kernel


def _rmsnorm_kernel(x_ref, g_ref, o_ref):
    x = x_ref[...]
    y = x * lax.rsqrt(jnp.mean(x * x, axis=-1, keepdims=True) + EPS)
    o_ref[...] = (y * g_ref[...]).astype(o_ref.dtype)


def _rmsnorm(x, g, tm=256):
    m, d = x.shape
    return pl.pallas_call(
        _rmsnorm_kernel,
        out_shape=jax.ShapeDtypeStruct((m, d), BF16),
        grid=(m // tm,),
        in_specs=[pl.BlockSpec((tm, d), lambda i: (i, 0)),
                  pl.BlockSpec((1, d), lambda i: (0, 0))],
        out_specs=pl.BlockSpec((tm, d), lambda i: (i, 0)),
        compiler_params=pltpu.CompilerParams(dimension_semantics=("parallel",)),
        name="rmsnorm",
    )(x, g.reshape(1, d))


def _linear_bias_kernel(x_ref, wt_ref, b_ref, o_ref):
    o_ref[...] = _dot_nt(x_ref[...], wt_ref[...]) + b_ref[...]


def _linear_bias(x, wt, b, tm, tn, name):
    m, k = x.shape
    n = b.shape[1]
    return pl.pallas_call(
        _linear_bias_kernel,
        out_shape=jax.ShapeDtypeStruct((m, n), F32),
        grid=(m // tm, n // tn),
        in_specs=[pl.BlockSpec((tm, k), lambda i, j: (i, 0)),
                  pl.BlockSpec((tn, k), lambda i, j: (j, 0)),
                  pl.BlockSpec((1, tn), lambda i, j: (0, j))],
        out_specs=pl.BlockSpec((tm, tn), lambda i, j: (i, j)),
        compiler_params=pltpu.CompilerParams(
            dimension_semantics=("parallel", "parallel"), vmem_limit_bytes=VMEM_LIMIT),
        name=name,
    )(x, wt, b)


def _linear_res_kernel(x_ref, w_ref, r_ref, o_ref, *, nk):
    part = _dot(x_ref[...], w_ref[...])
    if nk == 1:
        o_ref[...] = r_ref[...] + part
    else:
        k = pl.program_id(2)

        @pl.when(k == 0)
        def _():
            o_ref[...] = r_ref[...] + part

        @pl.when(k > 0)
        def _():
            o_ref[...] += part


def _linear_res(x, w, l, res, tm, tn, tk, name):
    m, k = x.shape
    n = w.shape[2]
    nk = k // tk
    return pl.pallas_call(
        functools.partial(_linear_res_kernel, nk=nk),
        out_shape=jax.ShapeDtypeStruct((m, n), F32),
        grid=(m // tm, n // tn, nk),
        in_specs=[pl.BlockSpec((tm, tk), lambda i, j, kk: (i, kk)),
                  pl.BlockSpec((None, tk, tn), lambda i, j, kk: (l, kk, j)),
                  pl.BlockSpec((tm, tn), lambda i, j, kk: (i, j))],
        out_specs=pl.BlockSpec((tm, tn), lambda i, j, kk: (i, j)),
        compiler_params=pltpu.CompilerParams(
            dimension_semantics=("parallel", "parallel", "arbitrary"), vmem_limit_bytes=VMEM_LIMIT),
        name=name,
    )(x, w, res)


def _merge_kernel(xn_ref, b0_ref, b1_ref, b2_ref, b3_ref, g0_ref, g1_ref, g2_ref, g3_ref,
                  c0_ref, c1_ref, c2_ref, c3_ref, wu_ref, o_ref):
    xn = xn_ref[...]
    acc = None
    branches = (b0_ref, b1_ref, b2_ref, b3_ref)
    gate_w = (g0_ref, g1_ref, g2_ref, g3_ref)
    gate_b = (c0_ref, c1_ref, c2_ref, c3_ref)
    for kb in range(N_BRANCH):
        g = _dot_nt(xn, gate_w[kb][...]) + gate_b[kb][...]
        u = _dot(branches[kb][...], wu_ref[kb])
        t = _sigmoid(g) * u
        acc = t if acc is None else acc + t
    o_ref[...] = acc.astype(o_ref.dtype)


def _merge(xn, branches, w_gate, b_gate, w_up, l, tm, tn):
    m, d = xn.shape
    nj = d // tn
    br_spec = pl.BlockSpec((tm, MIX_W), lambda i, j: (i, 0))

    def gw(kb):
        return pl.BlockSpec((tn, d), lambda i, j: (kb * nj + j, 0))

    def gb(kb):
        return pl.BlockSpec((1, tn), lambda i, j: (0, kb * nj + j))

    return pl.pallas_call(
        _merge_kernel,
        out_shape=jax.ShapeDtypeStruct((m, d), BF16),
        grid=(m // tm, nj),
        in_specs=[pl.BlockSpec((tm, d), lambda i, j: (i, 0)),
                  br_spec, br_spec, br_spec, br_spec,
                  gw(0), gw(1), gw(2), gw(3), gb(0), gb(1), gb(2), gb(3),
                  pl.BlockSpec((None, N_BRANCH, MIX_W, tn), lambda i, j: (l, 0, 0, j))],
        out_specs=pl.BlockSpec((tm, tn), lambda i, j: (i, j)),
        compiler_params=pltpu.CompilerParams(
            dimension_semantics=("parallel", "parallel"), vmem_limit_bytes=VMEM_LIMIT),
        name="merge",
    )(xn, *branches, w_gate, w_gate, w_gate, w_gate, b_gate, b_gate, b_gate, b_gate, w_up)


def _ffn_gu_kernel(x_ref, wg_ref, wu_ref, o_ref):
    x = x_ref[...]
    g = _dot(x, wg_ref[...])
    u = _dot(x, wu_ref[...])
    o_ref[...] = (g * _sigmoid(g) * u).astype(o_ref.dtype)


def _ffn_gate_up(x, wg, wu, l, tm, tn):
    m, k = x.shape
    n = wg.shape[2]
    return pl.pallas_call(
        _ffn_gu_kernel,
        out_shape=jax.ShapeDtypeStruct((m, n), BF16),
        grid=(m // tm, n // tn),
        in_specs=[pl.BlockSpec((tm, k), lambda i, j: (i, 0)),
                  pl.BlockSpec((None, k, tn), lambda i, j: (l, 0, j)),
                  pl.BlockSpec((None, k, tn), lambda i, j: (l, 0, j))],
        out_specs=pl.BlockSpec((tm, tn), lambda i, j: (i, j)),
        compiler_params=pltpu.CompilerParams(
            dimension_semantics=("parallel", "parallel"), vmem_limit_bytes=VMEM_LIMIT),
        name="ffn_gate_up",
    )(x, wg, wu)


def _mlstm_kernel(q_ref, k_ref, v_ref, o_ref, gc_ref, gr_ref, gn_ref, out_ref, c_ref, n_ref, m_ref):
    c = pl.program_id(1)
    T = ML_CHUNK

    @pl.when(c == 0)
    def _():
        c_ref[...] = jnp.zeros_like(c_ref)
        n_ref[...] = jnp.zeros_like(n_ref)
        m_ref[...] = jnp.zeros_like(m_ref)

    valid_col = (c * T + _iota((T, 1), 0)) >= PAD
    valid_row = (c * T + _iota((1, T), 1)) >= PAD
    gc = gc_ref[...]
    gr = gr_ref[...]
    tt = _iota((T, T), 0)
    ss = _iota((T, T), 1)
    causal = ss <= tt
    tril = jnp.where(causal, 1.0, 0.0).astype(BF16)
    triu = jnp.where(tt <= ss, 1.0, 0.0).astype(BF16)
    bc_all = _tri_left(jnp.where(valid_col, _log_sigmoid(gc), 0.0), tril)
    br_all = _tri_right(jnp.where(valid_row, _log_sigmoid(gr), 0.0), triu)

    heads = range(ML_HEADS)
    sl = [slice(h * ML_HD, (h + 1) * ML_HD) for h in heads]
    q = [q_ref[:, sl[h]] * (ML_HD ** -0.5) for h in heads]
    k = [k_ref[:, sl[h]] for h in heads]
    qb = [q[h].astype(BF16) for h in heads]
    vb = [v_ref[:, sl[h]].astype(BF16) for h in heads]
    qk = [_dot_nt(qb[h], k[h].astype(BF16)) for h in heads]
    c_prev = [c_ref[h] for h in heads]
    qc = [_dot(qb[h], c_prev[h].astype(BF16)) for h in heads]

    li_col = [jnp.where(valid_col, gc[:, h:h + 1], NEG) for h in heads]
    li_row = [jnp.where(valid_row, gr[h:h + 1, :], NEG) for h in heads]
    b_col = [bc_all[:, ML_HEADS + h:ML_HEADS + h + 1] for h in heads]
    b_row = [br_all[ML_HEADS + h:ML_HEADS + h + 1, :] for h in heads]
    m_prev = [m_ref[:, h:h + 1] for h in heads]
    dmat = [jnp.where(causal, b_col[h] - b_row[h] + li_row[h], NEG) for h in heads]
    g_col = [b_col[h] + m_prev[h] for h in heads]
    m_out = [jnp.maximum(g_col[h], jnp.max(dmat[h], axis=1, keepdims=True)) for h in heads]

    b_end = [b_row[h][:, T - 1:T] for h in heads]
    wlog = [b_end[h] - b_col[h] + li_col[h] for h in heads]
    m_new = [jnp.maximum(b_end[h] + m_prev[h], jnp.max(wlog[h], axis=0, keepdims=True)) for h in heads]
    decay = [jnp.exp(b_end[h] + m_prev[h] - m_new[h]) for h in heads]
    kw = [k[h] * jnp.exp(wlog[h] - m_new[h]) for h in heads]
    kv = [_dot(kw[h].T.astype(BF16), vb[h]) for h in heads]

    s = [qk[h] * jnp.exp(dmat[h] - m_out[h]) for h in heads]
    inter = [jnp.exp(g_col[h] - m_out[h]) for h in heads]
    sv = [_dot(s[h].astype(BF16), vb[h]) for h in heads]

    for h in heads:
        num = sv[h] + inter[h] * qc[h]
        den = (jnp.sum(s[h], axis=1, keepdims=True)
               + inter[h] * jnp.sum(q[h] * n_ref[h], axis=1, keepdims=True))
        hh = num / jnp.maximum(jnp.abs(den), jnp.exp(-m_out[h]))
        y = hh * lax.rsqrt(jnp.mean(hh * hh, axis=-1, keepdims=True) + EPS) * gn_ref[:, sl[h]]
        out_ref[:, sl[h]] = (y * _sigmoid(o_ref[:, sl[h]])).astype(out_ref.dtype)
        c_ref[h] = decay[h] * c_prev[h] + kv[h]
        n_ref[h] = decay[h] * n_ref[h] + jnp.sum(kw[h], axis=0, keepdims=True)
        m_ref[:, h:h + 1] = m_new[h]


def _mlstm(proj, gates, gates_t, gnorm, casts=()):
    b, lp, _ = proj.shape
    T = ML_CHUNK
    grid = (b, lp // T)

    def col(idx):
        return pl.BlockSpec((None, T, MIX_W), lambda bi, ci: (bi, ci, idx))

    in_specs = [col(0), col(1), col(2), col(3),
                pl.BlockSpec((None, T, LANES), lambda bi, ci: (bi, ci, 0)),
                pl.BlockSpec((None, 8, T), lambda bi, ci: (bi, 0, ci)),
                pl.BlockSpec((1, MIX_W), lambda bi, ci: (0, 0))]
    c_in, c_out, c_shapes, c_ops = _cast_specs(casts, grid)
    return pl.pallas_call(
        _with_casts(_mlstm_kernel, len(in_specs), 1, len(casts)),
        out_shape=[jax.ShapeDtypeStruct((b, lp, MIX_W), BF16)] + c_shapes,
        grid=grid,
        in_specs=in_specs + c_in,
        out_specs=[pl.BlockSpec((None, T, MIX_W), lambda bi, ci: (bi, ci, 0))] + c_out,
        scratch_shapes=[pltpu.VMEM((ML_HEADS, ML_HD, ML_HD), F32),
                        pltpu.VMEM((ML_HEADS, 1, ML_HD), F32),
                        pltpu.VMEM((1, LANES), F32)],
        compiler_params=pltpu.CompilerParams(
            dimension_semantics=("arbitrary", "arbitrary"), vmem_limit_bytes=VMEM_LIMIT),
        name="mlstm",
    )(proj, proj, proj, proj, gates, gates_t, gnorm.reshape(1, MIX_W), *c_ops)


def _sb_kernel(q_ref, k_ref, v_ref, gq_ref, gk_ref, out_ref, kn_ref, vb_ref, acc_ref, run_ref, qn_ref, tri_ref,
               z_ref, a_ref):
    qi = pl.program_id(2)
    T = SB_BLOCK
    W = SB_SPAN
    row0 = pl.multiple_of(qi * T, T)

    @pl.when(jnp.bitwise_and(qi, 1) == 0)
    def _():
        kn_ref[pl.ds(row0 + T, T), :] = jnp.zeros((T, kn_ref.shape[1]), BF16)
        vb_ref[pl.ds(row0 + T, T), :] = jnp.zeros((T, vb_ref.shape[1]), BF16)

    @pl.when(qi == 0)
    def _():
        jj = jnp.bitwise_and(_iota((2 * W, W), 0), W - 1)
        tri_ref[...] = jnp.where(jj > _iota((2 * W, W), 1), 1.0, 0.0).astype(BF16)

    heads = range(SB_HPS)
    sl = [slice(h * SB_HD, (h + 1) * SB_HD) for h in heads]
    vb_ref[pl.ds(row0, T), :] = v_ref[pl.ds(row0, T), :].astype(BF16)
    for h in heads:
        kh = k_ref[pl.ds(row0, T), sl[h]]
        kn = kh * lax.rsqrt(jnp.mean(kh * kh, axis=-1, keepdims=True) + EPS) * gk_ref[...]
        kn_ref[pl.ds(row0, T), sl[h]] = kn.astype(BF16)
        qh = q_ref[:, sl[h]]
        qs = qh * lax.rsqrt(jnp.mean(qh * qh, axis=-1, keepdims=True) + EPS) * (gq_ref[...] * SB_HD ** -0.5)
        qn_ref[:, sl[h]] = qs.astype(BF16)

    acc_ref[...] = jnp.zeros_like(acc_ref)
    run_ref[...] = jnp.zeros_like(run_ref)
    t_pos = row0 + _iota((T, W), 0)
    lane = _iota((T, W), 1)
    top = lax.shift_right_logical(qi, 1)

    def scores(col0, slot):
        for h in heads:
            z_ref[slot, h] = _dot_nt(qn_ref[:, sl[h]], kn_ref[pl.ds(col0, W), sl[h]])

    def weighted_values(col0):
        for h in heads:
            acc_ref[h] += _dot(a_ref[h], vb_ref[pl.ds(col0, W), sl[h]])

    def span_step(it, masked, has_prev):
        col0 = pl.multiple_of((top - it) * W, W)
        slot = it & 1
        z = [z_ref[slot, h] for h in heads]
        if has_prev:
            weighted_values(pl.multiple_of(col0 + W, W))
        l1p = [jnp.log(1.0 + jnp.exp(-jnp.abs(z[h]))) for h in heads]
        ls = [jnp.minimum(z[h], 0.0) - l1p[h] for h in heads]
        lk = [ls[h] - z[h] for h in heads]
        if masked:
            s_pos = col0 + lane
            vis = jnp.logical_and(s_pos < t_pos, s_pos >= PAD)
            lk = [jnp.where(vis, lk[h], 0.0) for h in heads]
        lk_hi = [lk[h].astype(BF16) for h in heads]
        lk_lo = [(lk[h] - lk_hi[h].astype(F32)).astype(BF16) for h in heads]
        cs = [_dot(jnp.concatenate([lk_hi[h], lk_lo[h]], axis=1), tri_ref[...]) for h in heads]
        scores(pl.multiple_of(jnp.maximum(top - it - 1, 0) * W, W), 1 - slot)
        a = [jnp.exp(ls[h] + cs[h] + run_ref[h]) for h in heads]
        if masked:
            a = [jnp.where(vis, a[h], 0.0) for h in heads]
        for h in heads:
            a_ref[h] = a[h].astype(BF16)
            run_ref[h] += jnp.sum(lk[h], axis=1, keepdims=True)

    scores(pl.multiple_of(top * W, W), 0)
    span_step(0, True, False)

    def body(it, carry):
        span_step(it, False, True)
        return carry

    lax.fori_loop(1, top, body, 0)

    @pl.when(top > 0)
    def _():
        span_step(top, True, True)

    weighted_values(0)
    for h in heads:
        out_ref[:, sl[h]] = acc_ref[h].astype(out_ref.dtype)


def _stick_breaking(proj, gq, gk, casts=()):
    b, lp, _ = proj.shape
    T = SB_BLOCK
    wid = SB_HPS * SB_HD
    ng = SB_HEADS // SB_HPS
    per = wid // LANES
    grid = (b, ng, lp // T)
    in_specs = [pl.BlockSpec((None, T, wid), lambda bi, gi, qi: (bi, qi, _SBQ // per + gi)),
                pl.BlockSpec((None, lp, wid), lambda bi, gi, qi: (bi, 0, _SBK // per + gi),
                             pipeline_mode=pl.Buffered(1)),
                pl.BlockSpec((None, lp, wid), lambda bi, gi, qi: (bi, 0, _SBV // per + gi),
                             pipeline_mode=pl.Buffered(1)),
                pl.BlockSpec((1, SB_HD), lambda bi, gi, qi: (0, 0)),
                pl.BlockSpec((1, SB_HD), lambda bi, gi, qi: (0, 0))]
    c_in, c_out, c_shapes, c_ops = _cast_specs(casts, grid)
    return pl.pallas_call(
        _with_casts(_sb_kernel, len(in_specs), 1, len(casts)),
        out_shape=[jax.ShapeDtypeStruct((b, lp, MIX_W), BF16)] + c_shapes,
        grid=grid,
        in_specs=in_specs + c_in,
        out_specs=[pl.BlockSpec((None, T, wid), lambda bi, gi, qi: (bi, qi, gi))] + c_out,
        scratch_shapes=[pltpu.VMEM((lp + SB_SPAN - T, wid), BF16),
                        pltpu.VMEM((lp + SB_SPAN - T, wid), BF16),
                        pltpu.VMEM((SB_HPS, T, SB_HD), F32),
                        pltpu.VMEM((SB_HPS, T, 1), F32),
                        pltpu.VMEM((T, wid), BF16),
                        pltpu.VMEM((2 * SB_SPAN, SB_SPAN), BF16),
                        pltpu.VMEM((2, SB_HPS, T, SB_SPAN), F32),
                        pltpu.VMEM((SB_HPS, T, SB_SPAN), BF16)],
        compiler_params=pltpu.CompilerParams(
            dimension_semantics=("arbitrary", "arbitrary", "arbitrary"), vmem_limit_bytes=VMEM_LIMIT),
        name="stick_breaking",
    )(proj, proj, proj, gq.reshape(1, SB_HD), gk.reshape(1, SB_HD), *c_ops)


def _hgrn2_kernel(q_ref, f_ref, i_ref, g_ref, lb_ref, gn_ref, out_ref, st_ref):
    c = pl.program_id(2)
    T = HG_CHUNK
    S8 = HG_SUB

    @pl.when(c == 0)
    def _():
        st_ref[...] = jnp.zeros_like(st_ref)

    valid = (c * T + _iota((T, 1), 0)) >= PAD
    tril = jnp.where(_iota((T, T), 1) <= _iota((T, T), 0), 1.0, 0.0).astype(BF16)
    lane8 = _iota((S8, T), 1)
    row8 = _iota((S8, T), 0)

    heads = range(HG_HPS)
    hs = [slice(h * HG_HD, (h + 1) * HG_HD) for h in heads]
    lb = [lb_ref[:, hs[h]] for h in heads]
    sig = [_sigmoid(f_ref[:, hs[h]]) for h in heads]
    logf = [jnp.where(valid, jnp.log(lb[h] + (1.0 - lb[h]) * sig[h]), 0.0) for h in heads]
    kk = [jnp.where(valid, (1.0 - lb[h]) * (1.0 - sig[h]), 0.0) for h in heads]
    q = [q_ref[:, hs[h]] * _sigmoid(q_ref[:, hs[h]]) for h in heads]
    iv = [i_ref[:, hs[h]].astype(BF16) for h in heads]
    b = [_tri_left(logf[h], tril) for h in heads]

    st = [st_ref[h] for h in heads]
    b_end = [b[h][T - 1:T] for h in heads]
    o_inter = [_dot_nt((q[h] * jnp.exp(b[h])).astype(BF16), st[h].astype(BF16)) for h in heads]
    st_add = [_dot_tn(iv[h], (kk[h] * jnp.exp(b_end[h] - b[h])).astype(BF16)) for h in heads]

    def diag_block(h, blk):
        r0 = blk * S8
        qb = q[h][r0:r0 + S8]
        bb = b[h][r0:r0 + S8]
        diag = jnp.zeros((S8, T), F32)
        for j in range(S8):
            s_idx = r0 + j
            p = qb * kk[h][s_idx:s_idx + 1] * jnp.exp(jnp.minimum(bb - b[h][s_idx:s_idx + 1], 0.0))
            diag = jnp.where(lane8 == s_idx, jnp.sum(p, axis=1, keepdims=True), diag)
        return jnp.where(lane8 - r0 <= row8, diag, 0.0)

    def off_block(h, blk):
        r0 = blk * S8
        bref = b[h][r0 - 1:r0]
        qs = (q[h][r0:r0 + S8] * jnp.exp(b[h][r0:r0 + S8] - bref)).astype(BF16)
        ks = (kk[h] * jnp.exp(jnp.minimum(bref - b[h], 0.0))).astype(BF16)
        return _dot_nt(qs, ks)

    nblk = T // S8
    off = [[off_block(h, blk) for h in heads] for blk in range(1, nblk)]
    att = []
    for h in heads:
        rows = [diag_block(h, 0)]
        for blk in range(1, nblk):
            rows.append(jnp.where(lane8 < blk * S8, off[blk - 1][h], diag_block(h, blk)))
        att.append(jnp.concatenate(rows, axis=0).astype(BF16))
    o_intra = [_dot(att[h], iv[h]) for h in heads]

    for h in heads:
        o = o_intra[h] + o_inter[h]
        st_ref[h] = st[h] * jnp.exp(b_end[h]) + st_add[h]
        y = o * lax.rsqrt(jnp.mean(o * o, axis=-1, keepdims=True) + EPS) * gn_ref[:, hs[h]]
        out_ref[:, hs[h]] = (y * _sigmoid(g_ref[:, hs[h]])).astype(out_ref.dtype)


def _hgrn2(proj, lb, gnorm, casts=()):
    b, lp, _ = proj.shape
    T = HG_CHUNK
    wid = HG_HPS * HG_HD
    ng = HG_HEADS // HG_HPS
    per = wid // LANES
    grid = (b, ng, lp // T)

    def col(off):
        return pl.BlockSpec((None, T, wid), lambda bi, gi, ci: (bi, ci, off // per + gi))

    vec = pl.BlockSpec((1, wid), lambda bi, gi, ci: (0, gi))
    in_specs = [col(_HGQ), col(_HGF), col(_HGI), col(_HGG), vec, vec]
    c_in, c_out, c_shapes, c_ops = _cast_specs(casts, grid)
    return pl.pallas_call(
        _with_casts(_hgrn2_kernel, len(in_specs), 1, len(casts)),
        out_shape=[jax.ShapeDtypeStruct((b, lp, MIX_W), BF16)] + c_shapes,
        grid=grid,
        in_specs=in_specs + c_in,
        out_specs=[pl.BlockSpec((None, T, wid), lambda bi, gi, ci: (bi, ci, gi))] + c_out,
        scratch_shapes=[pltpu.VMEM((HG_HPS, HG_HD, HG_HD), F32)],
        compiler_params=pltpu.CompilerParams(
            dimension_semantics=("arbitrary", "arbitrary", "arbitrary"), vmem_limit_bytes=VMEM_LIMIT),
        name="hgrn2",
    )(proj, proj, proj, proj, lb.reshape(1, MIX_W), gnorm.reshape(1, MIX_W), *c_ops)


def _rglru_kernel(x_ref, y_ref, cw_ref, cb_ref, w_ref, ba_ref, bx_ref, lam_ref, out_ref, xbuf_ref, h_ref):
    c = pl.program_id(2)
    T = RG_T
    G = RG_GROUP

    @pl.when(c == 0)
    def _():
        xbuf_ref[0:8, :] = jnp.zeros((8, G), F32)
        h_ref[...] = jnp.zeros_like(h_ref)

    valid = (c * T + _iota((T, 1), 0)) >= PAD
    x = jnp.where(valid, x_ref[...], 0.0)
    xbuf_ref[8:8 + T, :] = x
    xc = cb_ref[...] + cw_ref[RG_CONV - 1:RG_CONV, :] * x
    for j in range(1, RG_CONV):
        xc = xc + cw_ref[RG_CONV - 1 - j:RG_CONV - j, :] * xbuf_ref[8 - j:8 - j + T, :]
    xbuf_ref[0:8, :] = x[T - 8:T]

    pre = _dot(xc.astype(BF16), w_ref[...])
    r = _sigmoid(pre[:, :G] + ba_ref[...])
    ig = _sigmoid(pre[:, G:] + bx_ref[...])
    lam = lam_ref[...]
    softplus_neg = jnp.maximum(-lam, 0.0) + jnp.log1p(jnp.exp(-jnp.abs(lam)))
    log_a = -RG_C * r * softplus_neg
    a = jnp.exp(log_a)
    th = jnp.tanh(log_a)
    u = jnp.where(valid, jnp.sqrt(-2.0 * th / (1.0 - th)) * (ig * xc), 0.0)

    row = _iota((T, G), 0)
    shift = 1
    while shift < T:
        keep = row >= shift
        u_s = pltpu.roll(u, shift, axis=0)
        a_s = pltpu.roll(a, shift, axis=0)
        u = jnp.where(keep, a * u_s + u, u)
        a = jnp.where(keep, a * a_s, a)
        shift *= 2
    hseq = a * h_ref[...] + u
    h_ref[...] = hseq[T - 1:T]

    y = y_ref[...]
    gelu = 0.5 * y * (1.0 + jnp.tanh(0.7978845608028654 * (y + 0.044715 * (y * y * y))))
    out_ref[...] = (hseq * gelu).astype(out_ref.dtype)


def _rglru(proj, conv_w, conv_b, w_bd, ba, bx, lam, casts=()):
    b, lp, _ = proj.shape
    T = RG_T
    G = RG_GROUP
    ng = MIX_W // G
    per = G // LANES
    grid = (b, ng, lp // T)

    def col(off):
        return pl.BlockSpec((None, T, G), lambda bi, gi, ci: (bi, ci, off // per + gi))

    vec = pl.BlockSpec((1, G), lambda bi, gi, ci: (0, gi))
    in_specs = [col(_RGX), col(_RGY),
                pl.BlockSpec((RG_CONV, G), lambda bi, gi, ci: (0, gi)),
                vec,
                pl.BlockSpec((None, G, 2 * G), lambda bi, gi, ci: (gi, 0, 0)),
                vec, vec, vec]
    c_in, c_out, c_shapes, c_ops = _cast_specs(casts, grid)
    return pl.pallas_call(
        _with_casts(_rglru_kernel, len(in_specs), 1, len(casts)),
        out_shape=[jax.ShapeDtypeStruct((b, lp, MIX_W), BF16)] + c_shapes,
        grid=grid,
        in_specs=in_specs + c_in,
        out_specs=[pl.BlockSpec((None, T, G), lambda bi, gi, ci: (bi, ci, gi))] + c_out,
        scratch_shapes=[pltpu.VMEM((T + 8, G), F32), pltpu.VMEM((1, G), F32)],
        compiler_params=pltpu.CompilerParams(
            dimension_semantics=("arbitrary", "arbitrary", "arbitrary"), vmem_limit_bytes=VMEM_LIMIT),
        name="rglru",
    )(proj, proj, conv_w, conv_b.reshape(1, MIX_W), w_bd, ba.reshape(1, MIX_W), bx.reshape(1, MIX_W),
      lam.reshape(1, MIX_W), *c_ops)


def _block_diag_gates(wa, wx):
    per = RG_GROUP // RG_BD
    ng = RG_BLOCKS // per
    eye = jnp.eye(per, dtype=F32)

    def bd(w):
        w4 = w.reshape(ng, per, RG_BD, RG_BD)
        return jnp.einsum("gaij,ab->gaibj", w4, eye).reshape(ng, RG_GROUP, RG_GROUP)

    return jnp.concatenate([bd(wa), bd(wx)], axis=-1).astype(BF16)


def _assemble_kernel(x_ref, front_ref, o_ref):
    r = pl.program_id(1)

    @pl.when(r == 0)
    def _():
        o_ref[...] = front_ref[...]

    @pl.when(r > 0)
    def _():
        o_ref[...] = x_ref[...]


def _assemble(x, front):
    b, seq, d = x.shape
    tr = PAD + N_META
    return pl.pallas_call(
        _assemble_kernel,
        out_shape=jax.ShapeDtypeStruct((b, tr + seq, d), x.dtype),
        grid=(b, 1 + seq // tr),
        in_specs=[pl.BlockSpec((None, tr, d), lambda bi, r: (bi, jnp.maximum(r - 1, 0), 0)),
                  pl.BlockSpec((tr, d), lambda bi, r: (0, 0))],
        out_specs=pl.BlockSpec((None, tr, d), lambda bi, r: (bi, r, 0)),
        compiler_params=pltpu.CompilerParams(dimension_semantics=("parallel", "parallel")),
        name="assemble",
    )(x, front)


def _layer(h, l, wt_in_bf, p, lower_bounds, tiles):
    b, lp = p["b"], p["lp"]
    m = b * lp
    tm = tiles["tm"]

    depth = p["wt_in"].shape[0]
    b_in = p["b_in"][l]
    if_lo = N_ML
    if_hi = N_ML + 2 * ML_HEADS
    gate_lo = if_hi + N_REST
    wt_rest = wt_in_bf[if_hi:gate_lo]
    wt_gate = wt_in_bf[gate_lo:]
    wt_if = jnp.pad(wt_in_bf[if_lo:if_hi], ((0, LANES - 2 * ML_HEADS), (0, 0)))
    b_if = jnp.pad(b_in[if_lo:if_hi], (0, LANES - 2 * ML_HEADS)).reshape(1, LANES)

    xn = _rmsnorm(h, p["norm_mix"][l])
    proj_ml = _linear_bias(xn, wt_in_bf, b_in[:N_ML].reshape(1, N_ML), tm, tiles["tn_mix"], "in_proj_ml")
    proj_rest = _linear_bias(xn, wt_rest, b_in[if_hi:gate_lo].reshape(1, N_REST), tm, tiles["tn_mix"],
                             "in_proj_rest")
    proj_ml = proj_ml.reshape(b, lp, N_ML)
    proj_rest = proj_rest.reshape(b, lp, N_REST)
    gates = _linear_bias(xn, wt_if, b_if, tm, LANES, "in_proj_gates").reshape(b, lp, LANES)
    gates_t = jnp.swapaxes(gates[:, :, :8], 1, 2)

    w_up_stack = p["w_up"].reshape(depth, N_BRANCH * MIX_W, D_MODEL)
    sb_casts = [(p["w_ffn_down"], l, 128)]
    if l + 1 < depth:
        sb_casts.append((p["wt_in"], l + 1, 240))
    ml_out, w_out_bf, w_up_bf = _mlstm(proj_ml, gates, gates_t, p["ml_norm"][l],
                                       casts=[(p["w_out"], l, 64), (w_up_stack, l, 64)])
    sb_out, w_down_bf, *w_in_next = _stick_breaking(proj_rest, p["sb_q_norm"][l], p["sb_k_norm"][l],
                                                    casts=sb_casts)
    hg_out, w_fgate_bf = _hgrn2(proj_rest, lower_bounds[l], p["hg_norm"][l], casts=[(p["w_ffn_gate"], l, 16)])
    rg_out, w_fup_bf = _rglru(proj_rest, p["rg_conv_w"][l], p["rg_conv_b"][l],
                              _block_diag_gates(p["rg_wa"][l], p["rg_wx"][l]),
                              p["rg_ba"][l], p["rg_bx"][l], p["rg_lambda"][l], casts=[(p["w_ffn_up"], l, 16)])
    branches = tuple(br.reshape(m, MIX_W) for br in (ml_out, sb_out, hg_out, rg_out))
    merged = _merge(xn, branches, wt_gate, b_in[gate_lo:].reshape(1, N_BRANCH * D_MODEL),
                    w_up_bf.reshape(1, N_BRANCH, MIX_W, D_MODEL), 0, tiles["tm_merge"], tiles["tn_merge"])
    h = _linear_res(merged, w_out_bf[None], 0, h, tm, tiles["tn_out"], D_MODEL, "out_proj")

    d_ff = p["w_ffn_gate"].shape[-1]
    hn = _rmsnorm(h, p["norm_ffn"][l])
    act = _ffn_gate_up(hn, w_fgate_bf[None], w_fup_bf[None], 0, tiles["tm_ff"], tiles["tn_ff"])
    h = _linear_res(act, w_down_bf[None], 0, h, tiles["tm_down"], tiles["tn_down"], d_ff // tiles["nk_down"],
                    "ffn_down")
    return h, (w_in_next[0] if w_in_next else None)


def _forward(x, meta, params, tiles):
    b, seq, d = x.shape
    lp = PAD + N_META + seq
    depth = params["w_in"].shape[0]
    hg_lb = params["hg_lb"]
    p_lb = jax.nn.softmax(hg_lb.astype(F32), axis=0)
    lower_bounds = jnp.clip(jnp.cumsum(p_lb, axis=0) - p_lb[0:1], 0.0, 0.999)

    front = jnp.concatenate([jnp.zeros((PAD, d), x.dtype), meta.astype(x.dtype)], axis=0)
    h = _assemble(x, front).reshape(b * lp, d)
    p = dict(params, b=b, lp=lp, wt_in=jnp.swapaxes(params["w_in"], 1, 2))
    wt_in_bf = p["wt_in"][0].astype(BF16)
    for l in range(depth):
        h, wt_in_bf = _layer(h, l, wt_in_bf, p, lower_bounds, tiles)
    return h.reshape(b, lp, d)[:, PAD + N_META:]


_TILES = dict(tm=1056, tn_mix=1024, tm_merge=768, tn_merge=256, tn_out=512,
              tm_ff=2112, tn_ff=256, tm_down=768, tn_down=256, nk_down=1)


def kernel(x, meta, norm_mix, norm_ffn, w_in, b_in, ml_norm, sb_q_norm, sb_k_norm, hg_lb, hg_norm,
           rg_conv_w, rg_conv_b, rg_wa, rg_ba, rg_wx, rg_bx, rg_lambda, w_up, w_out,
           w_ffn_gate, w_ffn_up, w_ffn_down):
    params = dict(norm_mix=norm_mix, norm_ffn=norm_ffn, w_in=w_in, b_in=b_in, ml_norm=ml_norm,
                  sb_q_norm=sb_q_norm, sb_k_norm=sb_k_norm, hg_lb=hg_lb, hg_norm=hg_norm,
                  rg_conv_w=rg_conv_w, rg_conv_b=rg_conv_b, rg_wa=rg_wa, rg_ba=rg_ba, rg_wx=rg_wx,
                  rg_bx=rg_bx, rg_lambda=rg_lambda, w_up=w_up, w_out=w_out,
                  w_ffn_gate=w_ffn_gate, w_ffn_up=w_ffn_up, w_ffn_down=w_ffn_down)
    return _forward(x, meta, params, _TILES)
```

```python
import functools

import jax
import jax.numpy as jnp
from jax import lax
from jax.experimental import pallas as pl
from jax.experimental.pallas import tpu as pltpu

F32 = jnp.float32
BF16 = jnp.bfloat16

D_MODEL = 4096
N_META = 16
N_BRANCH = 4
MIX_W = D_MODEL // 4
ML_HEADS = 4
ML_HD = MIX_W // ML_HEADS
SB_HEADS = 8
SB_HD = MIX_W // SB_HEADS
HG_HEADS = 8
HG_HD = MIX_W // HG_HEADS
RG_BLOCKS = 16
RG_BD = MIX_W // RG_BLOCKS
RG_CONV = 4
RG_C = 8.0
EPS = 1e-6
NEG = -1e30

LANES = 128
PAD = LANES - N_META
ML_CHUNK = 128
SB_BLOCK = 128
SB_SPAN = 256
SB_HPS = 4
HG_CHUNK = 64
HG_SUB = 8
HG_HPS = 4
RG_GROUP = 256
RG_T = 128
VMEM_LIMIT = 56 * 1024 * 1024

_SBQ, _SBK, _SBV = 0, 8, 16
_HGQ, _HGF, _HGI, _HGG = 24, 32, 40, 48
_RGX, _RGY = 56, 64
N_ML = 4 * MIX_W
N_REST = 9 * MIX_W
_IN_SPLITS = (N_ML, N_ML + 2 * ML_HEADS, N_ML + 2 * ML_HEADS + N_REST)


def _sigmoid(x):
    return 1.0 / (1.0 + jnp.exp(-x))


def _log_sigmoid(x):
    return jnp.minimum(x, 0.0) - jnp.log1p(jnp.exp(-jnp.abs(x)))


def _split3(x):
    hi = x.astype(BF16)
    r1 = x - hi.astype(F32)
    mid = r1.astype(BF16)
    lo = (r1 - mid.astype(F32)).astype(BF16)
    return hi, mid, lo


def _dot(a, b):
    return jnp.dot(a, b, preferred_element_type=F32)


def _dot_nt(a, b):
    return lax.dot_general(a, b, (((1,), (1,)), ((), ())), preferred_element_type=F32)


def _dot_tn(a, b):
    return lax.dot_general(a, b, (((0,), (0,)), ((), ())), preferred_element_type=F32)


def _tri_left(x, tri):
    hi, mid, lo = _split3(x)
    return _dot(tri, hi) + _dot(tri, mid) + _dot(tri, lo)


def _tri_right(x, tri):
    hi, mid, lo = _split3(x)
    return _dot(hi, tri) + _dot(mid, tri) + _dot(lo, tri)


def _iota(shape, dim):
    return lax.broadcasted_iota(jnp.int32, shape, dim)


def _cast_specs(casts, grid):
    steps = 1
    for g in grid:
        steps *= g

    def step_of(*idx):
        s = idx[0]
        for g, i in zip(grid[1:], idx[1:]):
            s = s * g + i
        return s

    in_specs, out_specs, out_shapes, operands = [], [], [], []
    for w, layer, rows, *span in casts:
        _, r, c = w.shape
        row0, nrows = span if span else (0, r)
        nb = nrows // rows
        assert nb * rows == nrows and nb <= steps and row0 % 8 == 0, (w.shape, rows, span, steps)
        if row0 == 0:
            in_specs.append(pl.BlockSpec(
                (None, rows, c),
                lambda *idx, layer=layer, nb=nb: (layer, jnp.minimum(step_of(*idx), nb - 1), 0)))
        else:
            in_specs.append(pl.BlockSpec(
                (None, pl.Element(rows), pl.Element(c)),
                lambda *idx, layer=layer, nb=nb, row0=row0, rows=rows:
                (layer, (row0 // 8 + jnp.minimum(step_of(*idx), nb - 1) * (rows // 8)) * 8, 0)))
        out_specs.append(pl.BlockSpec(
            (rows, c), lambda *idx, nb=nb: (jnp.minimum(step_of(*idx), nb - 1), 0)))
        out_shapes.append(jax.ShapeDtypeStruct((nrows, c), BF16))
        operands.append(w)
    return in_specs, out_specs, out_shapes, operands


def _with_casts(body, n_in, n_out, n_cast):
    def kernel(*refs):
        ins = refs[:n_in]
        cast_in = refs[n_in:n_in + n_cast]
        outs = refs[n_in + n_cast:n_in + n_cast + n_out]
        cast_out = refs[n_in + n_cast + n_out:n_in + 2 * n_cast + n_out]
        scratch = refs[n_in + 2 * n_cast + n_out:]
        for src, dst in zip(cast_in, cast_out):
            dst[...] = src[...].astype(BF16)
        body(*ins, *outs, *scratch)
    return kernel


def _rmsnorm_kernel(x_ref, g_ref, o_ref):
    x = x_ref[...]
    y = x * lax.rsqrt(jnp.mean(x * x, axis=-1, keepdims=True) + EPS)
    o_ref[...] = (y * g_ref[...]).astype(o_ref.dtype)


def _rmsnorm(x, g, tm=256):
    m, d = x.shape
    return pl.pallas_call(
        _rmsnorm_kernel,
        out_shape=jax.ShapeDtypeStruct((m, d), BF16),
        grid=(m // tm,),
        in_specs=[pl.BlockSpec((tm, d), lambda i: (i, 0)),
                  pl.BlockSpec((1, d), lambda i: (0, 0))],
        out_specs=pl.BlockSpec((tm, d), lambda i: (i, 0)),
        compiler_params=pltpu.CompilerParams(dimension_semantics=("parallel",)),
        name="rmsnorm",
    )(x, g.reshape(1, d))


def _linear_bias_kernel(x_ref, wt_ref, b_ref, o_ref):
    o_ref[...] = _dot_nt(x_ref[...], wt_ref[...].astype(BF16)) + b_ref[...]


def _linear_bias(x, wt, b, tm, tn, name):
    m, k = x.shape
    n = b.shape[1]
    return pl.pallas_call(
        _linear_bias_kernel,
        out_shape=jax.ShapeDtypeStruct((m, n), F32),
        grid=(m // tm, n // tn),
        in_specs=[pl.BlockSpec((tm, k), lambda i, j: (i, 0)),
                  pl.BlockSpec((tn, k), lambda i, j: (j, 0)),
                  pl.BlockSpec((1, tn), lambda i, j: (0, j))],
        out_specs=pl.BlockSpec((tm, tn), lambda i, j: (i, j)),
        compiler_params=pltpu.CompilerParams(
            dimension_semantics=("parallel", "parallel"), vmem_limit_bytes=VMEM_LIMIT),
        name=name,
    )(x, wt, b)


def _linear_res_kernel(x_ref, w_ref, r_ref, o_ref, *, nk):
    part = _dot(x_ref[...], w_ref[...])
    if nk == 1:
        o_ref[...] = r_ref[...] + part
    else:
        k = pl.program_id(2)

        @pl.when(k == 0)
        def _():
            o_ref[...] = r_ref[...] + part

        @pl.when(k > 0)
        def _():
            o_ref[...] += part


def _linear_res(x, w, l, res, tm, tn, tk, name):
    m, k = x.shape
    n = w.shape[2]
    nk = k // tk
    return pl.pallas_call(
        functools.partial(_linear_res_kernel, nk=nk),
        out_shape=jax.ShapeDtypeStruct((m, n), F32),
        grid=(m // tm, n // tn, nk),
        in_specs=[pl.BlockSpec((tm, tk), lambda i, j, kk: (i, kk)),
                  pl.BlockSpec((None, tk, tn), lambda i, j, kk: (l, kk, j)),
                  pl.BlockSpec((tm, tn), lambda i, j, kk: (i, j))],
        out_specs=pl.BlockSpec((tm, tn), lambda i, j, kk: (i, j)),
        compiler_params=pltpu.CompilerParams(
            dimension_semantics=("parallel", "parallel", "arbitrary"), vmem_limit_bytes=VMEM_LIMIT),
        name=name,
    )(x, w, res)


def _merge_kernel(xn_ref, b0_ref, b1_ref, b2_ref, b3_ref, g0_ref, g1_ref, g2_ref, g3_ref,
                  c0_ref, c1_ref, c2_ref, c3_ref, wu_ref, o_ref):
    xn = xn_ref[...]
    acc = None
    branches = (b0_ref, b1_ref, b2_ref, b3_ref)
    gate_w = (g0_ref, g1_ref, g2_ref, g3_ref)
    gate_b = (c0_ref, c1_ref, c2_ref, c3_ref)
    for kb in range(N_BRANCH):
        g = _dot_nt(xn, gate_w[kb][...]) + gate_b[kb][...]
        u = _dot(branches[kb][...], wu_ref[kb])
        t = _sigmoid(g) * u
        acc = t if acc is None else acc + t
    o_ref[...] = acc.astype(o_ref.dtype)


def _merge(xn, branches, w_gate, b_gate, w_up, l, tm, tn):
    m, d = xn.shape
    nj = d // tn
    br_spec = pl.BlockSpec((tm, MIX_W), lambda i, j: (i, 0))

    def gw(kb):
        return pl.BlockSpec((tn, d), lambda i, j: (kb * nj + j, 0))

    def gb(kb):
        return pl.BlockSpec((1, tn), lambda i, j: (0, kb * nj + j))

    return pl.pallas_call(
        _merge_kernel,
        out_shape=jax.ShapeDtypeStruct((m, d), BF16),
        grid=(m // tm, nj),
        in_specs=[pl.BlockSpec((tm, d), lambda i, j: (i, 0)),
                  br_spec, br_spec, br_spec, br_spec,
                  gw(0), gw(1), gw(2), gw(3), gb(0), gb(1), gb(2), gb(3),
                  pl.BlockSpec((None, N_BRANCH, MIX_W, tn), lambda i, j: (l, 0, 0, j))],
        out_specs=pl.BlockSpec((tm, tn), lambda i, j: (i, j)),
        compiler_params=pltpu.CompilerParams(
            dimension_semantics=("parallel", "parallel"), vmem_limit_bytes=VMEM_LIMIT),
        name="merge",
    )(xn, *branches, w_gate, w_gate, w_gate, w_gate, b_gate, b_gate, b_gate, b_gate, w_up)


def _ffn_gu_kernel(x_ref, wg_ref, wu_ref, o_ref):
    x = x_ref[...]
    g = _dot(x, wg_ref[...])
    u = _dot(x, wu_ref[...])
    o_ref[...] = (g * _sigmoid(g) * u).astype(o_ref.dtype)


def _ffn_gate_up(x, wg, wu, l, tm, tn):
    m, k = x.shape
    n = wg.shape[2]
    return pl.pallas_call(
        _ffn_gu_kernel,
        out_shape=jax.ShapeDtypeStruct((m, n), BF16),
        grid=(m // tm, n // tn),
        in_specs=[pl.BlockSpec((tm, k), lambda i, j: (i, 0)),
                  pl.BlockSpec((None, k, tn), lambda i, j: (l, 0, j)),
                  pl.BlockSpec((None, k, tn), lambda i, j: (l, 0, j))],
        out_specs=pl.BlockSpec((tm, tn), lambda i, j: (i, j)),
        compiler_params=pltpu.CompilerParams(
            dimension_semantics=("parallel", "parallel"), vmem_limit_bytes=VMEM_LIMIT),
        name="ffn_gate_up",
    )(x, wg, wu)


def _mlstm_kernel(q_ref, k_ref, v_ref, o_ref, gc_ref, gr_ref, gn_ref, out_ref, c_ref, n_ref, m_ref):
    c = pl.program_id(1)
    T = ML_CHUNK

    @pl.when(c == 0)
    def _():
        c_ref[...] = jnp.zeros_like(c_ref)
        n_ref[...] = jnp.zeros_like(n_ref)
        m_ref[...] = jnp.zeros_like(m_ref)

    valid_col = (c * T + _iota((T, 1), 0)) >= PAD
    valid_row = (c * T + _iota((1, T), 1)) >= PAD
    gc = gc_ref[...]
    gr = gr_ref[...]
    tt = _iota((T, T), 0)
    ss = _iota((T, T), 1)
    causal = ss <= tt
    tril = jnp.where(causal, 1.0, 0.0).astype(BF16)
    triu = jnp.where(tt <= ss, 1.0, 0.0).astype(BF16)
    bc_all = _tri_left(jnp.where(valid_col, _log_sigmoid(gc), 0.0), tril)
    br_all = _tri_right(jnp.where(valid_row, _log_sigmoid(gr), 0.0), triu)

    heads = range(ML_HEADS)
    sl = [slice(h * ML_HD, (h + 1) * ML_HD) for h in heads]
    q = [q_ref[:, sl[h]] * (ML_HD ** -0.5) for h in heads]
    k = [k_ref[:, sl[h]] for h in heads]
    qb = [q[h].astype(BF16) for h in heads]
    vb = [v_ref[:, sl[h]].astype(BF16) for h in heads]
    qk = [_dot_nt(qb[h], k[h].astype(BF16)) for h in heads]
    c_prev = [c_ref[h] for h in heads]
    qc = [_dot(qb[h], c_prev[h].astype(BF16)) for h in heads]

    li_col = [jnp.where(valid_col, gc[:, h:h + 1], NEG) for h in heads]
    li_row = [jnp.where(valid_row, gr[h:h + 1, :], NEG) for h in heads]
    b_col = [bc_all[:, ML_HEADS + h:ML_HEADS + h + 1] for h in heads]
    b_row = [br_all[ML_HEADS + h:ML_HEADS + h + 1, :] for h in heads]
    m_prev = [m_ref[:, h:h + 1] for h in heads]
    dmat = [jnp.where(causal, b_col[h] - b_row[h] + li_row[h], NEG) for h in heads]
    g_col = [b_col[h] + m_prev[h] for h in heads]
    m_out = [jnp.maximum(g_col[h], jnp.max(dmat[h], axis=1, keepdims=True)) for h in heads]

    b_end = [b_row[h][:, T - 1:T] for h in heads]
    wlog = [b_end[h] - b_col[h] + li_col[h] for h in heads]
    m_new = [jnp.maximum(b_end[h] + m_prev[h], jnp.max(wlog[h], axis=0, keepdims=True)) for h in heads]
    decay = [jnp.exp(b_end[h] + m_prev[h] - m_new[h]) for h in heads]
    kw = [k[h] * jnp.exp(wlog[h] - m_new[h]) for h in heads]
    kv = [_dot(kw[h].T.astype(BF16), vb[h]) for h in heads]

    s = [qk[h] * jnp.exp(dmat[h] - m_out[h]) for h in heads]
    inter = [jnp.exp(g_col[h] - m_out[h]) for h in heads]
    sv = [_dot(s[h].astype(BF16), vb[h]) for h in heads]

    for h in heads:
        num = sv[h] + inter[h] * qc[h]
        den = (jnp.sum(s[h], axis=1, keepdims=True)
               + inter[h] * jnp.sum(q[h] * n_ref[h], axis=1, keepdims=True))
        hh = num / jnp.maximum(jnp.abs(den), jnp.exp(-m_out[h]))
        y = hh * lax.rsqrt(jnp.mean(hh * hh, axis=-1, keepdims=True) + EPS) * gn_ref[:, sl[h]]
        out_ref[:, sl[h]] = (y * _sigmoid(o_ref[:, sl[h]])).astype(out_ref.dtype)
        c_ref[h] = decay[h] * c_prev[h] + kv[h]
        n_ref[h] = decay[h] * n_ref[h] + jnp.sum(kw[h], axis=0, keepdims=True)
        m_ref[:, h:h + 1] = m_new[h]


def _mlstm(proj, gates, gates_t, gnorm, casts=()):
    b, lp, _ = proj.shape
    T = ML_CHUNK
    grid = (b, lp // T)

    def col(idx):
        return pl.BlockSpec((None, T, MIX_W), lambda bi, ci: (bi, ci, idx))

    in_specs = [col(0), col(1), col(2), col(3),
                pl.BlockSpec((None, T, LANES), lambda bi, ci: (bi, ci, 0)),
                pl.BlockSpec((None, 8, T), lambda bi, ci: (bi, 0, ci)),
                pl.BlockSpec((1, MIX_W), lambda bi, ci: (0, 0))]
    c_in, c_out, c_shapes, c_ops = _cast_specs(casts, grid)
    return pl.pallas_call(
        _with_casts(_mlstm_kernel, len(in_specs), 1, len(casts)),
        out_shape=[jax.ShapeDtypeStruct((b, lp, MIX_W), BF16)] + c_shapes,
        grid=grid,
        in_specs=in_specs + c_in,
        out_specs=[pl.BlockSpec((None, T, MIX_W), lambda bi, ci: (bi, ci, 0))] + c_out,
        scratch_shapes=[pltpu.VMEM((ML_HEADS, ML_HD, ML_HD), F32),
                        pltpu.VMEM((ML_HEADS, 1, ML_HD), F32),
                        pltpu.VMEM((1, LANES), F32)],
        compiler_params=pltpu.CompilerParams(
            dimension_semantics=("arbitrary", "arbitrary"), vmem_limit_bytes=VMEM_LIMIT),
        name="mlstm",
    )(proj, proj, proj, proj, gates, gates_t, gnorm.reshape(1, MIX_W), *c_ops)


def _sb_kernel(q_ref, k_ref, v_ref, gq_ref, gk_ref, out_ref, kn_ref, vb_ref, acc_ref, run_ref, qn_ref, tri_ref,
               z_ref, a_ref):
    qi = pl.program_id(2)
    T = SB_BLOCK
    W = SB_SPAN
    row0 = pl.multiple_of(qi * T, T)

    @pl.when(jnp.bitwise_and(qi, 1) == 0)
    def _():
        kn_ref[pl.ds(row0 + T, T), :] = jnp.zeros((T, kn_ref.shape[1]), BF16)
        vb_ref[pl.ds(row0 + T, T), :] = jnp.zeros((T, vb_ref.shape[1]), BF16)

    @pl.when(qi == 0)
    def _():
        jj = jnp.bitwise_and(_iota((2 * W, W), 0), W - 1)
        tri_ref[...] = jnp.where(jj > _iota((2 * W, W), 1), 1.0, 0.0).astype(BF16)

    heads = range(SB_HPS)
    sl = [slice(h * SB_HD, (h + 1) * SB_HD) for h in heads]
    vb_ref[pl.ds(row0, T), :] = v_ref[...].astype(BF16)
    for h in heads:
        kh = k_ref[:, sl[h]]
        kn = kh * lax.rsqrt(jnp.mean(kh * kh, axis=-1, keepdims=True) + EPS) * gk_ref[...]
        kn_ref[pl.ds(row0, T), sl[h]] = kn.astype(BF16)
        qh = q_ref[:, sl[h]]
        qs = qh * lax.rsqrt(jnp.mean(qh * qh, axis=-1, keepdims=True) + EPS) * (gq_ref[...] * SB_HD ** -0.5)
        qn_ref[:, sl[h]] = qs.astype(BF16)

    acc_ref[...] = jnp.zeros_like(acc_ref)
    run_ref[...] = jnp.zeros_like(run_ref)
    t_pos = row0 + _iota((T, W), 0)
    lane = _iota((T, W), 1)
    top = lax.shift_right_logical(qi, 1)

    def scores(col0, slot):
        for h in heads:
            z_ref[slot, h] = _dot_nt(qn_ref[:, sl[h]], kn_ref[pl.ds(col0, W), sl[h]])

    def weighted_values(col0):
        for h in heads:
            acc_ref[h] += _dot(a_ref[h], vb_ref[pl.ds(col0, W), sl[h]])

    def span_step(it, masked, has_prev):
        col0 = pl.multiple_of((top - it) * W, W)
        slot = it & 1
        z = [z_ref[slot, h] for h in heads]
        if has_prev:
            weighted_values(pl.multiple_of(col0 + W, W))
        l1p = [jnp.log(1.0 + jnp.exp(-jnp.abs(z[h]))) for h in heads]
        ls = [jnp.minimum(z[h], 0.0) - l1p[h] for h in heads]
        lk = [ls[h] - z[h] for h in heads]
        if masked:
            s_pos = col0 + lane
            vis = jnp.logical_and(s_pos < t_pos, s_pos >= PAD)
            lk = [jnp.where(vis, lk[h], 0.0) for h in heads]
        lk_hi = [lk[h].astype(BF16) for h in heads]
        lk_lo = [(lk[h] - lk_hi[h].astype(F32)).astype(BF16) for h in heads]
        cs = [_dot(jnp.concatenate([lk_hi[h], lk_lo[h]], axis=1), tri_ref[...]) for h in heads]
        scores(pl.multiple_of(jnp.maximum(top - it - 1, 0) * W, W), 1 - slot)
        a = [jnp.exp(ls[h] + cs[h] + run_ref[h]) for h in heads]
        if masked:
            a = [jnp.where(vis, a[h], 0.0) for h in heads]
        for h in heads:
            a_ref[h] = a[h].astype(BF16)
            run_ref[h] += jnp.sum(lk[h], axis=1, keepdims=True)

    scores(pl.multiple_of(top * W, W), 0)
    span_step(0, True, False)

    def body(it, carry):
        span_step(it, False, True)
        return carry

    lax.fori_loop(1, top, body, 0)

    @pl.when(top > 0)
    def _():
        span_step(top, True, True)

    weighted_values(0)
    for h in heads:
        out_ref[:, sl[h]] = acc_ref[h].astype(out_ref.dtype)


def _stick_breaking(proj, gq, gk, casts=()):
    b, lp, _ = proj.shape
    T = SB_BLOCK
    wid = SB_HPS * SB_HD
    ng = SB_HEADS // SB_HPS
    per = wid // LANES
    grid = (b, ng, lp // T)
    in_specs = [pl.BlockSpec((None, T, wid), lambda bi, gi, qi: (bi, qi, _SBQ // per + gi)),
                pl.BlockSpec((None, T, wid), lambda bi, gi, qi: (bi, qi, _SBK // per + gi)),
                pl.BlockSpec((None, T, wid), lambda bi, gi, qi: (bi, qi, _SBV // per + gi)),
                pl.BlockSpec((1, SB_HD), lambda bi, gi, qi: (0, 0)),
                pl.BlockSpec((1, SB_HD), lambda bi, gi, qi: (0, 0))]
    c_in, c_out, c_shapes, c_ops = _cast_specs(casts, grid)
    return pl.pallas_call(
        _with_casts(_sb_kernel, len(in_specs), 1, len(casts)),
        out_shape=[jax.ShapeDtypeStruct((b, lp, MIX_W), BF16)] + c_shapes,
        grid=grid,
        in_specs=in_specs + c_in,
        out_specs=[pl.BlockSpec((None, T, wid), lambda bi, gi, qi: (bi, qi, gi))] + c_out,
        scratch_shapes=[pltpu.VMEM((lp + SB_SPAN - T, wid), BF16),
                        pltpu.VMEM((lp + SB_SPAN - T, wid), BF16),
                        pltpu.VMEM((SB_HPS, T, SB_HD), F32),
                        pltpu.VMEM((SB_HPS, T, 1), F32),
                        pltpu.VMEM((T, wid), BF16),
                        pltpu.VMEM((2 * SB_SPAN, SB_SPAN), BF16),
                        pltpu.VMEM((2, SB_HPS, T, SB_SPAN), F32),
                        pltpu.VMEM((SB_HPS, T, SB_SPAN), BF16)],
        compiler_params=pltpu.CompilerParams(
            dimension_semantics=("arbitrary", "arbitrary", "arbitrary"), vmem_limit_bytes=VMEM_LIMIT),
        name="stick_breaking",
    )(proj, proj, proj, gq.reshape(1, SB_HD), gk.reshape(1, SB_HD), *c_ops)


def _hgrn2_kernel(q_ref, f_ref, i_ref, g_ref, lb_ref, gn_ref, out_ref, st_ref):
    c = pl.program_id(2)
    T = HG_CHUNK
    S8 = HG_SUB

    @pl.when(c == 0)
    def _():
        st_ref[...] = jnp.zeros_like(st_ref)

    valid = (c * T + _iota((T, 1), 0)) >= PAD
    tril = jnp.where(_iota((T, T), 1) <= _iota((T, T), 0), 1.0, 0.0).astype(BF16)
    lane8 = _iota((S8, T), 1)
    row8 = _iota((S8, T), 0)

    heads = range(HG_HPS)
    hs = [slice(h * HG_HD, (h + 1) * HG_HD) for h in heads]
    lb = [lb_ref[:, hs[h]] for h in heads]
    sig = [_sigmoid(f_ref[:, hs[h]]) for h in heads]
    logf = [jnp.where(valid, jnp.log(lb[h] + (1.0 - lb[h]) * sig[h]), 0.0) for h in heads]
    kk = [jnp.where(valid, (1.0 - lb[h]) * (1.0 - sig[h]), 0.0) for h in heads]
    q = [q_ref[:, hs[h]] * _sigmoid(q_ref[:, hs[h]]) for h in heads]
    iv = [i_ref[:, hs[h]].astype(BF16) for h in heads]
    b = [_tri_left(logf[h], tril) for h in heads]

    st = [st_ref[h] for h in heads]
    b_end = [b[h][T - 1:T] for h in heads]
    o_inter = [_dot_nt((q[h] * jnp.exp(b[h])).astype(BF16), st[h].astype(BF16)) for h in heads]
    st_add = [_dot_tn(iv[h], (kk[h] * jnp.exp(b_end[h] - b[h])).astype(BF16)) for h in heads]

    def diag_block(h, blk):
        r0 = blk * S8
        qb = q[h][r0:r0 + S8]
        bb = b[h][r0:r0 + S8]
        diag = jnp.zeros((S8, T), F32)
        for j in range(S8):
            s_idx = r0 + j
            p = qb * kk[h][s_idx:s_idx + 1] * jnp.exp(jnp.minimum(bb - b[h][s_idx:s_idx + 1], 0.0))
            diag = jnp.where(lane8 == s_idx, jnp.sum(p, axis=1, keepdims=True), diag)
        return jnp.where(lane8 - r0 <= row8, diag, 0.0)

    def off_block(h, blk):
        r0 = blk * S8
        bref = b[h][r0 - 1:r0]
        qs = (q[h][r0:r0 + S8] * jnp.exp(b[h][r0:r0 + S8] - bref)).astype(BF16)
        ks = (kk[h] * jnp.exp(jnp.minimum(bref - b[h], 0.0))).astype(BF16)
        return _dot_nt(qs, ks)

    nblk = T // S8
    off = [[off_block(h, blk) for h in heads] for blk in range(1, nblk)]
    att = []
    for h in heads:
        rows = [diag_block(h, 0)]
        for blk in range(1, nblk):
            rows.append(jnp.where(lane8 < blk * S8, off[blk - 1][h], diag_block(h, blk)))
        att.append(jnp.concatenate(rows, axis=0).astype(BF16))
    o_intra = [_dot(att[h], iv[h]) for h in heads]

    for h in heads:
        o = o_intra[h] + o_inter[h]
        st_ref[h] = st[h] * jnp.exp(b_end[h]) + st_add[h]
        y = o * lax.rsqrt(jnp.mean(o * o, axis=-1, keepdims=True) + EPS) * gn_ref[:, hs[h]]
        out_ref[:, hs[h]] = (y * _sigmoid(g_ref[:, hs[h]])).astype(out_ref.dtype)


def _hgrn2(proj, lb, gnorm, casts=()):
    b, lp, _ = proj.shape
    T = HG_CHUNK
    wid = HG_HPS * HG_HD
    ng = HG_HEADS // HG_HPS
    per = wid // LANES
    grid = (b, ng, lp // T)

    def col(off):
        return pl.BlockSpec((None, T, wid), lambda bi, gi, ci: (bi, ci, off // per + gi))

    vec = pl.BlockSpec((1, wid), lambda bi, gi, ci: (0, gi))
    in_specs = [col(_HGQ), col(_HGF), col(_HGI), col(_HGG), vec, vec]
    c_in, c_out, c_shapes, c_ops = _cast_specs(casts, grid)
    return pl.pallas_call(
        _with_casts(_hgrn2_kernel, len(in_specs), 1, len(casts)),
        out_shape=[jax.ShapeDtypeStruct((b, lp, MIX_W), BF16)] + c_shapes,
        grid=grid,
        in_specs=in_specs + c_in,
        out_specs=[pl.BlockSpec((None, T, wid), lambda bi, gi, ci: (bi, ci, gi))] + c_out,
        scratch_shapes=[pltpu.VMEM((HG_HPS, HG_HD, HG_HD), F32)],
        compiler_params=pltpu.CompilerParams(
            dimension_semantics=("arbitrary", "arbitrary", "arbitrary"), vmem_limit_bytes=VMEM_LIMIT),
        name="hgrn2",
    )(proj, proj, proj, proj, lb.reshape(1, MIX_W), gnorm.reshape(1, MIX_W), *c_ops)


def _rglru_kernel(x_ref, y_ref, cw_ref, cb_ref, w_ref, ba_ref, bx_ref, lam_ref, out_ref, xbuf_ref, h_ref):
    c = pl.program_id(2)
    T = RG_T
    G = RG_GROUP

    @pl.when(c == 0)
    def _():
        xbuf_ref[0:8, :] = jnp.zeros((8, G), F32)
        h_ref[...] = jnp.zeros_like(h_ref)

    valid = (c * T + _iota((T, 1), 0)) >= PAD
    x = jnp.where(valid, x_ref[...], 0.0)
    xbuf_ref[8:8 + T, :] = x
    xc = cb_ref[...] + cw_ref[RG_CONV - 1:RG_CONV, :] * x
    for j in range(1, RG_CONV):
        xc = xc + cw_ref[RG_CONV - 1 - j:RG_CONV - j, :] * xbuf_ref[8 - j:8 - j + T, :]
    xbuf_ref[0:8, :] = x[T - 8:T]

    pre = _dot(xc.astype(BF16), w_ref[...])
    r = _sigmoid(pre[:, :G] + ba_ref[...])
    ig = _sigmoid(pre[:, G:] + bx_ref[...])
    lam = lam_ref[...]
    softplus_neg = jnp.maximum(-lam, 0.0) + jnp.log1p(jnp.exp(-jnp.abs(lam)))
    log_a = -RG_C * r * softplus_neg
    a = jnp.exp(log_a)
    th = jnp.tanh(log_a)
    u = jnp.where(valid, jnp.sqrt(-2.0 * th / (1.0 - th)) * (ig * xc), 0.0)

    row = _iota((T, G), 0)
    shift = 1
    while shift < T:
        keep = row >= shift
        u_s = pltpu.roll(u, shift, axis=0)
        a_s = pltpu.roll(a, shift, axis=0)
        u = jnp.where(keep, a * u_s + u, u)
        a = jnp.where(keep, a * a_s, a)
        shift *= 2
    hseq = a * h_ref[...] + u
    h_ref[...] = hseq[T - 1:T]

    y = y_ref[...]
    gelu = 0.5 * y * (1.0 + jnp.tanh(0.7978845608028654 * (y + 0.044715 * (y * y * y))))
    out_ref[...] = (hseq * gelu).astype(out_ref.dtype)


def _rglru(proj, conv_w, conv_b, w_bd, ba, bx, lam, casts=()):
    b, lp, _ = proj.shape
    T = RG_T
    G = RG_GROUP
    ng = MIX_W // G
    per = G // LANES
    grid = (b, ng, lp // T)

    def col(off):
        return pl.BlockSpec((None, T, G), lambda bi, gi, ci: (bi, ci, off // per + gi))

    vec = pl.BlockSpec((1, G), lambda bi, gi, ci: (0, gi))
    in_specs = [col(_RGX), col(_RGY),
                pl.BlockSpec((RG_CONV, G), lambda bi, gi, ci: (0, gi)),
                vec,
                pl.BlockSpec((None, G, 2 * G), lambda bi, gi, ci: (gi, 0, 0)),
                vec, vec, vec]
    c_in, c_out, c_shapes, c_ops = _cast_specs(casts, grid)
    return pl.pallas_call(
        _with_casts(_rglru_kernel, len(in_specs), 1, len(casts)),
        out_shape=[jax.ShapeDtypeStruct((b, lp, MIX_W), BF16)] + c_shapes,
        grid=grid,
        in_specs=in_specs + c_in,
        out_specs=[pl.BlockSpec((None, T, G), lambda bi, gi, ci: (bi, ci, gi))] + c_out,
        scratch_shapes=[pltpu.VMEM((T + 8, G), F32), pltpu.VMEM((1, G), F32)],
        compiler_params=pltpu.CompilerParams(
            dimension_semantics=("arbitrary", "arbitrary", "arbitrary"), vmem_limit_bytes=VMEM_LIMIT),
        name="rglru",
    )(proj, proj, conv_w, conv_b.reshape(1, MIX_W), w_bd, ba.reshape(1, MIX_W), bx.reshape(1, MIX_W),
      lam.reshape(1, MIX_W), *c_ops)


def _block_diag_gates(wa, wx):
    per = RG_GROUP // RG_BD
    ng = RG_BLOCKS // per
    eye = jnp.eye(per, dtype=F32)

    def bd(w):
        w4 = w.reshape(ng, per, RG_BD, RG_BD)
        return jnp.einsum("gaij,ab->gaibj", w4, eye).reshape(ng, RG_GROUP, RG_GROUP)

    return jnp.concatenate([bd(wa), bd(wx)], axis=-1).astype(BF16)


def _assemble_kernel(x_ref, front_ref, o_ref):
    r = pl.program_id(1)

    @pl.when(r == 0)
    def _():
        o_ref[...] = front_ref[...]

    @pl.when(r > 0)
    def _():
        o_ref[...] = x_ref[...]


def _assemble(x, front, casts=()):
    b, seq, d = x.shape
    tr = PAD + N_META
    grid = (b, 1 + seq // tr)
    in_specs = [pl.BlockSpec((None, tr, d), lambda bi, r: (bi, jnp.maximum(r - 1, 0), 0)),
                pl.BlockSpec((tr, d), lambda bi, r: (0, 0))]
    c_in, c_out, c_shapes, c_ops = _cast_specs(casts, grid)
    return pl.pallas_call(
        _with_casts(_assemble_kernel, len(in_specs), 1, len(casts)),
        out_shape=[jax.ShapeDtypeStruct((b, tr + seq, d), x.dtype)] + c_shapes,
        grid=grid,
        in_specs=in_specs + c_in,
        out_specs=[pl.BlockSpec((None, tr, d), lambda bi, r: (bi, r, 0))] + c_out,
        compiler_params=pltpu.CompilerParams(
            dimension_semantics=("arbitrary", "arbitrary"), vmem_limit_bytes=VMEM_LIMIT),
        name="assemble",
    )(x, front, *c_ops)


def _layer(h, l, wt_in_bf, p, lower_bounds, tiles):
    b, lp = p["b"], p["lp"]
    m = b * lp
    tm = tiles["tm"]

    depth = p["wt_in"].shape[0]
    b_in = p["b_in"][l]
    if_lo, if_hi, gate_lo = _IN_SPLITS
    wt_ml, wt_rest, wt_gate = wt_in_bf
    wt_if = jnp.pad(p["wt_in"][l, if_lo:if_hi], ((0, LANES - 2 * ML_HEADS), (0, 0)))
    b_if = jnp.pad(b_in[if_lo:if_hi], (0, LANES - 2 * ML_HEADS)).reshape(1, LANES)

    xn = _rmsnorm(h, p["norm_mix"][l])
    proj_ml = _linear_bias(xn, wt_ml, b_in[:N_ML].reshape(1, N_ML), tm, tiles["tn_mix"], "in_proj_ml")
    proj_rest = _linear_bias(xn, wt_rest, b_in[if_hi:gate_lo].reshape(1, N_REST), tm, tiles["tn_mix"],
                             "in_proj_rest")
    proj_ml = proj_ml.reshape(b, lp, N_ML)
    proj_rest = proj_rest.reshape(b, lp, N_REST)
    gates = _linear_bias(xn, wt_if, b_if, tm, LANES, "in_proj_gates").reshape(b, lp, LANES)
    gates_t = jnp.swapaxes(gates[:, :, :8], 1, 2)

    w_up_stack = p["w_up"].reshape(depth, N_BRANCH * MIX_W, D_MODEL)
    sb_casts = [(p["w_ffn_down"], l, 128)]
    if l + 1 < depth:
        sb_casts += [(p["wt_in"], l + 1, 32, 0, N_ML),
                     (p["wt_in"], l + 1, 96, if_hi, N_REST),
                     (p["wt_in"], l + 1, 128, gate_lo, N_BRANCH * D_MODEL)]
    ml_out, w_out_bf, w_up_bf = _mlstm(proj_ml, gates, gates_t, p["ml_norm"][l],
                                       casts=[(p["w_out"], l, 64), (w_up_stack, l, 64)])
    sb_out, w_down_bf, *w_in_next = _stick_breaking(proj_rest, p["sb_q_norm"][l], p["sb_k_norm"][l],
                                                    casts=sb_casts)
    hg_out, w_fgate_bf = _hgrn2(proj_rest, lower_bounds[l], p["hg_norm"][l], casts=[(p["w_ffn_gate"], l, 16)])
    rg_out, w_fup_bf = _rglru(proj_rest, p["rg_conv_w"][l], p["rg_conv_b"][l],
                              _block_diag_gates(p["rg_wa"][l], p["rg_wx"][l]),
                              p["rg_ba"][l], p["rg_bx"][l], p["rg_lambda"][l], casts=[(p["w_ffn_up"], l, 16)])
    branches = tuple(br.reshape(m, MIX_W) for br in (ml_out, sb_out, hg_out, rg_out))
    merged = _merge(xn, branches, wt_gate, b_in[gate_lo:].reshape(1, N_BRANCH * D_MODEL),
                    w_up_bf.reshape(1, N_BRANCH, MIX_W, D_MODEL), 0, tiles["tm_merge"], tiles["tn_merge"])
    h = _linear_res(merged, w_out_bf[None], 0, h, tm, tiles["tn_out"], D_MODEL, "out_proj")

    d_ff = p["w_ffn_gate"].shape[-1]
    hn = _rmsnorm(h, p["norm_ffn"][l])
    act = _ffn_gate_up(hn, w_fgate_bf[None], w_fup_bf[None], 0, tiles["tm_ff"], tiles["tn_ff"])
    h = _linear_res(act, w_down_bf[None], 0, h, tiles["tm_down"], tiles["tn_down"], d_ff // tiles["nk_down"],
                    "ffn_down")
    return h, (tuple(w_in_next) if w_in_next else None)


def _forward(x, meta, params, tiles):
    b, seq, d = x.shape
    lp = PAD + N_META + seq
    depth = params["w_in"].shape[0]
    hg_lb = params["hg_lb"]
    p_lb = jax.nn.softmax(hg_lb.astype(F32), axis=0)
    lower_bounds = jnp.clip(jnp.cumsum(p_lb, axis=0) - p_lb[0:1], 0.0, 0.999)

    front = jnp.concatenate([jnp.zeros((PAD, d), x.dtype), meta.astype(x.dtype)], axis=0)
    p = dict(params, b=b, lp=lp, wt_in=jnp.swapaxes(params["w_in"], 1, 2))
    if_lo, if_hi, gate_lo = _IN_SPLITS
    h, *wt_in_bf = _assemble(x, front, casts=[(p["wt_in"], 0, 64, 0, N_ML),
                                              (p["wt_in"], 0, 144, if_hi, N_REST),
                                              (p["wt_in"], 0, 256, gate_lo, N_BRANCH * D_MODEL)])
    h = h.reshape(b * lp, d)
    for l in range(depth):
        h, wt_in_bf = _layer(h, l, wt_in_bf, p, lower_bounds, tiles)
    return h.reshape(b, lp, d)[:, PAD + N_META:]


_TILES = dict(tm=1056, tn_mix=1024, tm_merge=768, tn_merge=256, tn_out=512,
              tm_ff=2112, tn_ff=256, tm_down=768, tn_down=256, nk_down=1)


def kernel(x, meta, norm_mix, norm_ffn, w_in, b_in, ml_norm, sb_q_norm, sb_k_norm, hg_lb, hg_norm,
           rg_conv_w, rg_conv_b, rg_wa, rg_ba, rg_wx, rg_bx, rg_lambda, w_up, w_out,
           w_ffn_gate, w_ffn_up, w_ffn_down):
    params = dict(norm_mix=norm_mix, norm_ffn=norm_ffn, w_in=w_in, b_in=b_in, ml_norm=ml_norm,
                  sb_q_norm=sb_q_norm, sb_k_norm=sb_k_norm, hg_lb=hg_lb, hg_norm=hg_norm,
                  rg_conv_w=rg_conv_w, rg_conv_b=rg_conv_b, rg_wa=rg_wa, rg_ba=rg_ba, rg_wx=rg_wx,
                  rg_bx=rg_bx, rg_lambda=rg_lambda, w_up=w_up, w_out=w_out,
                  w_ffn_gate=w_ffn_gate, w_ffn_up=w_ffn_up, w_ffn_down=w_ffn_down)
    return _forward(x, meta, params, _TILES)
```

```python
import functools

import jax
import jax.numpy as jnp
from jax import lax
from jax.experimental import pallas as pl
from jax.experimental.pallas import tpu as pltpu

F32 = jnp.float32
BF16 = jnp.bfloat16

D_MODEL = 4096
N_META = 16
N_BRANCH = 4
MIX_W = D_MODEL // 4
ML_HEADS = 4
ML_HD = MIX_W // ML_HEADS
SB_HEADS = 8
SB_HD = MIX_W // SB_HEADS
HG_HEADS = 8
HG_HD = MIX_W // HG_HEADS
RG_BLOCKS = 16
RG_BD = MIX_W // RG_BLOCKS
RG_CONV = 4
RG_C = 8.0
EPS = 1e-6
NEG = -1e30

LANES = 128
PAD = LANES - N_META
ML_CHUNK = 128
SB_BLOCK = 128
SB_SPAN = 256
SB_HPS = 8
HG_CHUNK = 64
HG_SUB = 8
HG_HPS = 4
RG_GROUP = 512
RG_T = 128
VMEM_LIMIT = 56 * 1024 * 1024

_SBQ, _SBK, _SBV = 0, 8, 16
_HGQ, _HGF, _HGI, _HGG = 24, 32, 40, 48
_RGX, _RGY = 56, 64
N_ML = 4 * MIX_W
N_REST = 9 * MIX_W
_IN_SPLITS = (N_ML, N_ML + 2 * ML_HEADS, N_ML + 2 * ML_HEADS + N_REST)


def _sigmoid(x):
    return 1.0 / (1.0 + jnp.exp(-x))


def _log_sigmoid(x):
    return jnp.minimum(x, 0.0) - jnp.log1p(jnp.exp(-jnp.abs(x)))


def _split3(x):
    hi = x.astype(BF16)
    r1 = x - hi.astype(F32)
    mid = r1.astype(BF16)
    lo = (r1 - mid.astype(F32)).astype(BF16)
    return hi, mid, lo


def _dot(a, b):
    return jnp.dot(a, b, preferred_element_type=F32)


def _dot_nt(a, b):
    return lax.dot_general(a, b, (((1,), (1,)), ((), ())), preferred_element_type=F32)


def _dot_tn(a, b):
    return lax.dot_general(a, b, (((0,), (0,)), ((), ())), preferred_element_type=F32)


def _tri_left(x, tri):
    hi, mid, lo = _split3(x)
    return _dot(tri, hi) + _dot(tri, mid) + _dot(tri, lo)


def _tri_right(x, tri):
    hi, mid, lo = _split3(x)
    return _dot(hi, tri) + _dot(mid, tri) + _dot(lo, tri)


def _iota(shape, dim):
    return lax.broadcasted_iota(jnp.int32, shape, dim)


def _cast_specs(casts, grid):
    steps = 1
    for g in grid:
        steps *= g

    def step_of(*idx):
        s = idx[0]
        for g, i in zip(grid[1:], idx[1:]):
            s = s * g + i
        return s

    in_specs, out_specs, out_shapes, operands = [], [], [], []
    for w, layer, rows, *span in casts:
        _, r, c = w.shape
        row0, nrows = span if span else (0, r)
        nb = nrows // rows
        assert nb * rows == nrows and nb <= steps and row0 % 8 == 0, (w.shape, rows, span, steps)
        if row0 == 0:
            in_specs.append(pl.BlockSpec(
                (None, rows, c),
                lambda *idx, layer=layer, nb=nb: (layer, jnp.minimum(step_of(*idx), nb - 1), 0)))
        else:
            in_specs.append(pl.BlockSpec(
                (None, pl.Element(rows), pl.Element(c)),
                lambda *idx, layer=layer, nb=nb, row0=row0, rows=rows:
                (layer, (row0 // 8 + jnp.minimum(step_of(*idx), nb - 1) * (rows // 8)) * 8, 0)))
        out_specs.append(pl.BlockSpec(
            (rows, c), lambda *idx, nb=nb: (jnp.minimum(step_of(*idx), nb - 1), 0)))
        out_shapes.append(jax.ShapeDtypeStruct((nrows, c), BF16))
        operands.append(w)
    return in_specs, out_specs, out_shapes, operands


def _with_casts(body, n_in, n_out, n_cast):
    def kernel(*refs):
        ins = refs[:n_in]
        cast_in = refs[n_in:n_in + n_cast]
        outs = refs[n_in + n_cast:n_in + n_cast + n_out]
        cast_out = refs[n_in + n_cast + n_out:n_in + 2 * n_cast + n_out]
        scratch = refs[n_in + 2 * n_cast + n_out:]
        for src, dst in zip(cast_in, cast_out):
            dst[...] = src[...].astype(BF16)
        body(*ins, *outs, *scratch)
    return kernel


def _rmsnorm_kernel(x_ref, g_ref, o_ref):
    x = x_ref[...]
    y = x * lax.rsqrt(jnp.mean(x * x, axis=-1, keepdims=True) + EPS)
    o_ref[...] = (y * g_ref[...]).astype(o_ref.dtype)


def _rmsnorm(x, g, tm=256):
    m, d = x.shape
    return pl.pallas_call(
        _rmsnorm_kernel,
        out_shape=jax.ShapeDtypeStruct((m, d), BF16),
        grid=(m // tm,),
        in_specs=[pl.BlockSpec((tm, d), lambda i: (i, 0)),
                  pl.BlockSpec((1, d), lambda i: (0, 0))],
        out_specs=pl.BlockSpec((tm, d), lambda i: (i, 0)),
        compiler_params=pltpu.CompilerParams(dimension_semantics=("parallel",)),
        name="rmsnorm",
    )(x, g.reshape(1, d))


def _linear_bias_kernel(x_ref, wt_ref, b_ref, o_ref):
    o_ref[...] = _dot_nt(x_ref[...], wt_ref[...].astype(BF16)) + b_ref[...]


def _linear_bias(x, wt, b, tm, tn, name):
    m, k = x.shape
    n = b.shape[1]
    return pl.pallas_call(
        _linear_bias_kernel,
        out_shape=jax.ShapeDtypeStruct((m, n), F32),
        grid=(m // tm, n // tn),
        in_specs=[pl.BlockSpec((tm, k), lambda i, j: (i, 0)),
                  pl.BlockSpec((tn, k), lambda i, j: (j, 0)),
                  pl.BlockSpec((1, tn), lambda i, j: (0, j))],
        out_specs=pl.BlockSpec((tm, tn), lambda i, j: (i, j)),
        compiler_params=pltpu.CompilerParams(
            dimension_semantics=("parallel", "parallel"), vmem_limit_bytes=VMEM_LIMIT),
        name=name,
    )(x, wt, b)


def _linear_res_kernel(x_ref, w_ref, r_ref, o_ref, *, nk):
    part = _dot(x_ref[...], w_ref[...])
    if nk == 1:
        o_ref[...] = r_ref[...] + part
    else:
        k = pl.program_id(2)

        @pl.when(k == 0)
        def _():
            o_ref[...] = r_ref[...] + part

        @pl.when(k > 0)
        def _():
            o_ref[...] += part


def _linear_res(x, w, l, res, tm, tn, tk, name):
    m, k = x.shape
    n = w.shape[2]
    nk = k // tk
    return pl.pallas_call(
        functools.partial(_linear_res_kernel, nk=nk),
        out_shape=jax.ShapeDtypeStruct((m, n), F32),
        grid=(m // tm, n // tn, nk),
        in_specs=[pl.BlockSpec((tm, tk), lambda i, j, kk: (i, kk)),
                  pl.BlockSpec((None, tk, tn), lambda i, j, kk: (l, kk, j)),
                  pl.BlockSpec((tm, tn), lambda i, j, kk: (i, j))],
        out_specs=pl.BlockSpec((tm, tn), lambda i, j, kk: (i, j)),
        compiler_params=pltpu.CompilerParams(
            dimension_semantics=("parallel", "parallel", "arbitrary"), vmem_limit_bytes=VMEM_LIMIT),
        name=name,
    )(x, w, res)


def _merge_kernel(xn_ref, b0_ref, b1_ref, b2_ref, b3_ref, g0_ref, g1_ref, g2_ref, g3_ref,
                  c0_ref, c1_ref, c2_ref, c3_ref, wu_ref, o_ref):
    xn = xn_ref[...]
    acc = None
    branches = (b0_ref, b1_ref, b2_ref, b3_ref)
    gate_w = (g0_ref, g1_ref, g2_ref, g3_ref)
    gate_b = (c0_ref, c1_ref, c2_ref, c3_ref)
    for kb in range(N_BRANCH):
        g = _dot_nt(xn, gate_w[kb][...]) + gate_b[kb][...]
        u = _dot(branches[kb][...], wu_ref[kb])
        t = _sigmoid(g) * u
        acc = t if acc is None else acc + t
    o_ref[...] = acc.astype(o_ref.dtype)


def _merge(xn, branches, w_gate, b_gate, w_up, l, tm, tn):
    m, d = xn.shape
    nj = d // tn
    br_spec = pl.BlockSpec((tm, MIX_W), lambda i, j: (i, 0))

    def gw(kb):
        return pl.BlockSpec((tn, d), lambda i, j: (kb * nj + j, 0))

    def gb(kb):
        return pl.BlockSpec((1, tn), lambda i, j: (0, kb * nj + j))

    return pl.pallas_call(
        _merge_kernel,
        out_shape=jax.ShapeDtypeStruct((m, d), BF16),
        grid=(m // tm, nj),
        in_specs=[pl.BlockSpec((tm, d), lambda i, j: (i, 0)),
                  br_spec, br_spec, br_spec, br_spec,
                  gw(0), gw(1), gw(2), gw(3), gb(0), gb(1), gb(2), gb(3),
                  pl.BlockSpec((None, N_BRANCH, MIX_W, tn), lambda i, j: (l, 0, 0, j))],
        out_specs=pl.BlockSpec((tm, tn), lambda i, j: (i, j)),
        compiler_params=pltpu.CompilerParams(
            dimension_semantics=("parallel", "parallel"), vmem_limit_bytes=VMEM_LIMIT),
        name="merge",
    )(xn, *branches, w_gate, w_gate, w_gate, w_gate, b_gate, b_gate, b_gate, b_gate, w_up)


def _ffn_gu_kernel(x_ref, wg_ref, wu_ref, o_ref):
    x = x_ref[...]
    g = _dot(x, wg_ref[...])
    u = _dot(x, wu_ref[...])
    o_ref[...] = (g * _sigmoid(g) * u).astype(o_ref.dtype)


def _ffn_gate_up(x, wg, wu, l, tm, tn):
    m, k = x.shape
    n = wg.shape[2]
    return pl.pallas_call(
        _ffn_gu_kernel,
        out_shape=jax.ShapeDtypeStruct((m, n), BF16),
        grid=(m // tm, n // tn),
        in_specs=[pl.BlockSpec((tm, k), lambda i, j: (i, 0)),
                  pl.BlockSpec((None, k, tn), lambda i, j: (l, 0, j)),
                  pl.BlockSpec((None, k, tn), lambda i, j: (l, 0, j))],
        out_specs=pl.BlockSpec((tm, tn), lambda i, j: (i, j)),
        compiler_params=pltpu.CompilerParams(
            dimension_semantics=("parallel", "parallel"), vmem_limit_bytes=VMEM_LIMIT),
        name="ffn_gate_up",
    )(x, wg, wu)


def _mlstm_kernel(q_ref, k_ref, v_ref, o_ref, gc_ref, gr_ref, gn_ref, out_ref, c_ref, n_ref, m_ref):
    c = pl.program_id(1)
    T = ML_CHUNK

    @pl.when(c == 0)
    def _():
        c_ref[...] = jnp.zeros_like(c_ref)
        n_ref[...] = jnp.zeros_like(n_ref)
        m_ref[...] = jnp.zeros_like(m_ref)

    valid_col = (c * T + _iota((T, 1), 0)) >= PAD
    valid_row = (c * T + _iota((1, T), 1)) >= PAD
    gc = gc_ref[...]
    gr = gr_ref[...]
    tt = _iota((T, T), 0)
    ss = _iota((T, T), 1)
    causal = ss <= tt
    tril = jnp.where(causal, 1.0, 0.0).astype(BF16)
    triu = jnp.where(tt <= ss, 1.0, 0.0).astype(BF16)
    bc_all = _tri_left(jnp.where(valid_col, _log_sigmoid(gc), 0.0), tril)
    br_all = _tri_right(jnp.where(valid_row, _log_sigmoid(gr), 0.0), triu)

    heads = range(ML_HEADS)
    sl = [slice(h * ML_HD, (h + 1) * ML_HD) for h in heads]
    q = [q_ref[:, sl[h]] * (ML_HD ** -0.5) for h in heads]
    k = [k_ref[:, sl[h]] for h in heads]
    qb = [q[h].astype(BF16) for h in heads]
    vb = [v_ref[:, sl[h]].astype(BF16) for h in heads]
    qk = [_dot_nt(qb[h], k[h].astype(BF16)) for h in heads]
    c_prev = [c_ref[h] for h in heads]
    qc = [_dot(qb[h], c_prev[h].astype(BF16)) for h in heads]

    li_col = [jnp.where(valid_col, gc[:, h:h + 1], NEG) for h in heads]
    li_row = [jnp.where(valid_row, gr[h:h + 1, :], NEG) for h in heads]
    b_col = [bc_all[:, ML_HEADS + h:ML_HEADS + h + 1] for h in heads]
    b_row = [br_all[ML_HEADS + h:ML_HEADS + h + 1, :] for h in heads]
    m_prev = [m_ref[:, h:h + 1] for h in heads]
    dmat = [jnp.where(causal, b_col[h] - b_row[h] + li_row[h], NEG) for h in heads]
    g_col = [b_col[h] + m_prev[h] for h in heads]
    m_out = [jnp.maximum(g_col[h], jnp.max(dmat[h], axis=1, keepdims=True)) for h in heads]

    b_end = [b_row[h][:, T - 1:T] for h in heads]
    wlog = [b_end[h] - b_col[h] + li_col[h] for h in heads]
    m_new = [jnp.maximum(b_end[h] + m_prev[h], jnp.max(wlog[h], axis=0, keepdims=True)) for h in heads]
    decay = [jnp.exp(b_end[h] + m_prev[h] - m_new[h]) for h in heads]
    kw = [k[h] * jnp.exp(wlog[h] - m_new[h]) for h in heads]
    kv = [_dot(kw[h].T.astype(BF16), vb[h]) for h in heads]

    s = [qk[h] * jnp.exp(dmat[h] - m_out[h]) for h in heads]
    inter = [jnp.exp(g_col[h] - m_out[h]) for h in heads]
    sv = [_dot(s[h].astype(BF16), vb[h]) for h in heads]

    for h in heads:
        num = sv[h] + inter[h] * qc[h]
        den = (jnp.sum(s[h], axis=1, keepdims=True)
               + inter[h] * jnp.sum(q[h] * n_ref[h], axis=1, keepdims=True))
        hh = num / jnp.maximum(jnp.abs(den), jnp.exp(-m_out[h]))
        y = hh * lax.rsqrt(jnp.mean(hh * hh, axis=-1, keepdims=True) + EPS) * gn_ref[:, sl[h]]
        out_ref[:, sl[h]] = (y * _sigmoid(o_ref[:, sl[h]])).astype(out_ref.dtype)
        c_ref[h] = decay[h] * c_prev[h] + kv[h]
        n_ref[h] = decay[h] * n_ref[h] + jnp.sum(kw[h], axis=0, keepdims=True)
        m_ref[:, h:h + 1] = m_new[h]


def _mlstm(proj, gates, gates_t, gnorm, casts=()):
    b, lp, _ = proj.shape
    T = ML_CHUNK
    grid = (b, lp // T)

    def col(idx):
        return pl.BlockSpec((None, T, MIX_W), lambda bi, ci: (bi, ci, idx))

    in_specs = [col(0), col(1), col(2), col(3),
                pl.BlockSpec((None, T, LANES), lambda bi, ci: (bi, ci, 0)),
                pl.BlockSpec((None, 8, T), lambda bi, ci: (bi, 0, ci)),
                pl.BlockSpec((1, MIX_W), lambda bi, ci: (0, 0))]
    c_in, c_out, c_shapes, c_ops = _cast_specs(casts, grid)
    return pl.pallas_call(
        _with_casts(_mlstm_kernel, len(in_specs), 1, len(casts)),
        out_shape=[jax.ShapeDtypeStruct((b, lp, MIX_W), BF16)] + c_shapes,
        grid=grid,
        in_specs=in_specs + c_in,
        out_specs=[pl.BlockSpec((None, T, MIX_W), lambda bi, ci: (bi, ci, 0))] + c_out,
        scratch_shapes=[pltpu.VMEM((ML_HEADS, ML_HD, ML_HD), F32),
                        pltpu.VMEM((ML_HEADS, 1, ML_HD), F32),
                        pltpu.VMEM((1, LANES), F32)],
        compiler_params=pltpu.CompilerParams(
            dimension_semantics=("arbitrary", "arbitrary"), vmem_limit_bytes=VMEM_LIMIT),
        name="mlstm",
    )(proj, proj, proj, proj, gates, gates_t, gnorm.reshape(1, MIX_W), *c_ops)


def _sb_kernel(q_ref, k_ref, v_ref, gq_ref, gk_ref, out_ref, kn_ref, vb_ref, acc_ref, run_ref, qn_ref, tri_ref,
               z_ref, a_ref):
    qi = pl.program_id(2)
    T = SB_BLOCK
    W = SB_SPAN
    row0 = pl.multiple_of(qi * T, T)

    @pl.when(jnp.bitwise_and(qi, 1) == 0)
    def _():
        kn_ref[pl.ds(row0 + T, T), :] = jnp.zeros((T, kn_ref.shape[1]), BF16)
        vb_ref[pl.ds(row0 + T, T), :] = jnp.zeros((T, vb_ref.shape[1]), BF16)

    @pl.when(qi == 0)
    def _():
        jj = jnp.bitwise_and(_iota((2 * W, W), 0), W - 1)
        tri_ref[...] = jnp.where(jj > _iota((2 * W, W), 1), 1.0, 0.0).astype(BF16)

    heads = range(SB_HPS)
    sl = [slice(h * SB_HD, (h + 1) * SB_HD) for h in heads]
    vb_ref[pl.ds(row0, T), :] = v_ref[...].astype(BF16)
    for h in heads:
        kh = k_ref[:, sl[h]]
        kn = kh * lax.rsqrt(jnp.mean(kh * kh, axis=-1, keepdims=True) + EPS) * gk_ref[...]
        kn_ref[pl.ds(row0, T), sl[h]] = kn.astype(BF16)
        qh = q_ref[:, sl[h]]
        qs = qh * lax.rsqrt(jnp.mean(qh * qh, axis=-1, keepdims=True) + EPS) * (gq_ref[...] * SB_HD ** -0.5)
        qn_ref[:, sl[h]] = qs.astype(BF16)

    acc_ref[...] = jnp.zeros_like(acc_ref)
    run_ref[...] = jnp.zeros_like(run_ref)
    t_pos = row0 + _iota((T, W), 0)
    lane = _iota((T, W), 1)
    top = lax.shift_right_logical(qi, 1)

    def scores(col0, slot):
        for h in heads:
            z_ref[slot, h] = _dot_nt(qn_ref[:, sl[h]], kn_ref[pl.ds(col0, W), sl[h]])

    def weighted_values(col0):
        for h in heads:
            acc_ref[h] += _dot(a_ref[h], vb_ref[pl.ds(col0, W), sl[h]])

    def span_step(it, masked, has_prev):
        col0 = pl.multiple_of((top - it) * W, W)
        slot = it & 1
        z = [z_ref[slot, h] for h in heads]
        if has_prev:
            weighted_values(pl.multiple_of(col0 + W, W))
        l1p = [jnp.log(1.0 + jnp.exp(-jnp.abs(z[h]))) for h in heads]
        ls = [jnp.minimum(z[h], 0.0) - l1p[h] for h in heads]
        lk = [ls[h] - z[h] for h in heads]
        if masked:
            s_pos = col0 + lane
            vis = jnp.logical_and(s_pos < t_pos, s_pos >= PAD)
            lk = [jnp.where(vis, lk[h], 0.0) for h in heads]
        lk_hi = [lk[h].astype(BF16) for h in heads]
        lk_lo = [(lk[h] - lk_hi[h].astype(F32)).astype(BF16) for h in heads]
        cs = [_dot(jnp.concatenate([lk_hi[h], lk_lo[h]], axis=1), tri_ref[...]) for h in heads]
        scores(pl.multiple_of(jnp.maximum(top - it - 1, 0) * W, W), 1 - slot)
        a = [jnp.exp(ls[h] + cs[h] + run_ref[h]) for h in heads]
        if masked:
            a = [jnp.where(vis, a[h], 0.0) for h in heads]
        for h in heads:
            a_ref[h] = a[h].astype(BF16)
            run_ref[h] += jnp.sum(lk[h], axis=1, keepdims=True)

    scores(pl.multiple_of(top * W, W), 0)
    span_step(0, True, False)

    def body(it, carry):
        span_step(it, False, True)
        return carry

    lax.fori_loop(1, top, body, 0)

    @pl.when(top > 0)
    def _():
        span_step(top, True, True)

    weighted_values(0)
    for h in heads:
        out_ref[:, sl[h]] = acc_ref[h].astype(out_ref.dtype)


def _stick_breaking(proj, gq, gk, casts=()):
    b, lp, _ = proj.shape
    T = SB_BLOCK
    wid = SB_HPS * SB_HD
    ng = SB_HEADS // SB_HPS
    per = wid // LANES
    grid = (b, ng, lp // T)
    in_specs = [pl.BlockSpec((None, T, wid), lambda bi, gi, qi: (bi, qi, _SBQ // per + gi)),
                pl.BlockSpec((None, T, wid), lambda bi, gi, qi: (bi, qi, _SBK // per + gi)),
                pl.BlockSpec((None, T, wid), lambda bi, gi, qi: (bi, qi, _SBV // per + gi)),
                pl.BlockSpec((1, SB_HD), lambda bi, gi, qi: (0, 0)),
                pl.BlockSpec((1, SB_HD), lambda bi, gi, qi: (0, 0))]
    c_in, c_out, c_shapes, c_ops = _cast_specs(casts, grid)
    return pl.pallas_call(
        _with_casts(_sb_kernel, len(in_specs), 1, len(casts)),
        out_shape=[jax.ShapeDtypeStruct((b, lp, MIX_W), BF16)] + c_shapes,
        grid=grid,
        in_specs=in_specs + c_in,
        out_specs=[pl.BlockSpec((None, T, wid), lambda bi, gi, qi: (bi, qi, gi))] + c_out,
        scratch_shapes=[pltpu.VMEM((lp + SB_SPAN - T, wid), BF16),
                        pltpu.VMEM((lp + SB_SPAN - T, wid), BF16),
                        pltpu.VMEM((SB_HPS, T, SB_HD), F32),
                        pltpu.VMEM((SB_HPS, T, 1), F32),
                        pltpu.VMEM((T, wid), BF16),
                        pltpu.VMEM((2 * SB_SPAN, SB_SPAN), BF16),
                        pltpu.VMEM((2, SB_HPS, T, SB_SPAN), F32),
                        pltpu.VMEM((SB_HPS, T, SB_SPAN), BF16)],
        compiler_params=pltpu.CompilerParams(
            dimension_semantics=("arbitrary", "arbitrary", "arbitrary"), vmem_limit_bytes=VMEM_LIMIT),
        name="stick_breaking",
    )(proj, proj, proj, gq.reshape(1, SB_HD), gk.reshape(1, SB_HD), *c_ops)


def _hgrn2_kernel(q_ref, f_ref, i_ref, g_ref, lb_ref, gn_ref, out_ref, st_ref):
    c = pl.program_id(2)
    T = HG_CHUNK
    S8 = HG_SUB

    @pl.when(c == 0)
    def _():
        st_ref[...] = jnp.zeros_like(st_ref)

    valid = (c * T + _iota((T, 1), 0)) >= PAD
    tril = jnp.where(_iota((T, T), 1) <= _iota((T, T), 0), 1.0, 0.0).astype(BF16)
    lane8 = _iota((S8, T), 1)
    row8 = _iota((S8, T), 0)

    heads = range(HG_HPS)
    hs = [slice(h * HG_HD, (h + 1) * HG_HD) for h in heads]
    lb = [lb_ref[:, hs[h]] for h in heads]
    sig = [_sigmoid(f_ref[:, hs[h]]) for h in heads]
    logf = [jnp.where(valid, jnp.log(lb[h] + (1.0 - lb[h]) * sig[h]), 0.0) for h in heads]
    kk = [jnp.where(valid, (1.0 - lb[h]) * (1.0 - sig[h]), 0.0) for h in heads]
    q = [q_ref[:, hs[h]] * _sigmoid(q_ref[:, hs[h]]) for h in heads]
    iv = [i_ref[:, hs[h]].astype(BF16) for h in heads]
    b = [_tri_left(logf[h], tril) for h in heads]

    st = [st_ref[h] for h in heads]
    b_end = [b[h][T - 1:T] for h in heads]
    o_inter = [_dot_nt((q[h] * jnp.exp(b[h])).astype(BF16), st[h].astype(BF16)) for h in heads]
    st_add = [_dot_tn(iv[h], (kk[h] * jnp.exp(b_end[h] - b[h])).astype(BF16)) for h in heads]

    def diag_block(h, blk):
        r0 = blk * S8
        qb = q[h][r0:r0 + S8]
        bb = b[h][r0:r0 + S8]
        diag = jnp.zeros((S8, T), F32)
        for j in range(S8):
            s_idx = r0 + j
            p = qb * kk[h][s_idx:s_idx + 1] * jnp.exp(jnp.minimum(bb - b[h][s_idx:s_idx + 1], 0.0))
            diag = jnp.where(lane8 == s_idx, jnp.sum(p, axis=1, keepdims=True), diag)
        return jnp.where(lane8 - r0 <= row8, diag, 0.0)

    def off_block(h, blk):
        r0 = blk * S8
        bref = b[h][r0 - 1:r0]
        qs = (q[h][r0:r0 + S8] * jnp.exp(b[h][r0:r0 + S8] - bref)).astype(BF16)
        ks = (kk[h] * jnp.exp(jnp.minimum(bref - b[h], 0.0))).astype(BF16)
        return _dot_nt(qs, ks)

    nblk = T // S8
    off = [[off_block(h, blk) for h in heads] for blk in range(1, nblk)]
    att = []
    for h in heads:
        rows = [diag_block(h, 0)]
        for blk in range(1, nblk):
            rows.append(jnp.where(lane8 < blk * S8, off[blk - 1][h], diag_block(h, blk)))
        att.append(jnp.concatenate(rows, axis=0).astype(BF16))
    o_intra = [_dot(att[h], iv[h]) for h in heads]

    for h in heads:
        o = o_intra[h] + o_inter[h]
        st_ref[h] = st[h] * jnp.exp(b_end[h]) + st_add[h]
        y = o * lax.rsqrt(jnp.mean(o * o, axis=-1, keepdims=True) + EPS) * gn_ref[:, hs[h]]
        out_ref[:, hs[h]] = (y * _sigmoid(g_ref[:, hs[h]])).astype(out_ref.dtype)


def _hgrn2(proj, lb, gnorm, casts=()):
    b, lp, _ = proj.shape
    T = HG_CHUNK
    wid = HG_HPS * HG_HD
    ng = HG_HEADS // HG_HPS
    per = wid // LANES
    grid = (b, ng, lp // T)

    def col(off):
        return pl.BlockSpec((None, T, wid), lambda bi, gi, ci: (bi, ci, off // per + gi))

    vec = pl.BlockSpec((1, wid), lambda bi, gi, ci: (0, gi))
    in_specs = [col(_HGQ), col(_HGF), col(_HGI), col(_HGG), vec, vec]
    c_in, c_out, c_shapes, c_ops = _cast_specs(casts, grid)
    return pl.pallas_call(
        _with_casts(_hgrn2_kernel, len(in_specs), 1, len(casts)),
        out_shape=[jax.ShapeDtypeStruct((b, lp, MIX_W), BF16)] + c_shapes,
        grid=grid,
        in_specs=in_specs + c_in,
        out_specs=[pl.BlockSpec((None, T, wid), lambda bi, gi, ci: (bi, ci, gi))] + c_out,
        scratch_shapes=[pltpu.VMEM((HG_HPS, HG_HD, HG_HD), F32)],
        compiler_params=pltpu.CompilerParams(
            dimension_semantics=("arbitrary", "arbitrary", "arbitrary"), vmem_limit_bytes=VMEM_LIMIT),
        name="hgrn2",
    )(proj, proj, proj, proj, lb.reshape(1, MIX_W), gnorm.reshape(1, MIX_W), *c_ops)


def _rglru_kernel(x_ref, y_ref, cw_ref, cb_ref, w_ref, ba_ref, bx_ref, lam_ref, out_ref, xbuf_ref, h_ref):
    c = pl.program_id(2)
    T = RG_T
    G = RG_GROUP

    @pl.when(c == 0)
    def _():
        xbuf_ref[0:8, :] = jnp.zeros((8, G), F32)
        h_ref[...] = jnp.zeros_like(h_ref)

    valid = (c * T + _iota((T, 1), 0)) >= PAD
    x = jnp.where(valid, x_ref[...], 0.0)
    xbuf_ref[8:8 + T, :] = x
    xc = cb_ref[...] + cw_ref[RG_CONV - 1:RG_CONV, :] * x
    for j in range(1, RG_CONV):
        xc = xc + cw_ref[RG_CONV - 1 - j:RG_CONV - j, :] * xbuf_ref[8 - j:8 - j + T, :]
    xbuf_ref[0:8, :] = x[T - 8:T]

    pre = _dot(xc.astype(BF16), w_ref[...])
    r = _sigmoid(pre[:, :G] + ba_ref[...])
    ig = _sigmoid(pre[:, G:] + bx_ref[...])
    lam = lam_ref[...]
    softplus_neg = jnp.maximum(-lam, 0.0) + jnp.log1p(jnp.exp(-jnp.abs(lam)))
    log_a = -RG_C * r * softplus_neg
    a = jnp.exp(log_a)
    th = jnp.tanh(log_a)
    u = jnp.where(valid, jnp.sqrt(-2.0 * th / (1.0 - th)) * (ig * xc), 0.0)

    row = _iota((T, G), 0)
    shift = 1
    while shift < T:
        keep = row >= shift
        u_s = pltpu.roll(u, shift, axis=0)
        a_s = pltpu.roll(a, shift, axis=0)
        u = jnp.where(keep, a * u_s + u, u)
        a = jnp.where(keep, a * a_s, a)
        shift *= 2
    hseq = a * h_ref[...] + u
    h_ref[...] = hseq[T - 1:T]

    y = y_ref[...]
    gelu = 0.5 * y * (1.0 + jnp.tanh(0.7978845608028654 * (y + 0.044715 * (y * y * y))))
    out_ref[...] = (hseq * gelu).astype(out_ref.dtype)


def _rglru(proj, conv_w, conv_b, w_bd, ba, bx, lam, casts=()):
    b, lp, _ = proj.shape
    T = RG_T
    G = RG_GROUP
    ng = MIX_W // G
    per = G // LANES
    grid = (b, ng, lp // T)

    def col(off):
        return pl.BlockSpec((None, T, G), lambda bi, gi, ci: (bi, ci, off // per + gi))

    vec = pl.BlockSpec((1, G), lambda bi, gi, ci: (0, gi))
    in_specs = [col(_RGX), col(_RGY),
                pl.BlockSpec((RG_CONV, G), lambda bi, gi, ci: (0, gi)),
                vec,
                pl.BlockSpec((None, G, 2 * G), lambda bi, gi, ci: (gi, 0, 0)),
                vec, vec, vec]
    c_in, c_out, c_shapes, c_ops = _cast_specs(casts, grid)
    return pl.pallas_call(
        _with_casts(_rglru_kernel, len(in_specs), 1, len(casts)),
        out_shape=[jax.ShapeDtypeStruct((b, lp, MIX_W), BF16)] + c_shapes,
        grid=grid,
        in_specs=in_specs + c_in,
        out_specs=[pl.BlockSpec((None, T, G), lambda bi, gi, ci: (bi, ci, gi))] + c_out,
        scratch_shapes=[pltpu.VMEM((T + 8, G), F32), pltpu.VMEM((1, G), F32)],
        compiler_params=pltpu.CompilerParams(
            dimension_semantics=("arbitrary", "arbitrary", "arbitrary"), vmem_limit_bytes=VMEM_LIMIT),
        name="rglru",
    )(proj, proj, conv_w, conv_b.reshape(1, MIX_W), w_bd, ba.reshape(1, MIX_W), bx.reshape(1, MIX_W),
      lam.reshape(1, MIX_W), *c_ops)


def _block_diag_gates(wa, wx):
    per = RG_GROUP // RG_BD
    ng = RG_BLOCKS // per
    eye = jnp.eye(per, dtype=F32)

    def bd(w):
        w4 = w.reshape(ng, per, RG_BD, RG_BD)
        return jnp.einsum("gaij,ab->gaibj", w4, eye).reshape(ng, RG_GROUP, RG_GROUP)

    return jnp.concatenate([bd(wa), bd(wx)], axis=-1).astype(BF16)


def _assemble_kernel(x_ref, front_ref, o_ref):
    r = pl.program_id(1)

    @pl.when(r == 0)
    def _():
        o_ref[...] = front_ref[...]

    @pl.when(r > 0)
    def _():
        o_ref[...] = x_ref[...]


def _assemble(x, front, casts=()):
    b, seq, d = x.shape
    tr = PAD + N_META
    grid = (b, 1 + seq // tr)
    in_specs = [pl.BlockSpec((None, tr, d), lambda bi, r: (bi, jnp.maximum(r - 1, 0), 0)),
                pl.BlockSpec((tr, d), lambda bi, r: (0, 0))]
    c_in, c_out, c_shapes, c_ops = _cast_specs(casts, grid)
    return pl.pallas_call(
        _with_casts(_assemble_kernel, len(in_specs), 1, len(casts)),
        out_shape=[jax.ShapeDtypeStruct((b, tr + seq, d), x.dtype)] + c_shapes,
        grid=grid,
        in_specs=in_specs + c_in,
        out_specs=[pl.BlockSpec((None, tr, d), lambda bi, r: (bi, r, 0))] + c_out,
        compiler_params=pltpu.CompilerParams(
            dimension_semantics=("arbitrary", "arbitrary"), vmem_limit_bytes=VMEM_LIMIT),
        name="assemble",
    )(x, front, *c_ops)


def _layer(h, l, wt_in_bf, p, lower_bounds, tiles):
    b, lp = p["b"], p["lp"]
    m = b * lp
    tm = tiles["tm"]

    depth = p["wt_in"].shape[0]
    b_in = p["b_in"][l]
    if_lo, if_hi, gate_lo = _IN_SPLITS
    wt_ml, wt_rest, wt_gate = wt_in_bf
    wt_if = jnp.pad(p["wt_in"][l, if_lo:if_hi], ((0, LANES - 2 * ML_HEADS), (0, 0)))
    b_if = jnp.pad(b_in[if_lo:if_hi], (0, LANES - 2 * ML_HEADS)).reshape(1, LANES)

    xn = _rmsnorm(h, p["norm_mix"][l])
    proj_ml = _linear_bias(xn, wt_ml, b_in[:N_ML].reshape(1, N_ML), tm, tiles["tn_mix"], "in_proj_ml")
    proj_rest = _linear_bias(xn, wt_rest, b_in[if_hi:gate_lo].reshape(1, N_REST), tm, tiles["tn_mix"],
                             "in_proj_rest")
    proj_ml = proj_ml.reshape(b, lp, N_ML)
    proj_rest = proj_rest.reshape(b, lp, N_REST)
    gates = _linear_bias(xn, wt_if, b_if, tm, LANES, "in_proj_gates").reshape(b, lp, LANES)
    gates_t = jnp.swapaxes(gates[:, :, :8], 1, 2)

    w_up_stack = p["w_up"].reshape(depth, N_BRANCH * MIX_W, D_MODEL)
    sb_casts = []
    if l + 1 < depth:
        sb_casts = [(p["wt_in"], l + 1, 64, 0, N_ML),
                    (p["wt_in"], l + 1, 144, if_hi, N_REST),
                    (p["wt_in"], l + 1, 256, gate_lo, N_BRANCH * D_MODEL)]
    ml_out, w_out_bf, w_up_bf = _mlstm(proj_ml, gates, gates_t, p["ml_norm"][l],
                                       casts=[(p["w_out"], l, 64), (w_up_stack, l, 64)])
    sb_out, *w_in_next = _stick_breaking(proj_rest, p["sb_q_norm"][l], p["sb_k_norm"][l], casts=sb_casts)
    hg_out, w_fgate_bf, w_down_bf = _hgrn2(proj_rest, lower_bounds[l], p["hg_norm"][l],
                                           casts=[(p["w_ffn_gate"], l, 16), (p["w_ffn_down"], l, 64)])
    rg_out, w_fup_bf = _rglru(proj_rest, p["rg_conv_w"][l], p["rg_conv_b"][l],
                              _block_diag_gates(p["rg_wa"][l], p["rg_wx"][l]),
                              p["rg_ba"][l], p["rg_bx"][l], p["rg_lambda"][l], casts=[(p["w_ffn_up"], l, 32)])
    branches = tuple(br.reshape(m, MIX_W) for br in (ml_out, sb_out, hg_out, rg_out))
    merged = _merge(xn, branches, wt_gate, b_in[gate_lo:].reshape(1, N_BRANCH * D_MODEL),
                    w_up_bf.reshape(1, N_BRANCH, MIX_W, D_MODEL), 0, tiles["tm_merge"], tiles["tn_merge"])
    h = _linear_res(merged, w_out_bf[None], 0, h, tm, tiles["tn_out"], D_MODEL, "out_proj")

    d_ff = p["w_ffn_gate"].shape[-1]
    hn = _rmsnorm(h, p["norm_ffn"][l])
    act = _ffn_gate_up(hn, w_fgate_bf[None], w_fup_bf[None], 0, tiles["tm_ff"], tiles["tn_ff"])
    h = _linear_res(act, w_down_bf[None], 0, h, tiles["tm_down"], tiles["tn_down"], d_ff // tiles["nk_down"],
                    "ffn_down")
    return h, (tuple(w_in_next) if w_in_next else None)


def _forward(x, meta, params, tiles):
    b, seq, d = x.shape
    lp = PAD + N_META + seq
    depth = params["w_in"].shape[0]
    hg_lb = params["hg_lb"]
    p_lb = jax.nn.softmax(hg_lb.astype(F32), axis=0)
    lower_bounds = jnp.clip(jnp.cumsum(p_lb, axis=0) - p_lb[0:1], 0.0, 0.999)

    front = jnp.concatenate([jnp.zeros((PAD, d), x.dtype), meta.astype(x.dtype)], axis=0)
    p = dict(params, b=b, lp=lp, wt_in=jnp.swapaxes(params["w_in"], 1, 2))
    if_lo, if_hi, gate_lo = _IN_SPLITS
    h, *wt_in_bf = _assemble(x, front, casts=[(p["wt_in"], 0, 64, 0, N_ML),
                                              (p["wt_in"], 0, 144, if_hi, N_REST),
                                              (p["wt_in"], 0, 256, gate_lo, N_BRANCH * D_MODEL)])
    h = h.reshape(b * lp, d)
    for l in range(depth):
        h, wt_in_bf = _layer(h, l, wt_in_bf, p, lower_bounds, tiles)
    return h.reshape(b, lp, d)[:, PAD + N_META:]


_TILES = dict(tm=1056, tn_mix=1024, tm_merge=768, tn_merge=256, tn_out=512,
              tm_ff=2112, tn_ff=256, tm_down=768, tn_down=256, nk_down=1)


def kernel(x, meta, norm_mix, norm_ffn, w_in, b_in, ml_norm, sb_q_norm, sb_k_norm, hg_lb, hg_norm,
           rg_conv_w, rg_conv_b, rg_wa, rg_ba, rg_wx, rg_bx, rg_lambda, w_up, w_out,
           w_ffn_gate, w_ffn_up, w_ffn_down):
    params = dict(norm_mix=norm_mix, norm_ffn=norm_ffn, w_in=w_in, b_in=b_in, ml_norm=ml_norm,
                  sb_q_norm=sb_q_norm, sb_k_norm=sb_k_norm, hg_lb=hg_lb, hg_norm=hg_norm,
                  rg_conv_w=rg_conv_w, rg_conv_b=rg_conv_b, rg_wa=rg_wa, rg_ba=rg_ba, rg_wx=rg_wx,
                  rg_bx=rg_bx, rg_lambda=rg_lambda, w_up=w_up, w_out=w_out,
                  w_ffn_gate=w_ffn_gate, w_ffn_up=w_ffn_up, w_ffn_down=w_ffn_down)
    return _forward(x, meta, params, _TILES)
```

```python
import functools

import jax
import jax.numpy as jnp
from jax import lax
from jax.experimental import pallas as pl
from jax.experimental.pallas import tpu as pltpu

F32 = jnp.float32
BF16 = jnp.bfloat16

D_MODEL = 4096
N_META = 16
N_BRANCH = 4
MIX_W = D_MODEL // 4
ML_HEADS = 4
ML_HD = MIX_W // ML_HEADS
SB_HEADS = 8
SB_HD = MIX_W // SB_HEADS
HG_HEADS = 8
HG_HD = MIX_W // HG_HEADS
RG_BLOCKS = 16
RG_BD = MIX_W // RG_BLOCKS
RG_CONV = 4
RG_C = 8.0
EPS = 1e-6
NEG = -1e30

LANES = 128
PAD = LANES - N_META
ML_CHUNK = 128
SB_BLOCK = 128
SB_SPAN = 256
SB_HPS = 8
HG_CHUNK = 64
HG_SUB = 8
HG_HPS = 8
RG_GROUP = 512
RG_T = 128
VMEM_LIMIT = 56 * 1024 * 1024

_SBQ, _SBK, _SBV = 0, 8, 16
_HGQ, _HGF, _HGI, _HGG = 24, 32, 40, 48
_RGX, _RGY = 56, 64
N_ML = 4 * MIX_W
N_REST = 9 * MIX_W
_IN_SPLITS = (N_ML, N_ML + 2 * ML_HEADS, N_ML + 2 * ML_HEADS + N_REST)


def _sigmoid(x):
    return 1.0 / (1.0 + jnp.exp(-x))


def _log_sigmoid(x):
    return jnp.minimum(x, 0.0) - jnp.log1p(jnp.exp(-jnp.abs(x)))


def _split3(x):
    hi = x.astype(BF16)
    r1 = x - hi.astype(F32)
    mid = r1.astype(BF16)
    lo = (r1 - mid.astype(F32)).astype(BF16)
    return hi, mid, lo


def _dot(a, b):
    return jnp.dot(a, b, preferred_element_type=F32)


def _dot_nt(a, b):
    return lax.dot_general(a, b, (((1,), (1,)), ((), ())), preferred_element_type=F32)


def _dot_tn(a, b):
    return lax.dot_general(a, b, (((0,), (0,)), ((), ())), preferred_element_type=F32)


def _tri_left(x, tri):
    hi, mid, lo = _split3(x)
    return _dot(tri, hi) + _dot(tri, mid) + _dot(tri, lo)


def _tri_right(x, tri):
    hi, mid, lo = _split3(x)
    return _dot(hi, tri) + _dot(mid, tri) + _dot(lo, tri)


def _iota(shape, dim):
    return lax.broadcasted_iota(jnp.int32, shape, dim)


def _cast_specs(casts, grid):
    steps = 1
    for g in grid:
        steps *= g

    def step_of(*idx):
        s = idx[0]
        for g, i in zip(grid[1:], idx[1:]):
            s = s * g + i
        return s

    in_specs, out_specs, out_shapes, operands = [], [], [], []
    for w, layer, rows, *span in casts:
        _, r, c = w.shape
        row0, nrows = span if span else (0, r)
        nb = nrows // rows
        assert nb * rows == nrows and nb <= steps and row0 % 8 == 0, (w.shape, rows, span, steps)
        if row0 == 0:
            in_specs.append(pl.BlockSpec(
                (None, rows, c),
                lambda *idx, layer=layer, nb=nb: (layer, jnp.minimum(step_of(*idx), nb - 1), 0)))
        else:
            in_specs.append(pl.BlockSpec(
                (None, pl.Element(rows), pl.Element(c)),
                lambda *idx, layer=layer, nb=nb, row0=row0, rows=rows:
                (layer, (row0 // 8 + jnp.minimum(step_of(*idx), nb - 1) * (rows // 8)) * 8, 0)))
        out_specs.append(pl.BlockSpec(
            (rows, c), lambda *idx, nb=nb: (jnp.minimum(step_of(*idx), nb - 1), 0)))
        out_shapes.append(jax.ShapeDtypeStruct((nrows, c), BF16))
        operands.append(w)
    return in_specs, out_specs, out_shapes, operands


def _with_casts(body, n_in, n_out, n_cast):
    def kernel(*refs):
        ins = refs[:n_in]
        cast_in = refs[n_in:n_in + n_cast]
        outs = refs[n_in + n_cast:n_in + n_cast + n_out]
        cast_out = refs[n_in + n_cast + n_out:n_in + 2 * n_cast + n_out]
        scratch = refs[n_in + 2 * n_cast + n_out:]
        for src, dst in zip(cast_in, cast_out):
            dst[...] = src[...].astype(BF16)
        body(*ins, *outs, *scratch)
    return kernel


def _rmsnorm_kernel(x_ref, g_ref, o_ref):
    x = x_ref[...]
    y = x * lax.rsqrt(jnp.mean(x * x, axis=-1, keepdims=True) + EPS)
    o_ref[...] = (y * g_ref[...]).astype(o_ref.dtype)


def _rmsnorm(x, g, tm=256):
    m, d = x.shape
    return pl.pallas_call(
        _rmsnorm_kernel,
        out_shape=jax.ShapeDtypeStruct((m, d), BF16),
        grid=(m // tm,),
        in_specs=[pl.BlockSpec((tm, d), lambda i: (i, 0)),
                  pl.BlockSpec((1, d), lambda i: (0, 0))],
        out_specs=pl.BlockSpec((tm, d), lambda i: (i, 0)),
        compiler_params=pltpu.CompilerParams(dimension_semantics=("parallel",)),
        name="rmsnorm",
    )(x, g.reshape(1, d))


def _linear_bias_kernel(x_ref, wt_ref, b_ref, o_ref):
    o_ref[...] = _dot_nt(x_ref[...], wt_ref[...].astype(BF16)) + b_ref[...]


def _linear_bias(x, wt, b, tm, tn, name):
    m, k = x.shape
    n = b.shape[1]
    return pl.pallas_call(
        _linear_bias_kernel,
        out_shape=jax.ShapeDtypeStruct((m, n), F32),
        grid=(m // tm, n // tn),
        in_specs=[pl.BlockSpec((tm, k), lambda i, j: (i, 0)),
                  pl.BlockSpec((tn, k), lambda i, j: (j, 0)),
                  pl.BlockSpec((1, tn), lambda i, j: (0, j))],
        out_specs=pl.BlockSpec((tm, tn), lambda i, j: (i, j)),
        compiler_params=pltpu.CompilerParams(
            dimension_semantics=("parallel", "parallel"), vmem_limit_bytes=VMEM_LIMIT),
        name=name,
    )(x, wt, b)


def _linear_res_kernel(x_ref, w_ref, r_ref, o_ref, *, nk):
    part = _dot(x_ref[...], w_ref[...])
    if nk == 1:
        o_ref[...] = r_ref[...] + part
    else:
        k = pl.program_id(2)

        @pl.when(k == 0)
        def _():
            o_ref[...] = r_ref[...] + part

        @pl.when(k > 0)
        def _():
            o_ref[...] += part


def _linear_res(x, w, l, res, tm, tn, tk, name):
    m, k = x.shape
    n = w.shape[2]
    nk = k // tk
    return pl.pallas_call(
        functools.partial(_linear_res_kernel, nk=nk),
        out_shape=jax.ShapeDtypeStruct((m, n), F32),
        grid=(m // tm, n // tn, nk),
        in_specs=[pl.BlockSpec((tm, tk), lambda i, j, kk: (i, kk)),
                  pl.BlockSpec((None, tk, tn), lambda i, j, kk: (l, kk, j)),
                  pl.BlockSpec((tm, tn), lambda i, j, kk: (i, j))],
        out_specs=pl.BlockSpec((tm, tn), lambda i, j, kk: (i, j)),
        compiler_params=pltpu.CompilerParams(
            dimension_semantics=("parallel", "parallel", "arbitrary"), vmem_limit_bytes=VMEM_LIMIT),
        name=name,
    )(x, w, res)


def _merge_kernel(xn_ref, b0_ref, b1_ref, b2_ref, b3_ref, g0_ref, g1_ref, g2_ref, g3_ref,
                  c0_ref, c1_ref, c2_ref, c3_ref, wu_ref, o_ref):
    xn = xn_ref[...]
    acc = None
    branches = (b0_ref, b1_ref, b2_ref, b3_ref)
    gate_w = (g0_ref, g1_ref, g2_ref, g3_ref)
    gate_b = (c0_ref, c1_ref, c2_ref, c3_ref)
    for kb in range(N_BRANCH):
        g = _dot_nt(xn, gate_w[kb][...]) + gate_b[kb][...]
        u = _dot(branches[kb][...], wu_ref[kb])
        t = _sigmoid(g) * u
        acc = t if acc is None else acc + t
    o_ref[...] = acc.astype(o_ref.dtype)


def _merge(xn, branches, w_gate, b_gate, w_up, l, tm, tn):
    m, d = xn.shape
    nj = d // tn
    br_spec = pl.BlockSpec((tm, MIX_W), lambda i, j: (i, 0))

    def gw(kb):
        return pl.BlockSpec((tn, d), lambda i, j: (kb * nj + j, 0))

    def gb(kb):
        return pl.BlockSpec((1, tn), lambda i, j: (0, kb * nj + j))

    return pl.pallas_call(
        _merge_kernel,
        out_shape=jax.ShapeDtypeStruct((m, d), BF16),
        grid=(m // tm, nj),
        in_specs=[pl.BlockSpec((tm, d), lambda i, j: (i, 0)),
                  br_spec, br_spec, br_spec, br_spec,
                  gw(0), gw(1), gw(2), gw(3), gb(0), gb(1), gb(2), gb(3),
                  pl.BlockSpec((None, N_BRANCH, MIX_W, tn), lambda i, j: (l, 0, 0, j))],
        out_specs=pl.BlockSpec((tm, tn), lambda i, j: (i, j)),
        compiler_params=pltpu.CompilerParams(
            dimension_semantics=("parallel", "parallel"), vmem_limit_bytes=VMEM_LIMIT),
        name="merge",
    )(xn, *branches, w_gate, w_gate, w_gate, w_gate, b_gate, b_gate, b_gate, b_gate, w_up)


def _ffn_gu_kernel(x_ref, wg_ref, wu_ref, o_ref):
    x = x_ref[...]
    g = _dot(x, wg_ref[...])
    u = _dot(x, wu_ref[...])
    o_ref[...] = (g * _sigmoid(g) * u).astype(o_ref.dtype)


def _ffn_gate_up(x, wg, wu, l, tm, tn):
    m, k = x.shape
    n = wg.shape[2]
    return pl.pallas_call(
        _ffn_gu_kernel,
        out_shape=jax.ShapeDtypeStruct((m, n), BF16),
        grid=(m // tm, n // tn),
        in_specs=[pl.BlockSpec((tm, k), lambda i, j: (i, 0)),
                  pl.BlockSpec((None, k, tn), lambda i, j: (l, 0, j)),
                  pl.BlockSpec((None, k, tn), lambda i, j: (l, 0, j))],
        out_specs=pl.BlockSpec((tm, tn), lambda i, j: (i, j)),
        compiler_params=pltpu.CompilerParams(
            dimension_semantics=("parallel", "parallel"), vmem_limit_bytes=VMEM_LIMIT),
        name="ffn_gate_up",
    )(x, wg, wu)


def _mlstm_kernel(q_ref, k_ref, v_ref, o_ref, gc_ref, gr_ref, gn_ref, out_ref, c_ref, n_ref, m_ref):
    c = pl.program_id(1)
    T = ML_CHUNK

    @pl.when(c == 0)
    def _():
        c_ref[...] = jnp.zeros_like(c_ref)
        n_ref[...] = jnp.zeros_like(n_ref)
        m_ref[...] = jnp.zeros_like(m_ref)

    valid_col = (c * T + _iota((T, 1), 0)) >= PAD
    valid_row = (c * T + _iota((1, T), 1)) >= PAD
    gc = gc_ref[...]
    gr = gr_ref[...]
    tt = _iota((T, T), 0)
    ss = _iota((T, T), 1)
    causal = ss <= tt
    tril = jnp.where(causal, 1.0, 0.0).astype(BF16)
    triu = jnp.where(tt <= ss, 1.0, 0.0).astype(BF16)
    bc_all = _tri_left(jnp.where(valid_col, _log_sigmoid(gc), 0.0), tril)
    br_all = _tri_right(jnp.where(valid_row, _log_sigmoid(gr), 0.0), triu)

    heads = range(ML_HEADS)
    sl = [slice(h * ML_HD, (h + 1) * ML_HD) for h in heads]
    q = [q_ref[:, sl[h]] * (ML_HD ** -0.5) for h in heads]
    k = [k_ref[:, sl[h]] for h in heads]
    qb = [q[h].astype(BF16) for h in heads]
    vb = [v_ref[:, sl[h]].astype(BF16) for h in heads]
    qk = [_dot_nt(qb[h], k[h].astype(BF16)) for h in heads]
    c_prev = [c_ref[h] for h in heads]
    qc = [_dot(qb[h], c_prev[h].astype(BF16)) for h in heads]

    li_col = [jnp.where(valid_col, gc[:, h:h + 1], NEG) for h in heads]
    li_row = [jnp.where(valid_row, gr[h:h + 1, :], NEG) for h in heads]
    b_col = [bc_all[:, ML_HEADS + h:ML_HEADS + h + 1] for h in heads]
    b_row = [br_all[ML_HEADS + h:ML_HEADS + h + 1, :] for h in heads]
    m_prev = [m_ref[:, h:h + 1] for h in heads]
    dmat = [jnp.where(causal, b_col[h] - b_row[h] + li_row[h], NEG) for h in heads]
    g_col = [b_col[h] + m_prev[h] for h in heads]
    m_out = [jnp.maximum(g_col[h], jnp.max(dmat[h], axis=1, keepdims=True)) for h in heads]

    b_end = [b_row[h][:, T - 1:T] for h in heads]
    wlog = [b_end[h] - b_col[h] + li_col[h] for h in heads]
    m_new = [jnp.maximum(b_end[h] + m_prev[h], jnp.max(wlog[h], axis=0, keepdims=True)) for h in heads]
    decay = [jnp.exp(b_end[h] + m_prev[h] - m_new[h]) for h in heads]
    kw = [k[h] * jnp.exp(wlog[h] - m_new[h]) for h in heads]
    kv = [_dot(kw[h].T.astype(BF16), vb[h]) for h in heads]

    s = [qk[h] * jnp.exp(dmat[h] - m_out[h]) for h in heads]
    inter = [jnp.exp(g_col[h] - m_out[h]) for h in heads]
    sv = [_dot(s[h].astype(BF16), vb[h]) for h in heads]

    for h in heads:
        num = sv[h] + inter[h] * qc[h]
        den = (jnp.sum(s[h], axis=1, keepdims=True)
               + inter[h] * jnp.sum(q[h] * n_ref[h], axis=1, keepdims=True))
        hh = num / jnp.maximum(jnp.abs(den), jnp.exp(-m_out[h]))
        y = hh * lax.rsqrt(jnp.mean(hh * hh, axis=-1, keepdims=True) + EPS) * gn_ref[:, sl[h]]
        out_ref[:, sl[h]] = (y * _sigmoid(o_ref[:, sl[h]])).astype(out_ref.dtype)
        c_ref[h] = decay[h] * c_prev[h] + kv[h]
        n_ref[h] = decay[h] * n_ref[h] + jnp.sum(kw[h], axis=0, keepdims=True)
        m_ref[:, h:h + 1] = m_new[h]


def _mlstm(proj, gates, gates_t, gnorm, casts=()):
    b, lp, _ = proj.shape
    T = ML_CHUNK
    grid = (b, lp // T)

    def col(idx):
        return pl.BlockSpec((None, T, MIX_W), lambda bi, ci: (bi, ci, idx))

    in_specs = [col(0), col(1), col(2), col(3),
                pl.BlockSpec((None, T, LANES), lambda bi, ci: (bi, ci, 0)),
                pl.BlockSpec((None, 8, T), lambda bi, ci: (bi, 0, ci)),
                pl.BlockSpec((1, MIX_W), lambda bi, ci: (0, 0))]
    c_in, c_out, c_shapes, c_ops = _cast_specs(casts, grid)
    return pl.pallas_call(
        _with_casts(_mlstm_kernel, len(in_specs), 1, len(casts)),
        out_shape=[jax.ShapeDtypeStruct((b, lp, MIX_W), BF16)] + c_shapes,
        grid=grid,
        in_specs=in_specs + c_in,
        out_specs=[pl.BlockSpec((None, T, MIX_W), lambda bi, ci: (bi, ci, 0))] + c_out,
        scratch_shapes=[pltpu.VMEM((ML_HEADS, ML_HD, ML_HD), F32),
                        pltpu.VMEM((ML_HEADS, 1, ML_HD), F32),
                        pltpu.VMEM((1, LANES), F32)],
        compiler_params=pltpu.CompilerParams(
            dimension_semantics=("arbitrary", "arbitrary"), vmem_limit_bytes=VMEM_LIMIT),
        name="mlstm",
    )(proj, proj, proj, proj, gates, gates_t, gnorm.reshape(1, MIX_W), *c_ops)


def _sb_kernel(q_ref, k_ref, v_ref, gq_ref, gk_ref, out_ref, kn_ref, vb_ref, acc_ref, run_ref, qn_ref, tri_ref,
               z_ref, a_ref):
    qi = pl.program_id(2)
    T = SB_BLOCK
    W = SB_SPAN
    row0 = pl.multiple_of(qi * T, T)

    @pl.when(jnp.bitwise_and(qi, 1) == 0)
    def _():
        kn_ref[pl.ds(row0 + T, T), :] = jnp.zeros((T, kn_ref.shape[1]), BF16)
        vb_ref[pl.ds(row0 + T, T), :] = jnp.zeros((T, vb_ref.shape[1]), BF16)

    @pl.when(qi == 0)
    def _():
        jj = jnp.bitwise_and(_iota((2 * W, W), 0), W - 1)
        tri_ref[...] = jnp.where(jj > _iota((2 * W, W), 1), 1.0, 0.0).astype(BF16)

    heads = range(SB_HPS)
    sl = [slice(h * SB_HD, (h + 1) * SB_HD) for h in heads]
    vb_ref[pl.ds(row0, T), :] = v_ref[...].astype(BF16)
    for h in heads:
        kh = k_ref[:, sl[h]]
        kn = kh * lax.rsqrt(jnp.mean(kh * kh, axis=-1, keepdims=True) + EPS) * gk_ref[...]
        kn_ref[pl.ds(row0, T), sl[h]] = kn.astype(BF16)
        qh = q_ref[:, sl[h]]
        qs = qh * lax.rsqrt(jnp.mean(qh * qh, axis=-1, keepdims=True) + EPS) * (gq_ref[...] * SB_HD ** -0.5)
        qn_ref[:, sl[h]] = qs.astype(BF16)

    acc_ref[...] = jnp.zeros_like(acc_ref)
    run_ref[...] = jnp.zeros_like(run_ref)
    t_pos = row0 + _iota((T, W), 0)
    lane = _iota((T, W), 1)
    top = lax.shift_right_logical(qi, 1)

    def scores(col0, slot):
        for h in heads:
            z_ref[slot, h] = _dot_nt(qn_ref[:, sl[h]], kn_ref[pl.ds(col0, W), sl[h]])

    def weighted_values(col0):
        for h in heads:
            acc_ref[h] += _dot(a_ref[h], vb_ref[pl.ds(col0, W), sl[h]])

    def span_step(it, masked, has_prev):
        col0 = pl.multiple_of((top - it) * W, W)
        slot = it & 1
        z = [z_ref[slot, h] for h in heads]
        if has_prev:
            weighted_values(pl.multiple_of(col0 + W, W))
        l1p = [jnp.log(1.0 + jnp.exp(-jnp.abs(z[h]))) for h in heads]
        ls = [jnp.minimum(z[h], 0.0) - l1p[h] for h in heads]
        lk = [ls[h] - z[h] for h in heads]
        if masked:
            s_pos = col0 + lane
            vis = jnp.logical_and(s_pos < t_pos, s_pos >= PAD)
            lk = [jnp.where(vis, lk[h], 0.0) for h in heads]
        lk_hi = [lk[h].astype(BF16) for h in heads]
        lk_lo = [(lk[h] - lk_hi[h].astype(F32)).astype(BF16) for h in heads]
        cs = [_dot(jnp.concatenate([lk_hi[h], lk_lo[h]], axis=1), tri_ref[...]) for h in heads]
        scores(pl.multiple_of(jnp.maximum(top - it - 1, 0) * W, W), 1 - slot)
        a = [jnp.exp(ls[h] + cs[h] + run_ref[h]) for h in heads]
        if masked:
            a = [jnp.where(vis, a[h], 0.0) for h in heads]
        for h in heads:
            a_ref[h] = a[h].astype(BF16)
            run_ref[h] += jnp.sum(lk[h], axis=1, keepdims=True)

    scores(pl.multiple_of(top * W, W), 0)
    span_step(0, True, False)

    def body(it, carry):
        span_step(it, False, True)
        return carry

    lax.fori_loop(1, top, body, 0)

    @pl.when(top > 0)
    def _():
        span_step(top, True, True)

    weighted_values(0)
    for h in heads:
        out_ref[:, sl[h]] = acc_ref[h].astype(out_ref.dtype)


def _stick_breaking(proj, gq, gk, casts=()):
    b, lp, _ = proj.shape
    T = SB_BLOCK
    wid = SB_HPS * SB_HD
    ng = SB_HEADS // SB_HPS
    per = wid // LANES
    grid = (b, ng, lp // T)
    in_specs = [pl.BlockSpec((None, T, wid), lambda bi, gi, qi: (bi, qi, _SBQ // per + gi)),
                pl.BlockSpec((None, T, wid), lambda bi, gi, qi: (bi, qi, _SBK // per + gi)),
                pl.BlockSpec((None, T, wid), lambda bi, gi, qi: (bi, qi, _SBV // per + gi)),
                pl.BlockSpec((1, SB_HD), lambda bi, gi, qi: (0, 0)),
                pl.BlockSpec((1, SB_HD), lambda bi, gi, qi: (0, 0))]
    c_in, c_out, c_shapes, c_ops = _cast_specs(casts, grid)
    return pl.pallas_call(
        _with_casts(_sb_kernel, len(in_specs), 1, len(casts)),
        out_shape=[jax.ShapeDtypeStruct((b, lp, MIX_W), BF16)] + c_shapes,
        grid=grid,
        in_specs=in_specs + c_in,
        out_specs=[pl.BlockSpec((None, T, wid), lambda bi, gi, qi: (bi, qi, gi))] + c_out,
        scratch_shapes=[pltpu.VMEM((lp + SB_SPAN - T, wid), BF16),
                        pltpu.VMEM((lp + SB_SPAN - T, wid), BF16),
                        pltpu.VMEM((SB_HPS, T, SB_HD), F32),
                        pltpu.VMEM((SB_HPS, T, 1), F32),
                        pltpu.VMEM((T, wid), BF16),
                        pltpu.VMEM((2 * SB_SPAN, SB_SPAN), BF16),
                        pltpu.VMEM((2, SB_HPS, T, SB_SPAN), F32),
                        pltpu.VMEM((SB_HPS, T, SB_SPAN), BF16)],
        compiler_params=pltpu.CompilerParams(
            dimension_semantics=("arbitrary", "arbitrary", "arbitrary"), vmem_limit_bytes=VMEM_LIMIT),
        name="stick_breaking",
    )(proj, proj, proj, gq.reshape(1, SB_HD), gk.reshape(1, SB_HD), *c_ops)


def _hgrn2_kernel(q_ref, f_ref, i_ref, g_ref, lb_ref, gn_ref, out_ref, st_ref):
    c = pl.program_id(2)
    T = HG_CHUNK
    S8 = HG_SUB

    @pl.when(c == 0)
    def _():
        st_ref[...] = jnp.zeros_like(st_ref)

    valid = (c * T + _iota((T, 1), 0)) >= PAD
    tril = jnp.where(_iota((T, T), 1) <= _iota((T, T), 0), 1.0, 0.0).astype(BF16)
    lane8 = _iota((S8, T), 1)
    row8 = _iota((S8, T), 0)

    heads = range(HG_HPS)
    hs = [slice(h * HG_HD, (h + 1) * HG_HD) for h in heads]
    lb = [lb_ref[:, hs[h]] for h in heads]
    sig = [_sigmoid(f_ref[:, hs[h]]) for h in heads]
    logf = [jnp.where(valid, jnp.log(lb[h] + (1.0 - lb[h]) * sig[h]), 0.0) for h in heads]
    kk = [jnp.where(valid, (1.0 - lb[h]) * (1.0 - sig[h]), 0.0) for h in heads]
    q = [q_ref[:, hs[h]] * _sigmoid(q_ref[:, hs[h]]) for h in heads]
    iv = [i_ref[:, hs[h]].astype(BF16) for h in heads]
    b = [_tri_left(logf[h], tril) for h in heads]

    st = [st_ref[h] for h in heads]
    b_end = [b[h][T - 1:T] for h in heads]
    o_inter = [_dot_nt((q[h] * jnp.exp(b[h])).astype(BF16), st[h].astype(BF16)) for h in heads]
    st_add = [_dot_tn(iv[h], (kk[h] * jnp.exp(b_end[h] - b[h])).astype(BF16)) for h in heads]

    def diag_block(h, blk):
        r0 = blk * S8
        qb = q[h][r0:r0 + S8]
        bb = b[h][r0:r0 + S8]
        diag = jnp.zeros((S8, T), F32)
        for j in range(S8):
            s_idx = r0 + j
            p = qb * kk[h][s_idx:s_idx + 1] * jnp.exp(jnp.minimum(bb - b[h][s_idx:s_idx + 1], 0.0))
            diag = jnp.where(lane8 == s_idx, jnp.sum(p, axis=1, keepdims=True), diag)
        return jnp.where(lane8 - r0 <= row8, diag, 0.0)

    def off_block(h, blk):
        r0 = blk * S8
        bref = b[h][r0 - 1:r0]
        qs = (q[h][r0:r0 + S8] * jnp.exp(b[h][r0:r0 + S8] - bref)).astype(BF16)
        ks = (kk[h] * jnp.exp(jnp.minimum(bref - b[h], 0.0))).astype(BF16)
        return _dot_nt(qs, ks)

    nblk = T // S8
    off = [[off_block(h, blk) for h in heads] for blk in range(1, nblk)]
    att = []
    for h in heads:
        rows = [diag_block(h, 0)]
        for blk in range(1, nblk):
            rows.append(jnp.where(lane8 < blk * S8, off[blk - 1][h], diag_block(h, blk)))
        att.append(jnp.concatenate(rows, axis=0).astype(BF16))
    o_intra = [_dot(att[h], iv[h]) for h in heads]

    for h in heads:
        o = o_intra[h] + o_inter[h]
        st_ref[h] = st[h] * jnp.exp(b_end[h]) + st_add[h]
        y = o * lax.rsqrt(jnp.mean(o * o, axis=-1, keepdims=True) + EPS) * gn_ref[:, hs[h]]
        out_ref[:, hs[h]] = (y * _sigmoid(g_ref[:, hs[h]])).astype(out_ref.dtype)


def _hgrn2(proj, lb, gnorm, casts=()):
    b, lp, _ = proj.shape
    T = HG_CHUNK
    wid = HG_HPS * HG_HD
    ng = HG_HEADS // HG_HPS
    per = wid // LANES
    grid = (b, ng, lp // T)

    def col(off):
        return pl.BlockSpec((None, T, wid), lambda bi, gi, ci: (bi, ci, off // per + gi))

    vec = pl.BlockSpec((1, wid), lambda bi, gi, ci: (0, gi))
    in_specs = [col(_HGQ), col(_HGF), col(_HGI), col(_HGG), vec, vec]
    c_in, c_out, c_shapes, c_ops = _cast_specs(casts, grid)
    return pl.pallas_call(
        _with_casts(_hgrn2_kernel, len(in_specs), 1, len(casts)),
        out_shape=[jax.ShapeDtypeStruct((b, lp, MIX_W), BF16)] + c_shapes,
        grid=grid,
        in_specs=in_specs + c_in,
        out_specs=[pl.BlockSpec((None, T, wid), lambda bi, gi, ci: (bi, ci, gi))] + c_out,
        scratch_shapes=[pltpu.VMEM((HG_HPS, HG_HD, HG_HD), F32)],
        compiler_params=pltpu.CompilerParams(
            dimension_semantics=("arbitrary", "arbitrary", "arbitrary"), vmem_limit_bytes=VMEM_LIMIT),
        name="hgrn2",
    )(proj, proj, proj, proj, lb.reshape(1, MIX_W), gnorm.reshape(1, MIX_W), *c_ops)


def _rglru_kernel(x_ref, y_ref, cw_ref, cb_ref, w_ref, ba_ref, bx_ref, lam_ref, out_ref, xbuf_ref, h_ref):
    c = pl.program_id(2)
    T = RG_T
    G = RG_GROUP

    @pl.when(c == 0)
    def _():
        xbuf_ref[0:8, :] = jnp.zeros((8, G), F32)
        h_ref[...] = jnp.zeros_like(h_ref)

    valid = (c * T + _iota((T, 1), 0)) >= PAD
    x = jnp.where(valid, x_ref[...], 0.0)
    xbuf_ref[8:8 + T, :] = x
    xc = cb_ref[...] + cw_ref[RG_CONV - 1:RG_CONV, :] * x
    for j in range(1, RG_CONV):
        xc = xc + cw_ref[RG_CONV - 1 - j:RG_CONV - j, :] * xbuf_ref[8 - j:8 - j + T, :]
    xbuf_ref[0:8, :] = x[T - 8:T]

    pre = _dot(xc.astype(BF16), w_ref[...])
    r = _sigmoid(pre[:, :G] + ba_ref[...])
    ig = _sigmoid(pre[:, G:] + bx_ref[...])
    lam = lam_ref[...]
    softplus_neg = jnp.maximum(-lam, 0.0) + jnp.log1p(jnp.exp(-jnp.abs(lam)))
    log_a = -RG_C * r * softplus_neg
    a = jnp.exp(log_a)
    th = jnp.tanh(log_a)
    u = jnp.where(valid, jnp.sqrt(-2.0 * th / (1.0 - th)) * (ig * xc), 0.0)

    row = _iota((T, G), 0)
    shift = 1
    while shift < T:
        keep = row >= shift
        u_s = pltpu.roll(u, shift, axis=0)
        a_s = pltpu.roll(a, shift, axis=0)
        u = jnp.where(keep, a * u_s + u, u)
        a = jnp.where(keep, a * a_s, a)
        shift *= 2
    hseq = a * h_ref[...] + u
    h_ref[...] = hseq[T - 1:T]

    y = y_ref[...]
    gelu = 0.5 * y * (1.0 + jnp.tanh(0.7978845608028654 * (y + 0.044715 * (y * y * y))))
    out_ref[...] = (hseq * gelu).astype(out_ref.dtype)


def _rglru(proj, conv_w, conv_b, w_bd, ba, bx, lam, casts=()):
    b, lp, _ = proj.shape
    T = RG_T
    G = RG_GROUP
    ng = MIX_W // G
    per = G // LANES
    grid = (b, ng, lp // T)

    def col(off):
        return pl.BlockSpec((None, T, G), lambda bi, gi, ci: (bi, ci, off // per + gi))

    vec = pl.BlockSpec((1, G), lambda bi, gi, ci: (0, gi))
    in_specs = [col(_RGX), col(_RGY),
                pl.BlockSpec((RG_CONV, G), lambda bi, gi, ci: (0, gi)),
                vec,
                pl.BlockSpec((None, G, 2 * G), lambda bi, gi, ci: (gi, 0, 0)),
                vec, vec, vec]
    c_in, c_out, c_shapes, c_ops = _cast_specs(casts, grid)
    return pl.pallas_call(
        _with_casts(_rglru_kernel, len(in_specs), 1, len(casts)),
        out_shape=[jax.ShapeDtypeStruct((b, lp, MIX_W), BF16)] + c_shapes,
        grid=grid,
        in_specs=in_specs + c_in,
        out_specs=[pl.BlockSpec((None, T, G), lambda bi, gi, ci: (bi, ci, gi))] + c_out,
        scratch_shapes=[pltpu.VMEM((T + 8, G), F32), pltpu.VMEM((1, G), F32)],
        compiler_params=pltpu.CompilerParams(
            dimension_semantics=("arbitrary", "arbitrary", "arbitrary"), vmem_limit_bytes=VMEM_LIMIT),
        name="rglru",
    )(proj, proj, conv_w, conv_b.reshape(1, MIX_W), w_bd, ba.reshape(1, MIX_W), bx.reshape(1, MIX_W),
      lam.reshape(1, MIX_W), *c_ops)


def _block_diag_gates(wa, wx):
    per = RG_GROUP // RG_BD
    ng = RG_BLOCKS // per
    eye = jnp.eye(per, dtype=F32)

    def bd(w):
        w4 = w.reshape(ng, per, RG_BD, RG_BD)
        return jnp.einsum("gaij,ab->gaibj", w4, eye).reshape(ng, RG_GROUP, RG_GROUP)

    return jnp.concatenate([bd(wa), bd(wx)], axis=-1).astype(BF16)


def _assemble_kernel(x_ref, front_ref, g_ref, o_ref, n_ref):
    r = pl.program_id(1)

    def emit(rows):
        o_ref[...] = rows
        y = rows * lax.rsqrt(jnp.mean(rows * rows, axis=-1, keepdims=True) + EPS)
        n_ref[...] = (y * g_ref[...]).astype(n_ref.dtype)

    @pl.when(r == 0)
    def _():
        emit(front_ref[...])

    @pl.when(r > 0)
    def _():
        emit(x_ref[...])


def _assemble(x, front, gain, casts=()):
    b, seq, d = x.shape
    tr = PAD + N_META
    grid = (b, 1 + seq // tr)
    in_specs = [pl.BlockSpec((None, tr, d), lambda bi, r: (bi, jnp.maximum(r - 1, 0), 0)),
                pl.BlockSpec((tr, d), lambda bi, r: (0, 0)),
                pl.BlockSpec((1, d), lambda bi, r: (0, 0))]
    row_spec = pl.BlockSpec((None, tr, d), lambda bi, r: (bi, r, 0))
    c_in, c_out, c_shapes, c_ops = _cast_specs(casts, grid)
    return pl.pallas_call(
        _with_casts(_assemble_kernel, len(in_specs), 2, len(casts)),
        out_shape=[jax.ShapeDtypeStruct((b, tr + seq, d), x.dtype),
                   jax.ShapeDtypeStruct((b, tr + seq, d), BF16)] + c_shapes,
        grid=grid,
        in_specs=in_specs + c_in,
        out_specs=[row_spec, row_spec] + c_out,
        compiler_params=pltpu.CompilerParams(
            dimension_semantics=("arbitrary", "arbitrary"), vmem_limit_bytes=VMEM_LIMIT),
        name="assemble",
    )(x, front, gain.reshape(1, d), *c_ops)


def _layer(h, xn, l, wt_in_bf, p, lower_bounds, tiles):
    b, lp = p["b"], p["lp"]
    m = b * lp
    tm = tiles["tm"]

    depth = p["wt_in"].shape[0]
    b_in = p["b_in"][l]
    if_lo, if_hi, gate_lo = _IN_SPLITS
    wt_ml, wt_rest, wt_gate = wt_in_bf
    wt_if = jnp.pad(p["wt_in"][l, if_lo:if_hi], ((0, LANES - 2 * ML_HEADS), (0, 0)))
    b_if = jnp.pad(b_in[if_lo:if_hi], (0, LANES - 2 * ML_HEADS)).reshape(1, LANES)

    if xn is None:
        xn = _rmsnorm(h, p["norm_mix"][l])
    proj_ml = _linear_bias(xn, wt_ml, b_in[:N_ML].reshape(1, N_ML), tm, tiles["tn_mix"], "in_proj_ml")
    proj_rest = _linear_bias(xn, wt_rest, b_in[if_hi:gate_lo].reshape(1, N_REST), tm, tiles["tn_mix"],
                             "in_proj_rest")
    proj_ml = proj_ml.reshape(b, lp, N_ML)
    proj_rest = proj_rest.reshape(b, lp, N_REST)
    gates = _linear_bias(xn, wt_if, b_if, tm, LANES, "in_proj_gates").reshape(b, lp, LANES)
    gates_t = jnp.swapaxes(gates[:, :, :8], 1, 2)

    w_up_stack = p["w_up"].reshape(depth, N_BRANCH * MIX_W, D_MODEL)
    sb_casts = []
    if l + 1 < depth:
        sb_casts = [(p["wt_in"], l + 1, 64, 0, N_ML),
                    (p["wt_in"], l + 1, 144, if_hi, N_REST),
                    (p["wt_in"], l + 1, 256, gate_lo, N_BRANCH * D_MODEL)]
    ml_out, w_out_bf, w_up_bf = _mlstm(proj_ml, gates, gates_t, p["ml_norm"][l],
                                       casts=[(p["w_out"], l, 64), (w_up_stack, l, 64)])
    sb_out, *w_in_next = _stick_breaking(proj_rest, p["sb_q_norm"][l], p["sb_k_norm"][l], casts=sb_casts)
    hg_out, w_fgate_bf, w_down_bf = _hgrn2(proj_rest, lower_bounds[l], p["hg_norm"][l],
                                           casts=[(p["w_ffn_gate"], l, 32), (p["w_ffn_down"], l, 128)])
    rg_out, w_fup_bf = _rglru(proj_rest, p["rg_conv_w"][l], p["rg_conv_b"][l],
                              _block_diag_gates(p["rg_wa"][l], p["rg_wx"][l]),
                              p["rg_ba"][l], p["rg_bx"][l], p["rg_lambda"][l], casts=[(p["w_ffn_up"], l, 32)])
    branches = tuple(br.reshape(m, MIX_W) for br in (ml_out, sb_out, hg_out, rg_out))
    merged = _merge(xn, branches, wt_gate, b_in[gate_lo:].reshape(1, N_BRANCH * D_MODEL),
                    w_up_bf.reshape(1, N_BRANCH, MIX_W, D_MODEL), 0, tiles["tm_merge"], tiles["tn_merge"])
    h = _linear_res(merged, w_out_bf[None], 0, h, tm, tiles["tn_out"], D_MODEL, "out_proj")

    d_ff = p["w_ffn_gate"].shape[-1]
    hn = _rmsnorm(h, p["norm_ffn"][l])
    act = _ffn_gate_up(hn, w_fgate_bf[None], w_fup_bf[None], 0, tiles["tm_ff"], tiles["tn_ff"])
    h = _linear_res(act, w_down_bf[None], 0, h, tiles["tm_down"], tiles["tn_down"], d_ff // tiles["nk_down"],
                    "ffn_down")
    return h, (tuple(w_in_next) if w_in_next else None)


def _forward(x, meta, params, tiles):
    b, seq, d = x.shape
    lp = PAD + N_META + seq
    depth = params["w_in"].shape[0]
    hg_lb = params["hg_lb"]
    p_lb = jax.nn.softmax(hg_lb.astype(F32), axis=0)
    lower_bounds = jnp.clip(jnp.cumsum(p_lb, axis=0) - p_lb[0:1], 0.0, 0.999)

    front = jnp.concatenate([jnp.zeros((PAD, d), x.dtype), meta.astype(x.dtype)], axis=0)
    p = dict(params, b=b, lp=lp, wt_in=jnp.swapaxes(params["w_in"], 1, 2))
    if_lo, if_hi, gate_lo = _IN_SPLITS
    h, xn, *wt_in_bf = _assemble(x, front, params["norm_mix"][0],
                                 casts=[(p["wt_in"], 0, 64, 0, N_ML),
                                        (p["wt_in"], 0, 144, if_hi, N_REST),
                                        (p["wt_in"], 0, 256, gate_lo, N_BRANCH * D_MODEL)])
    h = h.reshape(b * lp, d)
    xn = xn.reshape(b * lp, d)
    for l in range(depth):
        h, wt_in_bf = _layer(h, xn, l, wt_in_bf, p, lower_bounds, tiles)
        xn = None
    return h.reshape(b, lp, d)[:, PAD + N_META:]


_TILES = dict(tm=1056, tn_mix=1024, tm_merge=768, tn_merge=256, tn_out=512,
              tm_ff=2112, tn_ff=256, tm_down=768, tn_down=256, nk_down=1)


def kernel(x, meta, norm_mix, norm_ffn, w_in, b_in, ml_norm, sb_q_norm, sb_k_norm, hg_lb, hg_norm,
           rg_conv_w, rg_conv_b, rg_wa, rg_ba, rg_wx, rg_bx, rg_lambda, w_up, w_out,
           w_ffn_gate, w_ffn_up, w_ffn_down):
    params = dict(norm_mix=norm_mix, norm_ffn=norm_ffn, w_in=w_in, b_in=b_in, ml_norm=ml_norm,
                  sb_q_norm=sb_q_norm, sb_k_norm=sb_k_norm, hg_lb=hg_lb, hg_norm=hg_norm,
                  rg_conv_w=rg_conv_w, rg_conv_b=rg_conv_b, rg_wa=rg_wa, rg_ba=rg_ba, rg_wx=rg_wx,
                  rg_bx=rg_bx, rg_lambda=rg_lambda, w_up=w_up, w_out=w_out,
                  w_ffn_gate=w_ffn_gate, w_ffn_up=w_ffn_up, w_ffn_down=w_ffn_down)
    return _forward(x, meta, params, _TILES)
```

```python
import functools

import jax
import jax.numpy as jnp
from jax import lax
from jax.experimental import pallas as pl
from jax.experimental.pallas import tpu as pltpu

F32 = jnp.float32
BF16 = jnp.bfloat16

D_MODEL = 4096
N_META = 16
N_BRANCH = 4
MIX_W = D_MODEL // 4
ML_HEADS = 4
ML_HD = MIX_W // ML_HEADS
SB_HEADS = 8
SB_HD = MIX_W // SB_HEADS
HG_HEADS = 8
HG_HD = MIX_W // HG_HEADS
RG_BLOCKS = 16
RG_BD = MIX_W // RG_BLOCKS
RG_CONV = 4
RG_C = 8.0
EPS = 1e-6
NEG = -1e30

LANES = 128
PAD = LANES - N_META
ML_CHUNK = 128
SB_BLOCK = 128
SB_SPAN = 256
SB_HPS = 8
HG_CHUNK = 64
HG_SUB = 8
HG_HPS = 8
RG_GROUP = 512
RG_T = 128
VMEM_LIMIT = 56 * 1024 * 1024

_SBQ, _SBK, _SBV = 0, 8, 16
_HGQ, _HGF, _HGI, _HGG = 24, 32, 40, 48
_RGX, _RGY = 56, 64
N_ML = 4 * MIX_W
N_REST = 9 * MIX_W
_IN_SPLITS = (N_ML, N_ML + 2 * ML_HEADS, N_ML + 2 * ML_HEADS + N_REST)


def _sigmoid(x):
    return 1.0 / (1.0 + jnp.exp(-x))


def _log_sigmoid(x):
    return jnp.minimum(x, 0.0) - jnp.log1p(jnp.exp(-jnp.abs(x)))


def _split3(x):
    hi = x.astype(BF16)
    r1 = x - hi.astype(F32)
    mid = r1.astype(BF16)
    lo = (r1 - mid.astype(F32)).astype(BF16)
    return hi, mid, lo


def _dot(a, b):
    return jnp.dot(a, b, preferred_element_type=F32)


def _dot_nt(a, b):
    return lax.dot_general(a, b, (((1,), (1,)), ((), ())), preferred_element_type=F32)


def _dot_tn(a, b):
    return lax.dot_general(a, b, (((0,), (0,)), ((), ())), preferred_element_type=F32)


def _tri_left(x, tri):
    hi, mid, lo = _split3(x)
    return _dot(tri, hi) + _dot(tri, mid) + _dot(tri, lo)


def _tri_right(x, tri):
    hi, mid, lo = _split3(x)
    return _dot(hi, tri) + _dot(mid, tri) + _dot(lo, tri)


def _iota(shape, dim):
    return lax.broadcasted_iota(jnp.int32, shape, dim)


def _cast_specs(casts, grid):
    steps = 1
    for g in grid:
        steps *= g

    def step_of(*idx):
        s = idx[0]
        for g, i in zip(grid[1:], idx[1:]):
            s = s * g + i
        return s

    in_specs, out_specs, out_shapes, operands = [], [], [], []
    for w, layer, rows, *span in casts:
        _, r, c = w.shape
        row0, nrows = span if span else (0, r)
        nb = nrows // rows
        assert nb * rows == nrows and nb <= steps and row0 % 8 == 0, (w.shape, rows, span, steps)
        if row0 == 0:
            in_specs.append(pl.BlockSpec(
                (None, rows, c),
                lambda *idx, layer=layer, nb=nb: (layer, jnp.minimum(step_of(*idx), nb - 1), 0)))
        else:
            in_specs.append(pl.BlockSpec(
                (None, pl.Element(rows), pl.Element(c)),
                lambda *idx, layer=layer, nb=nb, row0=row0, rows=rows:
                (layer, (row0 // 8 + jnp.minimum(step_of(*idx), nb - 1) * (rows // 8)) * 8, 0)))
        out_specs.append(pl.BlockSpec(
            (rows, c), lambda *idx, nb=nb: (jnp.minimum(step_of(*idx), nb - 1), 0)))
        out_shapes.append(jax.ShapeDtypeStruct((nrows, c), BF16))
        operands.append(w)
    return in_specs, out_specs, out_shapes, operands


def _with_casts(body, n_in, n_out, n_cast):
    def kernel(*refs):
        ins = refs[:n_in]
        cast_in = refs[n_in:n_in + n_cast]
        outs = refs[n_in + n_cast:n_in + n_cast + n_out]
        cast_out = refs[n_in + n_cast + n_out:n_in + 2 * n_cast + n_out]
        scratch = refs[n_in + 2 * n_cast + n_out:]
        for src, dst in zip(cast_in, cast_out):
            dst[...] = src[...].astype(BF16)
        body(*ins, *outs, *scratch)
    return kernel


def _rmsnorm_kernel(x_ref, g_ref, o_ref):
    x = x_ref[...]
    y = x * lax.rsqrt(jnp.mean(x * x, axis=-1, keepdims=True) + EPS)
    o_ref[...] = (y * g_ref[...]).astype(o_ref.dtype)


def _rmsnorm(x, g, tm=256):
    m, d = x.shape
    return pl.pallas_call(
        _rmsnorm_kernel,
        out_shape=jax.ShapeDtypeStruct((m, d), BF16),
        grid=(m // tm,),
        in_specs=[pl.BlockSpec((tm, d), lambda i: (i, 0)),
                  pl.BlockSpec((1, d), lambda i: (0, 0))],
        out_specs=pl.BlockSpec((tm, d), lambda i: (i, 0)),
        compiler_params=pltpu.CompilerParams(dimension_semantics=("parallel",)),
        name="rmsnorm",
    )(x, g.reshape(1, d))


def _linear_bias_kernel(x_ref, wt_ref, b_ref, o_ref):
    o_ref[...] = _dot_nt(x_ref[...], wt_ref[...].astype(BF16)) + b_ref[...]


def _linear_bias(x, wt, b, tm, tn, name):
    m, k = x.shape
    n = b.shape[1]
    return pl.pallas_call(
        _linear_bias_kernel,
        out_shape=jax.ShapeDtypeStruct((m, n), F32),
        grid=(m // tm, n // tn),
        in_specs=[pl.BlockSpec((tm, k), lambda i, j: (i, 0)),
                  pl.BlockSpec((tn, k), lambda i, j: (j, 0)),
                  pl.BlockSpec((1, tn), lambda i, j: (0, j))],
        out_specs=pl.BlockSpec((tm, tn), lambda i, j: (i, j)),
        compiler_params=pltpu.CompilerParams(
            dimension_semantics=("parallel", "parallel"), vmem_limit_bytes=VMEM_LIMIT),
        name=name,
    )(x, wt, b)


def _linear_res_kernel(x_ref, w_ref, r_ref, o_ref, *, nk):
    part = _dot(x_ref[...], w_ref[...])
    if nk == 1:
        o_ref[...] = r_ref[...] + part
    else:
        k = pl.program_id(2)

        @pl.when(k == 0)
        def _():
            o_ref[...] = r_ref[...] + part

        @pl.when(k > 0)
        def _():
            o_ref[...] += part


def _linear_res(x, w, l, res, tm, tn, tk, name):
    m, k = x.shape
    n = w.shape[2]
    nk = k // tk
    return pl.pallas_call(
        functools.partial(_linear_res_kernel, nk=nk),
        out_shape=jax.ShapeDtypeStruct((m, n), F32),
        grid=(m // tm, n // tn, nk),
        in_specs=[pl.BlockSpec((tm, tk), lambda i, j, kk: (i, kk)),
                  pl.BlockSpec((None, tk, tn), lambda i, j, kk: (l, kk, j)),
                  pl.BlockSpec((tm, tn), lambda i, j, kk: (i, j))],
        out_specs=pl.BlockSpec((tm, tn), lambda i, j, kk: (i, j)),
        compiler_params=pltpu.CompilerParams(
            dimension_semantics=("parallel", "parallel", "arbitrary"), vmem_limit_bytes=VMEM_LIMIT),
        name=name,
    )(x, w, res)


def _merge_kernel(xn_ref, b0_ref, b1_ref, b2_ref, b3_ref, g0_ref, g1_ref, g2_ref, g3_ref,
                  c0_ref, c1_ref, c2_ref, c3_ref, wu_ref, o_ref):
    xn = xn_ref[...]
    acc = None
    branches = (b0_ref, b1_ref, b2_ref, b3_ref)
    gate_w = (g0_ref, g1_ref, g2_ref, g3_ref)
    gate_b = (c0_ref, c1_ref, c2_ref, c3_ref)
    for kb in range(N_BRANCH):
        g = _dot_nt(xn, gate_w[kb][...]) + gate_b[kb][...]
        u = _dot(branches[kb][...], wu_ref[kb])
        t = _sigmoid(g) * u
        acc = t if acc is None else acc + t
    o_ref[...] = acc.astype(o_ref.dtype)


def _merge(xn, branches, w_gate, b_gate, w_up, l, tm, tn):
    m, d = xn.shape
    nj = d // tn
    br_spec = pl.BlockSpec((tm, MIX_W), lambda i, j: (i, 0))

    def gw(kb):
        return pl.BlockSpec((tn, d), lambda i, j: (kb * nj + j, 0))

    def gb(kb):
        return pl.BlockSpec((1, tn), lambda i, j: (0, kb * nj + j))

    return pl.pallas_call(
        _merge_kernel,
        out_shape=jax.ShapeDtypeStruct((m, d), BF16),
        grid=(m // tm, nj),
        in_specs=[pl.BlockSpec((tm, d), lambda i, j: (i, 0)),
                  br_spec, br_spec, br_spec, br_spec,
                  gw(0), gw(1), gw(2), gw(3), gb(0), gb(1), gb(2), gb(3),
                  pl.BlockSpec((None, N_BRANCH, MIX_W, tn), lambda i, j: (l, 0, 0, j))],
        out_specs=pl.BlockSpec((tm, tn), lambda i, j: (i, j)),
        compiler_params=pltpu.CompilerParams(
            dimension_semantics=("parallel", "parallel"), vmem_limit_bytes=VMEM_LIMIT),
        name="merge",
    )(xn, *branches, w_gate, w_gate, w_gate, w_gate, b_gate, b_gate, b_gate, b_gate, w_up)


def _ffn_gu_kernel(x_ref, wg_ref, wu_ref, o_ref):
    x = x_ref[...]
    g = _dot(x, wg_ref[...])
    u = _dot(x, wu_ref[...])
    o_ref[...] = (g * _sigmoid(g) * u).astype(o_ref.dtype)


def _ffn_gate_up(x, wg, wu, l, tm, tn):
    m, k = x.shape
    n = wg.shape[2]
    return pl.pallas_call(
        _ffn_gu_kernel,
        out_shape=jax.ShapeDtypeStruct((m, n), BF16),
        grid=(m // tm, n // tn),
        in_specs=[pl.BlockSpec((tm, k), lambda i, j: (i, 0)),
                  pl.BlockSpec((None, k, tn), lambda i, j: (l, 0, j)),
                  pl.BlockSpec((None, k, tn), lambda i, j: (l, 0, j))],
        out_specs=pl.BlockSpec((tm, tn), lambda i, j: (i, j)),
        compiler_params=pltpu.CompilerParams(
            dimension_semantics=("parallel", "parallel"), vmem_limit_bytes=VMEM_LIMIT),
        name="ffn_gate_up",
    )(x, wg, wu)


def _mlstm_kernel(q_ref, k_ref, v_ref, o_ref, gc_ref, gr_ref, gn_ref, out_ref, c_ref, n_ref, m_ref):
    c = pl.program_id(0)
    T = ML_CHUNK
    nbatch = q_ref.shape[0]

    @pl.when(c == 0)
    def _():
        c_ref[...] = jnp.zeros_like(c_ref)
        n_ref[...] = jnp.zeros_like(n_ref)
        m_ref[...] = jnp.zeros_like(m_ref)

    valid_col = (c * T + _iota((T, 1), 0)) >= PAD
    valid_row = (c * T + _iota((1, T), 1)) >= PAD
    tt = _iota((T, T), 0)
    ss = _iota((T, T), 1)
    causal = ss <= tt
    tril = jnp.where(causal, 1.0, 0.0).astype(BF16)
    triu = jnp.where(tt <= ss, 1.0, 0.0).astype(BF16)
    gcs = [gc_ref[bi] for bi in range(nbatch)]
    grs = [gr_ref[bi] for bi in range(nbatch)]
    bcs = [_tri_left(jnp.where(valid_col, _log_sigmoid(g), 0.0), tril) for g in gcs]
    brs = [_tri_right(jnp.where(valid_row, _log_sigmoid(g), 0.0), triu) for g in grs]

    pairs = [(bi, hd) for bi in range(nbatch) for hd in range(ML_HEADS)]
    heads = range(len(pairs))
    sl = [slice(hd * ML_HD, (hd + 1) * ML_HD) for _, hd in pairs]
    q = [q_ref[pairs[h][0], :, sl[h]] * (ML_HD ** -0.5) for h in heads]
    k = [k_ref[pairs[h][0], :, sl[h]] for h in heads]
    qb = [q[h].astype(BF16) for h in heads]
    vb = [v_ref[pairs[h][0], :, sl[h]].astype(BF16) for h in heads]
    qk = [_dot_nt(qb[h], k[h].astype(BF16)) for h in heads]
    c_prev = [c_ref[h] for h in heads]
    qc = [_dot(qb[h], c_prev[h].astype(BF16)) for h in heads]

    def gate_col(arr, h, off):
        return arr[pairs[h][0]][:, off + pairs[h][1]:off + pairs[h][1] + 1]

    def gate_row(arr, h, off):
        return arr[pairs[h][0]][off + pairs[h][1]:off + pairs[h][1] + 1, :]

    li_col = [jnp.where(valid_col, gate_col(gcs, h, 0), NEG) for h in heads]
    li_row = [jnp.where(valid_row, gate_row(grs, h, 0), NEG) for h in heads]
    b_col = [gate_col(bcs, h, ML_HEADS) for h in heads]
    b_row = [gate_row(brs, h, ML_HEADS) for h in heads]
    m_prev = [m_ref[:, h:h + 1] for h in heads]
    dmat = [jnp.where(causal, b_col[h] - b_row[h] + li_row[h], NEG) for h in heads]
    g_col = [b_col[h] + m_prev[h] for h in heads]
    m_out = [jnp.maximum(g_col[h], jnp.max(dmat[h], axis=1, keepdims=True)) for h in heads]

    b_end = [b_row[h][:, T - 1:T] for h in heads]
    wlog = [b_end[h] - b_col[h] + li_col[h] for h in heads]
    m_new = [jnp.maximum(b_end[h] + m_prev[h], jnp.max(wlog[h], axis=0, keepdims=True)) for h in heads]
    decay = [jnp.exp(b_end[h] + m_prev[h] - m_new[h]) for h in heads]
    kw = [k[h] * jnp.exp(wlog[h] - m_new[h]) for h in heads]
    kv = [_dot(kw[h].T.astype(BF16), vb[h]) for h in heads]

    s = [qk[h] * jnp.exp(dmat[h] - m_out[h]) for h in heads]
    inter = [jnp.exp(g_col[h] - m_out[h]) for h in heads]
    sv = [_dot(s[h].astype(BF16), vb[h]) for h in heads]

    for h in heads:
        num = sv[h] + inter[h] * qc[h]
        den = (jnp.sum(s[h], axis=1, keepdims=True)
               + inter[h] * jnp.sum(q[h] * n_ref[h], axis=1, keepdims=True))
        hh = num / jnp.maximum(jnp.abs(den), jnp.exp(-m_out[h]))
        y = hh * lax.rsqrt(jnp.mean(hh * hh, axis=-1, keepdims=True) + EPS) * gn_ref[:, sl[h]]
        bi = pairs[h][0]
        out_ref[bi, :, sl[h]] = (y * _sigmoid(o_ref[bi, :, sl[h]])).astype(out_ref.dtype)
        c_ref[h] = decay[h] * c_prev[h] + kv[h]
        n_ref[h] = decay[h] * n_ref[h] + jnp.sum(kw[h], axis=0, keepdims=True)
        m_ref[:, h:h + 1] = m_new[h]


def _mlstm(proj, gates, gates_t, gnorm, casts=()):
    b, lp, _ = proj.shape
    T = ML_CHUNK
    grid = (lp // T,)

    def col(idx):
        return pl.BlockSpec((b, T, MIX_W), lambda ci: (0, ci, idx))

    in_specs = [col(0), col(1), col(2), col(3),
                pl.BlockSpec((b, T, LANES), lambda ci: (0, ci, 0)),
                pl.BlockSpec((b, 8, T), lambda ci: (0, 0, ci)),
                pl.BlockSpec((1, MIX_W), lambda ci: (0, 0))]
    c_in, c_out, c_shapes, c_ops = _cast_specs(casts, grid)
    return pl.pallas_call(
        _with_casts(_mlstm_kernel, len(in_specs), 1, len(casts)),
        out_shape=[jax.ShapeDtypeStruct((b, lp, MIX_W), BF16)] + c_shapes,
        grid=grid,
        in_specs=in_specs + c_in,
        out_specs=[pl.BlockSpec((b, T, MIX_W), lambda ci: (0, ci, 0))] + c_out,
        scratch_shapes=[pltpu.VMEM((b * ML_HEADS, ML_HD, ML_HD), F32),
                        pltpu.VMEM((b * ML_HEADS, 1, ML_HD), F32),
                        pltpu.VMEM((1, LANES), F32)],
        compiler_params=pltpu.CompilerParams(
            dimension_semantics=("arbitrary",), vmem_limit_bytes=VMEM_LIMIT),
        name="mlstm",
    )(proj, proj, proj, proj, gates, gates_t, gnorm.reshape(1, MIX_W), *c_ops)


def _sb_kernel(q_ref, k_ref, v_ref, gq_ref, gk_ref, out_ref, kn_ref, vb_ref, acc_ref, run_ref, qn_ref, tri_ref,
               z_ref, a_ref):
    qi = pl.program_id(2)
    T = SB_BLOCK
    W = SB_SPAN
    row0 = pl.multiple_of(qi * T, T)

    @pl.when(jnp.bitwise_and(qi, 1) == 0)
    def _():
        kn_ref[pl.ds(row0 + T, T), :] = jnp.zeros((T, kn_ref.shape[1]), BF16)
        vb_ref[pl.ds(row0 + T, T), :] = jnp.zeros((T, vb_ref.shape[1]), BF16)

    @pl.when(qi == 0)
    def _():
        jj = jnp.bitwise_and(_iota((2 * W, W), 0), W - 1)
        tri_ref[...] = jnp.where(jj > _iota((2 * W, W), 1), 1.0, 0.0).astype(BF16)

    heads = range(SB_HPS)
    sl = [slice(h * SB_HD, (h + 1) * SB_HD) for h in heads]
    vb_ref[pl.ds(row0, T), :] = v_ref[...].astype(BF16)

    @pl.when(qi == 0)
    def _():
        vb_ref[0:PAD, :] = jnp.zeros((PAD, vb_ref.shape[1]), BF16)

    for h in heads:
        kh = k_ref[:, sl[h]]
        kn = kh * lax.rsqrt(jnp.mean(kh * kh, axis=-1, keepdims=True) + EPS) * gk_ref[...]
        kn_ref[pl.ds(row0, T), sl[h]] = kn.astype(BF16)
        qh = q_ref[:, sl[h]]
        qs = qh * lax.rsqrt(jnp.mean(qh * qh, axis=-1, keepdims=True) + EPS) * (gq_ref[...] * SB_HD ** -0.5)
        qn_ref[:, sl[h]] = qs.astype(BF16)

    acc_ref[...] = jnp.zeros_like(acc_ref)
    run_ref[...] = jnp.zeros_like(run_ref)
    t_pos = row0 + _iota((T, W), 0)
    lane = _iota((T, W), 1)
    top = lax.shift_right_logical(qi, 1)

    def scores(col0, slot):
        for h in heads:
            z_ref[slot, h] = _dot_nt(qn_ref[:, sl[h]], kn_ref[pl.ds(col0, W), sl[h]])

    def weighted_values(col0):
        for h in heads:
            acc_ref[h] += _dot(a_ref[h], vb_ref[pl.ds(col0, W), sl[h]])

    def span_step(it, masked, has_prev):
        col0 = pl.multiple_of((top - it) * W, W)
        slot = it & 1
        z = [z_ref[slot, h] for h in heads]
        if has_prev:
            weighted_values(pl.multiple_of(col0 + W, W))
        l1p = [jnp.log(1.0 + jnp.exp(-jnp.abs(z[h]))) for h in heads]
        ls = [jnp.minimum(z[h], 0.0) - l1p[h] for h in heads]
        lk = [ls[h] - z[h] for h in heads]
        if masked:
            vis = col0 + lane < t_pos
            lk = [jnp.where(vis, lk[h], 0.0) for h in heads]
        lk_hi = [lk[h].astype(BF16) for h in heads]
        lk_lo = [(lk[h] - lk_hi[h].astype(F32)).astype(BF16) for h in heads]
        cs = [_dot(jnp.concatenate([lk_hi[h], lk_lo[h]], axis=1), tri_ref[...]) for h in heads]
        scores(pl.multiple_of(jnp.maximum(top - it - 1, 0) * W, W), 1 - slot)
        a = [jnp.exp(ls[h] + cs[h] + run_ref[h]) for h in heads]
        if masked:
            a = [jnp.where(vis, a[h], 0.0) for h in heads]
        for h in heads:
            a_ref[h] = a[h].astype(BF16)
            run_ref[h] += jnp.sum(lk[h], axis=1, keepdims=True)

    scores(pl.multiple_of(top * W, W), 0)
    span_step(0, True, False)

    def body(it, carry):
        span_step(it, False, True)
        return carry

    lax.fori_loop(1, top + 1, body, 0)
    weighted_values(0)
    for h in heads:
        out_ref[:, sl[h]] = acc_ref[h].astype(out_ref.dtype)


def _stick_breaking(proj, gq, gk, casts=()):
    b, lp, _ = proj.shape
    T = SB_BLOCK
    wid = SB_HPS * SB_HD
    ng = SB_HEADS // SB_HPS
    per = wid // LANES
    grid = (b, ng, lp // T)
    in_specs = [pl.BlockSpec((None, T, wid), lambda bi, gi, qi: (bi, qi, _SBQ // per + gi)),
                pl.BlockSpec((None, T, wid), lambda bi, gi, qi: (bi, qi, _SBK // per + gi)),
                pl.BlockSpec((None, T, wid), lambda bi, gi, qi: (bi, qi, _SBV // per + gi)),
                pl.BlockSpec((1, SB_HD), lambda bi, gi, qi: (0, 0)),
                pl.BlockSpec((1, SB_HD), lambda bi, gi, qi: (0, 0))]
    c_in, c_out, c_shapes, c_ops = _cast_specs(casts, grid)
    return pl.pallas_call(
        _with_casts(_sb_kernel, len(in_specs), 1, len(casts)),
        out_shape=[jax.ShapeDtypeStruct((b, lp, MIX_W), BF16)] + c_shapes,
        grid=grid,
        in_specs=in_specs + c_in,
        out_specs=[pl.BlockSpec((None, T, wid), lambda bi, gi, qi: (bi, qi, gi))] + c_out,
        scratch_shapes=[pltpu.VMEM((lp + SB_SPAN - T, wid), BF16),
                        pltpu.VMEM((lp + SB_SPAN - T, wid), BF16),
                        pltpu.VMEM((SB_HPS, T, SB_HD), F32),
                        pltpu.VMEM((SB_HPS, T, 1), F32),
                        pltpu.VMEM((T, wid), BF16),
                        pltpu.VMEM((2 * SB_SPAN, SB_SPAN), BF16),
                        pltpu.VMEM((2, SB_HPS, T, SB_SPAN), F32),
                        pltpu.VMEM((SB_HPS, T, SB_SPAN), BF16)],
        compiler_params=pltpu.CompilerParams(
            dimension_semantics=("arbitrary", "arbitrary", "arbitrary"), vmem_limit_bytes=VMEM_LIMIT),
        name="stick_breaking",
    )(proj, proj, proj, gq.reshape(1, SB_HD), gk.reshape(1, SB_HD), *c_ops)


def _hgrn2_kernel(q_ref, f_ref, i_ref, g_ref, lb_ref, gn_ref, out_ref, st_ref):
    c = pl.program_id(2)
    T = HG_CHUNK
    S8 = HG_SUB

    @pl.when(c == 0)
    def _():
        st_ref[...] = jnp.zeros_like(st_ref)

    valid = (c * T + _iota((T, 1), 0)) >= PAD
    tril = jnp.where(_iota((T, T), 1) <= _iota((T, T), 0), 1.0, 0.0).astype(BF16)
    lane8 = _iota((S8, T), 1)
    row8 = _iota((S8, T), 0)

    heads = range(HG_HPS)
    hs = [slice(h * HG_HD, (h + 1) * HG_HD) for h in heads]
    lb = [lb_ref[:, hs[h]] for h in heads]
    sig = [_sigmoid(f_ref[:, hs[h]]) for h in heads]
    logf = [jnp.where(valid, jnp.log(lb[h] + (1.0 - lb[h]) * sig[h]), 0.0) for h in heads]
    kk = [jnp.where(valid, (1.0 - lb[h]) * (1.0 - sig[h]), 0.0) for h in heads]
    q = [q_ref[:, hs[h]] * _sigmoid(q_ref[:, hs[h]]) for h in heads]
    iv = [i_ref[:, hs[h]].astype(BF16) for h in heads]
    b = [_tri_left(logf[h], tril) for h in heads]

    st = [st_ref[h] for h in heads]
    b_end = [b[h][T - 1:T] for h in heads]
    o_inter = [_dot_nt((q[h] * jnp.exp(b[h])).astype(BF16), st[h].astype(BF16)) for h in heads]
    st_add = [_dot_tn(iv[h], (kk[h] * jnp.exp(b_end[h] - b[h])).astype(BF16)) for h in heads]

    def diag_block(h, blk):
        r0 = blk * S8
        qb = q[h][r0:r0 + S8]
        bb = b[h][r0:r0 + S8]
        diag = jnp.zeros((S8, T), F32)
        for j in range(S8):
            s_idx = r0 + j
            p = qb * kk[h][s_idx:s_idx + 1] * jnp.exp(jnp.minimum(bb - b[h][s_idx:s_idx + 1], 0.0))
            diag = jnp.where(lane8 == s_idx, jnp.sum(p, axis=1, keepdims=True), diag)
        return jnp.where(lane8 - r0 <= row8, diag, 0.0)

    def off_block(h, blk):
        r0 = blk * S8
        bref = b[h][r0 - 1:r0]
        qs = (q[h][r0:r0 + S8] * jnp.exp(b[h][r0:r0 + S8] - bref)).astype(BF16)
        ks = (kk[h] * jnp.exp(jnp.minimum(bref - b[h], 0.0))).astype(BF16)
        return _dot_nt(qs, ks)

    nblk = T // S8
    off = [[off_block(h, blk) for h in heads] for blk in range(1, nblk)]
    att = []
    for h in heads:
        rows = [diag_block(h, 0)]
        for blk in range(1, nblk):
            rows.append(jnp.where(lane8 < blk * S8, off[blk - 1][h], diag_block(h, blk)))
        att.append(jnp.concatenate(rows, axis=0).astype(BF16))
    o_intra = [_dot(att[h], iv[h]) for h in heads]

    for h in heads:
        o = o_intra[h] + o_inter[h]
        st_ref[h] = st[h] * jnp.exp(b_end[h]) + st_add[h]
        y = o * lax.rsqrt(jnp.mean(o * o, axis=-1, keepdims=True) + EPS) * gn_ref[:, hs[h]]
        out_ref[:, hs[h]] = (y * _sigmoid(g_ref[:, hs[h]])).astype(out_ref.dtype)


def _hgrn2(proj, lb, gnorm, casts=()):
    b, lp, _ = proj.shape
    T = HG_CHUNK
    wid = HG_HPS * HG_HD
    ng = HG_HEADS // HG_HPS
    per = wid // LANES
    grid = (b, ng, lp // T)

    def col(off):
        return pl.BlockSpec((None, T, wid), lambda bi, gi, ci: (bi, ci, off // per + gi))

    vec = pl.BlockSpec((1, wid), lambda bi, gi, ci: (0, gi))
    in_specs = [col(_HGQ), col(_HGF), col(_HGI), col(_HGG), vec, vec]
    c_in, c_out, c_shapes, c_ops = _cast_specs(casts, grid)
    return pl.pallas_call(
        _with_casts(_hgrn2_kernel, len(in_specs), 1, len(casts)),
        out_shape=[jax.ShapeDtypeStruct((b, lp, MIX_W), BF16)] + c_shapes,
        grid=grid,
        in_specs=in_specs + c_in,
        out_specs=[pl.BlockSpec((None, T, wid), lambda bi, gi, ci: (bi, ci, gi))] + c_out,
        scratch_shapes=[pltpu.VMEM((HG_HPS, HG_HD, HG_HD), F32)],
        compiler_params=pltpu.CompilerParams(
            dimension_semantics=("arbitrary", "arbitrary", "arbitrary"), vmem_limit_bytes=VMEM_LIMIT),
        name="hgrn2",
    )(proj, proj, proj, proj, lb.reshape(1, MIX_W), gnorm.reshape(1, MIX_W), *c_ops)


def _rglru_kernel(x_ref, y_ref, cw_ref, cb_ref, w_ref, ba_ref, bx_ref, lam_ref, out_ref, xbuf_ref, h_ref):
    c = pl.program_id(2)
    T = RG_T
    G = RG_GROUP

    @pl.when(c == 0)
    def _():
        xbuf_ref[0:8, :] = jnp.zeros((8, G), F32)
        h_ref[...] = jnp.zeros_like(h_ref)

    valid = (c * T + _iota((T, 1), 0)) >= PAD
    x = jnp.where(valid, x_ref[...], 0.0)
    xbuf_ref[8:8 + T, :] = x
    xc = cb_ref[...] + cw_ref[RG_CONV - 1:RG_CONV, :] * x
    for j in range(1, RG_CONV):
        xc = xc + cw_ref[RG_CONV - 1 - j:RG_CONV - j, :] * xbuf_ref[8 - j:8 - j + T, :]
    xbuf_ref[0:8, :] = x[T - 8:T]

    pre = _dot(xc.astype(BF16), w_ref[...])
    r = _sigmoid(pre[:, :G] + ba_ref[...])
    ig = _sigmoid(pre[:, G:] + bx_ref[...])
    lam = lam_ref[...]
    softplus_neg = jnp.maximum(-lam, 0.0) + jnp.log1p(jnp.exp(-jnp.abs(lam)))
    log_a = -RG_C * r * softplus_neg
    a = jnp.exp(log_a)
    th = jnp.tanh(log_a)
    u = jnp.where(valid, jnp.sqrt(-2.0 * th / (1.0 - th)) * (ig * xc), 0.0)

    row = _iota((T, G), 0)
    shift = 1
    while shift < T:
        keep = row >= shift
        u_s = pltpu.roll(u, shift, axis=0)
        a_s = pltpu.roll(a, shift, axis=0)
        u = jnp.where(keep, a * u_s + u, u)
        a = jnp.where(keep, a * a_s, a)
        shift *= 2
    hseq = a * h_ref[...] + u
    h_ref[...] = hseq[T - 1:T]

    y = y_ref[...]
    gelu = 0.5 * y * (1.0 + jnp.tanh(0.7978845608028654 * (y + 0.044715 * (y * y * y))))
    out_ref[...] = (hseq * gelu).astype(out_ref.dtype)


def _rglru(proj, conv_w, conv_b, w_bd, ba, bx, lam, casts=()):
    b, lp, _ = proj.shape
    T = RG_T
    G = RG_GROUP
    ng = MIX_W // G
    per = G // LANES
    grid = (b, ng, lp // T)

    def col(off):
        return pl.BlockSpec((None, T, G), lambda bi, gi, ci: (bi, ci, off // per + gi))

    vec = pl.BlockSpec((1, G), lambda bi, gi, ci: (0, gi))
    in_specs = [col(_RGX), col(_RGY),
                pl.BlockSpec((RG_CONV, G), lambda bi, gi, ci: (0, gi)),
                vec,
                pl.BlockSpec((None, G, 2 * G), lambda bi, gi, ci: (gi, 0, 0)),
                vec, vec, vec]
    c_in, c_out, c_shapes, c_ops = _cast_specs(casts, grid)
    return pl.pallas_call(
        _with_casts(_rglru_kernel, len(in_specs), 1, len(casts)),
        out_shape=[jax.ShapeDtypeStruct((b, lp, MIX_W), BF16)] + c_shapes,
        grid=grid,
        in_specs=in_specs + c_in,
        out_specs=[pl.BlockSpec((None, T, G), lambda bi, gi, ci: (bi, ci, gi))] + c_out,
        scratch_shapes=[pltpu.VMEM((T + 8, G), F32), pltpu.VMEM((1, G), F32)],
        compiler_params=pltpu.CompilerParams(
            dimension_semantics=("arbitrary", "arbitrary", "arbitrary"), vmem_limit_bytes=VMEM_LIMIT),
        name="rglru",
    )(proj, proj, conv_w, conv_b.reshape(1, MIX_W), w_bd, ba.reshape(1, MIX_W), bx.reshape(1, MIX_W),
      lam.reshape(1, MIX_W), *c_ops)


def _block_diag_gates(wa, wx):
    per = RG_GROUP // RG_BD
    ng = RG_BLOCKS // per
    eye = jnp.eye(per, dtype=F32)

    def bd(w):
        w4 = w.reshape(ng, per, RG_BD, RG_BD)
        return jnp.einsum("gaij,ab->gaibj", w4, eye).reshape(ng, RG_GROUP, RG_GROUP)

    return jnp.concatenate([bd(wa), bd(wx)], axis=-1).astype(BF16)


def _assemble_kernel(x_ref, front_ref, g_ref, o_ref, n_ref):
    r = pl.program_id(1)

    def emit(rows):
        o_ref[...] = rows
        y = rows * lax.rsqrt(jnp.mean(rows * rows, axis=-1, keepdims=True) + EPS)
        n_ref[...] = (y * g_ref[...]).astype(n_ref.dtype)

    @pl.when(r == 0)
    def _():
        emit(front_ref[...])

    @pl.when(r > 0)
    def _():
        emit(x_ref[...])


def _assemble(x, front, gain, casts=()):
    b, seq, d = x.shape
    tr = PAD + N_META
    grid = (b, 1 + seq // tr)
    in_specs = [pl.BlockSpec((None, tr, d), lambda bi, r: (bi, jnp.maximum(r - 1, 0), 0)),
                pl.BlockSpec((tr, d), lambda bi, r: (0, 0)),
                pl.BlockSpec((1, d), lambda bi, r: (0, 0))]
    row_spec = pl.BlockSpec((None, tr, d), lambda bi, r: (bi, r, 0))
    c_in, c_out, c_shapes, c_ops = _cast_specs(casts, grid)
    return pl.pallas_call(
        _with_casts(_assemble_kernel, len(in_specs), 2, len(casts)),
        out_shape=[jax.ShapeDtypeStruct((b, tr + seq, d), x.dtype),
                   jax.ShapeDtypeStruct((b, tr + seq, d), BF16)] + c_shapes,
        grid=grid,
        in_specs=in_specs + c_in,
        out_specs=[row_spec, row_spec] + c_out,
        compiler_params=pltpu.CompilerParams(
            dimension_semantics=("arbitrary", "arbitrary"), vmem_limit_bytes=VMEM_LIMIT),
        name="assemble",
    )(x, front, gain.reshape(1, d), *c_ops)


def _layer(h, xn, l, wt_in_bf, p, lower_bounds, tiles):
    b, lp = p["b"], p["lp"]
    m = b * lp
    tm = tiles["tm"]

    depth = p["wt_in"].shape[0]
    b_in = p["b_in"][l]
    if_lo, if_hi, gate_lo = _IN_SPLITS
    wt_ml, wt_rest, wt_gate = wt_in_bf
    wt_if = jnp.pad(p["wt_in"][l, if_lo:if_hi], ((0, LANES - 2 * ML_HEADS), (0, 0)))
    b_if = jnp.pad(b_in[if_lo:if_hi], (0, LANES - 2 * ML_HEADS)).reshape(1, LANES)

    if xn is None:
        xn = _rmsnorm(h, p["norm_mix"][l])
    proj_ml = _linear_bias(xn, wt_ml, b_in[:N_ML].reshape(1, N_ML), tm, tiles["tn_mix"], "in_proj_ml")
    proj_rest = _linear_bias(xn, wt_rest, b_in[if_hi:gate_lo].reshape(1, N_REST), tm, tiles["tn_mix"],
                             "in_proj_rest")
    proj_ml = proj_ml.reshape(b, lp, N_ML)
    proj_rest = proj_rest.reshape(b, lp, N_REST)
    gates = _linear_bias(xn, wt_if, b_if, tm, LANES, "in_proj_gates").reshape(b, lp, LANES)
    gates_t = jnp.swapaxes(gates[:, :, :8], 1, 2)

    w_up_stack = p["w_up"].reshape(depth, N_BRANCH * MIX_W, D_MODEL)
    sb_casts = []
    if l + 1 < depth:
        sb_casts = [(p["wt_in"], l + 1, 64, 0, N_ML),
                    (p["wt_in"], l + 1, 144, if_hi, N_REST),
                    (p["wt_in"], l + 1, 256, gate_lo, N_BRANCH * D_MODEL)]
    ml_out, w_out_bf, w_up_bf = _mlstm(proj_ml, gates, gates_t, p["ml_norm"][l],
                                       casts=[(p["w_out"], l, 128), (w_up_stack, l, 128)])
    sb_out, *w_in_next = _stick_breaking(proj_rest, p["sb_q_norm"][l], p["sb_k_norm"][l], casts=sb_casts)
    hg_out, w_fgate_bf, w_down_bf = _hgrn2(proj_rest, lower_bounds[l], p["hg_norm"][l],
                                           casts=[(p["w_ffn_gate"], l, 32), (p["w_ffn_down"], l, 128)])
    rg_out, w_fup_bf = _rglru(proj_rest, p["rg_conv_w"][l], p["rg_conv_b"][l],
                              _block_diag_gates(p["rg_wa"][l], p["rg_wx"][l]),
                              p["rg_ba"][l], p["rg_bx"][l], p["rg_lambda"][l], casts=[(p["w_ffn_up"], l, 32)])
    branches = tuple(br.reshape(m, MIX_W) for br in (ml_out, sb_out, hg_out, rg_out))
    merged = _merge(xn, branches, wt_gate, b_in[gate_lo:].reshape(1, N_BRANCH * D_MODEL),
                    w_up_bf.reshape(1, N_BRANCH, MIX_W, D_MODEL), 0, tiles["tm_merge"], tiles["tn_merge"])
    h = _linear_res(merged, w_out_bf[None], 0, h, tm, tiles["tn_out"], D_MODEL, "out_proj")

    d_ff = p["w_ffn_gate"].shape[-1]
    hn = _rmsnorm(h, p["norm_ffn"][l])
    act = _ffn_gate_up(hn, w_fgate_bf[None], w_fup_bf[None], 0, tiles["tm_ff"], tiles["tn_ff"])
    h = _linear_res(act, w_down_bf[None], 0, h, tiles["tm_down"], tiles["tn_down"], d_ff // tiles["nk_down"],
                    "ffn_down")
    return h, (tuple(w_in_next) if w_in_next else None)


def _forward(x, meta, params, tiles):
    b, seq, d = x.shape
    lp = PAD + N_META + seq
    depth = params["w_in"].shape[0]
    hg_lb = params["hg_lb"]
    p_lb = jax.nn.softmax(hg_lb.astype(F32), axis=0)
    lower_bounds = jnp.clip(jnp.cumsum(p_lb, axis=0) - p_lb[0:1], 0.0, 0.999)

    front = jnp.concatenate([jnp.zeros((PAD, d), x.dtype), meta.astype(x.dtype)], axis=0)
    p = dict(params, b=b, lp=lp, wt_in=jnp.swapaxes(params["w_in"], 1, 2))
    if_lo, if_hi, gate_lo = _IN_SPLITS
    h, xn, *wt_in_bf = _assemble(x, front, params["norm_mix"][0],
                                 casts=[(p["wt_in"], 0, 64, 0, N_ML),
                                        (p["wt_in"], 0, 144, if_hi, N_REST),
                                        (p["wt_in"], 0, 256, gate_lo, N_BRANCH * D_MODEL)])
    h = h.reshape(b * lp, d)
    xn = xn.reshape(b * lp, d)
    for l in range(depth):
        h, wt_in_bf = _layer(h, xn, l, wt_in_bf, p, lower_bounds, tiles)
        xn = None
    return h.reshape(b, lp, d)[:, PAD + N_META:]


_TILES = dict(tm=1056, tn_mix=1024, tm_merge=768, tn_merge=256, tn_out=512,
              tm_ff=2112, tn_ff=256, tm_down=768, tn_down=256, nk_down=1)


def kernel(x, meta, norm_mix, norm_ffn, w_in, b_in, ml_norm, sb_q_norm, sb_k_norm, hg_lb, hg_norm,
           rg_conv_w, rg_conv_b, rg_wa, rg_ba, rg_wx, rg_bx, rg_lambda, w_up, w_out,
           w_ffn_gate, w_ffn_up, w_ffn_down):
    params = dict(norm_mix=norm_mix, norm_ffn=norm_ffn, w_in=w_in, b_in=b_in, ml_norm=ml_norm,
                  sb_q_norm=sb_q_norm, sb_k_norm=sb_k_norm, hg_lb=hg_lb, hg_norm=hg_norm,
                  rg_conv_w=rg_conv_w, rg_conv_b=rg_conv_b, rg_wa=rg_wa, rg_ba=rg_ba, rg_wx=rg_wx,
                  rg_bx=rg_bx, rg_lambda=rg_lambda, w_up=w_up, w_out=w_out,
                  w_ffn_gate=w_ffn_gate, w_ffn_up=w_ffn_up, w_ffn_down=w_ffn_down)
    return _forward(x, meta, params, _TILES)
```

```python
import functools

import jax
import jax.numpy as jnp
from jax import lax
from jax.experimental import pallas as pl
from jax.experimental.pallas import tpu as pltpu

F32 = jnp.float32
BF16 = jnp.bfloat16

D_MODEL = 4096
N_META = 16
N_BRANCH = 4
MIX_W = D_MODEL // 4
ML_HEADS = 4
ML_HD = MIX_W // ML_HEADS
SB_HEADS = 8
SB_HD = MIX_W // SB_HEADS
HG_HEADS = 8
HG_HD = MIX_W // HG_HEADS
RG_BLOCKS = 16
RG_BD = MIX_W // RG_BLOCKS
RG_CONV = 4
RG_C = 8.0
EPS = 1e-6
NEG = -1e30

LANES = 128
PAD = LANES - N_META
ML_CHUNK = 128
SB_BLOCK = 128
SB_SPAN = 256
SB_HPS = 8
HG_CHUNK = 64
HG_SUB = 8
HG_HPS = 8
RG_GROUP = 512
RG_T = 128
VMEM_LIMIT = 56 * 1024 * 1024

_SBQ, _SBK, _SBV = 0, 8, 16
_HGQ, _HGF, _HGI, _HGG = 24, 32, 40, 48
_RGX, _RGY = 56, 64
N_ML = 4 * MIX_W
N_REST = 9 * MIX_W
_IN_SPLITS = (N_ML, N_ML + 2 * ML_HEADS, N_ML + 2 * ML_HEADS + N_REST)


def _sigmoid(x):
    return 1.0 / (1.0 + jnp.exp(-x))


def _log_sigmoid(x):
    return jnp.minimum(x, 0.0) - jnp.log1p(jnp.exp(-jnp.abs(x)))


def _split3(x):
    hi = x.astype(BF16)
    r1 = x - hi.astype(F32)
    mid = r1.astype(BF16)
    lo = (r1 - mid.astype(F32)).astype(BF16)
    return hi, mid, lo


def _dot(a, b):
    return jnp.dot(a, b, preferred_element_type=F32)


def _dot_nt(a, b):
    return lax.dot_general(a, b, (((1,), (1,)), ((), ())), preferred_element_type=F32)


def _dot_tn(a, b):
    return lax.dot_general(a, b, (((0,), (0,)), ((), ())), preferred_element_type=F32)


def _tri_left(x, tri):
    hi, mid, lo = _split3(x)
    return _dot(tri, hi) + _dot(tri, mid) + _dot(tri, lo)


def _tri_right(x, tri):
    hi, mid, lo = _split3(x)
    return _dot(hi, tri) + _dot(mid, tri) + _dot(lo, tri)


def _iota(shape, dim):
    return lax.broadcasted_iota(jnp.int32, shape, dim)


def _cast_specs(casts, grid):
    steps = 1
    for g in grid:
        steps *= g

    def step_of(*idx):
        s = idx[0]
        for g, i in zip(grid[1:], idx[1:]):
            s = s * g + i
        return s

    in_specs, out_specs, out_shapes, operands = [], [], [], []
    for w, layer, rows, *span in casts:
        _, r, c = w.shape
        row0, nrows = span if span else (0, r)
        nb = nrows // rows
        assert nb * rows == nrows and nb <= steps and row0 % 8 == 0, (w.shape, rows, span, steps)
        if row0 == 0:
            in_specs.append(pl.BlockSpec(
                (None, rows, c),
                lambda *idx, layer=layer, nb=nb: (layer, jnp.minimum(step_of(*idx), nb - 1), 0)))
        else:
            in_specs.append(pl.BlockSpec(
                (None, pl.Element(rows), pl.Element(c)),
                lambda *idx, layer=layer, nb=nb, row0=row0, rows=rows:
                (layer, (row0 // 8 + jnp.minimum(step_of(*idx), nb - 1) * (rows // 8)) * 8, 0)))
        out_specs.append(pl.BlockSpec(
            (rows, c), lambda *idx, nb=nb: (jnp.minimum(step_of(*idx), nb - 1), 0)))
        out_shapes.append(jax.ShapeDtypeStruct((nrows, c), BF16))
        operands.append(w)
    return in_specs, out_specs, out_shapes, operands


def _with_casts(body, n_in, n_out, n_cast):
    def kernel(*refs):
        ins = refs[:n_in]
        cast_in = refs[n_in:n_in + n_cast]
        outs = refs[n_in + n_cast:n_in + n_cast + n_out]
        cast_out = refs[n_in + n_cast + n_out:n_in + 2 * n_cast + n_out]
        scratch = refs[n_in + 2 * n_cast + n_out:]
        for src, dst in zip(cast_in, cast_out):
            dst[...] = src[...].astype(BF16)
        body(*ins, *outs, *scratch)
    return kernel


def _rmsnorm_kernel(x_ref, g_ref, o_ref):
    x = x_ref[...]
    y = x * lax.rsqrt(jnp.mean(x * x, axis=-1, keepdims=True) + EPS)
    o_ref[...] = (y * g_ref[...]).astype(o_ref.dtype)


def _rmsnorm(x, g, tm=256):
    m, d = x.shape
    return pl.pallas_call(
        _rmsnorm_kernel,
        out_shape=jax.ShapeDtypeStruct((m, d), BF16),
        grid=(m // tm,),
        in_specs=[pl.BlockSpec((tm, d), lambda i: (i, 0)),
                  pl.BlockSpec((1, d), lambda i: (0, 0))],
        out_specs=pl.BlockSpec((tm, d), lambda i: (i, 0)),
        compiler_params=pltpu.CompilerParams(dimension_semantics=("parallel",)),
        name="rmsnorm",
    )(x, g.reshape(1, d))


def _linear_bias_kernel(x_ref, wt_ref, b_ref, o_ref):
    o_ref[...] = _dot_nt(x_ref[...], wt_ref[...].astype(BF16)) + b_ref[...]


def _linear_bias(x, wt, b, tm, tn, name):
    m, k = x.shape
    n = b.shape[1]
    return pl.pallas_call(
        _linear_bias_kernel,
        out_shape=jax.ShapeDtypeStruct((m, n), F32),
        grid=(m // tm, n // tn),
        in_specs=[pl.BlockSpec((tm, k), lambda i, j: (i, 0)),
                  pl.BlockSpec((tn, k), lambda i, j: (j, 0)),
                  pl.BlockSpec((1, tn), lambda i, j: (0, j))],
        out_specs=pl.BlockSpec((tm, tn), lambda i, j: (i, j)),
        compiler_params=pltpu.CompilerParams(
            dimension_semantics=("parallel", "parallel"), vmem_limit_bytes=VMEM_LIMIT),
        name=name,
    )(x, wt, b)


def _linear_res_kernel(x_ref, w_ref, r_ref, o_ref, *, nk):
    part = _dot(x_ref[...], w_ref[...])
    if nk == 1:
        o_ref[...] = r_ref[...] + part
    else:
        k = pl.program_id(2)

        @pl.when(k == 0)
        def _():
            o_ref[...] = r_ref[...] + part

        @pl.when(k > 0)
        def _():
            o_ref[...] += part


def _linear_res(x, w, l, res, tm, tn, tk, name):
    m, k = x.shape
    n = w.shape[2]
    nk = k // tk
    return pl.pallas_call(
        functools.partial(_linear_res_kernel, nk=nk),
        out_shape=jax.ShapeDtypeStruct((m, n), F32),
        grid=(m // tm, n // tn, nk),
        in_specs=[pl.BlockSpec((tm, tk), lambda i, j, kk: (i, kk)),
                  pl.BlockSpec((None, tk, tn), lambda i, j, kk: (l, kk, j)),
                  pl.BlockSpec((tm, tn), lambda i, j, kk: (i, j))],
        out_specs=pl.BlockSpec((tm, tn), lambda i, j, kk: (i, j)),
        compiler_params=pltpu.CompilerParams(
            dimension_semantics=("parallel", "parallel", "arbitrary"), vmem_limit_bytes=VMEM_LIMIT),
        name=name,
    )(x, w, res)


def _linear_res_tokens(x, w, res, skip, tm, tn, name):
    b, lp, k = x.shape
    n = w.shape[2]
    seq = lp - skip
    sub = 16
    assert skip % sub == 0 and tm % sub == 0 and seq % tm == 0 and tn % LANES == 0

    def row(i):
        return (skip // sub + i * (tm // sub)) * sub

    return pl.pallas_call(
        functools.partial(_linear_res_kernel, nk=1),
        out_shape=jax.ShapeDtypeStruct((b, seq, n), F32),
        grid=(b, seq // tm, n // tn),
        in_specs=[pl.BlockSpec((None, pl.Element(tm), pl.Element(k)), lambda bi, i, j: (bi, row(i), 0)),
                  pl.BlockSpec((None, k, tn), lambda bi, i, j: (0, 0, j)),
                  pl.BlockSpec((None, pl.Element(tm), pl.Element(tn)),
                               lambda bi, i, j: (bi, row(i), j * (tn // LANES) * LANES))],
        out_specs=pl.BlockSpec((None, tm, tn), lambda bi, i, j: (bi, i, j)),
        compiler_params=pltpu.CompilerParams(
            dimension_semantics=("parallel", "parallel", "parallel"), vmem_limit_bytes=VMEM_LIMIT),
        name=name,
    )(x, w, res)


def _merge_kernel(xn_ref, b0_ref, b1_ref, b2_ref, b3_ref, g0_ref, g1_ref, g2_ref, g3_ref,
                  c0_ref, c1_ref, c2_ref, c3_ref, wu_ref, o_ref):
    xn = xn_ref[...]
    acc = None
    branches = (b0_ref, b1_ref, b2_ref, b3_ref)
    gate_w = (g0_ref, g1_ref, g2_ref, g3_ref)
    gate_b = (c0_ref, c1_ref, c2_ref, c3_ref)
    for kb in range(N_BRANCH):
        g = _dot_nt(xn, gate_w[kb][...]) + gate_b[kb][...]
        u = _dot(branches[kb][...], wu_ref[kb])
        t = _sigmoid(g) * u
        acc = t if acc is None else acc + t
    o_ref[...] = acc.astype(o_ref.dtype)


def _merge(xn, branches, w_gate, b_gate, w_up, l, tm, tn):
    m, d = xn.shape
    nj = d // tn
    br_spec = pl.BlockSpec((tm, MIX_W), lambda i, j: (i, 0))

    def gw(kb):
        return pl.BlockSpec((tn, d), lambda i, j: (kb * nj + j, 0))

    def gb(kb):
        return pl.BlockSpec((1, tn), lambda i, j: (0, kb * nj + j))

    return pl.pallas_call(
        _merge_kernel,
        out_shape=jax.ShapeDtypeStruct((m, d), BF16),
        grid=(m // tm, nj),
        in_specs=[pl.BlockSpec((tm, d), lambda i, j: (i, 0)),
                  br_spec, br_spec, br_spec, br_spec,
                  gw(0), gw(1), gw(2), gw(3), gb(0), gb(1), gb(2), gb(3),
                  pl.BlockSpec((None, N_BRANCH, MIX_W, tn), lambda i, j: (l, 0, 0, j))],
        out_specs=pl.BlockSpec((tm, tn), lambda i, j: (i, j)),
        compiler_params=pltpu.CompilerParams(
            dimension_semantics=("parallel", "parallel"), vmem_limit_bytes=VMEM_LIMIT),
        name="merge",
    )(xn, *branches, w_gate, w_gate, w_gate, w_gate, b_gate, b_gate, b_gate, b_gate, w_up)


def _ffn_gu_kernel(x_ref, wg_ref, wu_ref, o_ref):
    x = x_ref[...]
    g = _dot(x, wg_ref[...])
    u = _dot(x, wu_ref[...])
    o_ref[...] = (g * _sigmoid(g) * u).astype(o_ref.dtype)


def _ffn_gate_up(x, wg, wu, l, tm, tn):
    m, k = x.shape
    n = wg.shape[2]
    return pl.pallas_call(
        _ffn_gu_kernel,
        out_shape=jax.ShapeDtypeStruct((m, n), BF16),
        grid=(m // tm, n // tn),
        in_specs=[pl.BlockSpec((tm, k), lambda i, j: (i, 0)),
                  pl.BlockSpec((None, k, tn), lambda i, j: (l, 0, j)),
                  pl.BlockSpec((None, k, tn), lambda i, j: (l, 0, j))],
        out_specs=pl.BlockSpec((tm, tn), lambda i, j: (i, j)),
        compiler_params=pltpu.CompilerParams(
            dimension_semantics=("parallel", "parallel"), vmem_limit_bytes=VMEM_LIMIT),
        name="ffn_gate_up",
    )(x, wg, wu)


def _mlstm_kernel(q_ref, k_ref, v_ref, o_ref, gc_ref, gr_ref, gn_ref, out_ref, c_ref, n_ref, m_ref):
    c = pl.program_id(0)
    T = ML_CHUNK
    nbatch = q_ref.shape[0]

    @pl.when(c == 0)
    def _():
        c_ref[...] = jnp.zeros_like(c_ref)
        n_ref[...] = jnp.zeros_like(n_ref)
        m_ref[...] = jnp.zeros_like(m_ref)

    valid_col = (c * T + _iota((T, 1), 0)) >= PAD
    valid_row = (c * T + _iota((1, T), 1)) >= PAD
    tt = _iota((T, T), 0)
    ss = _iota((T, T), 1)
    causal = ss <= tt
    tril = jnp.where(causal, 1.0, 0.0).astype(BF16)
    triu = jnp.where(tt <= ss, 1.0, 0.0).astype(BF16)
    gcs = [gc_ref[bi] for bi in range(nbatch)]
    grs = [gr_ref[bi] for bi in range(nbatch)]
    bcs = [_tri_left(jnp.where(valid_col, _log_sigmoid(g), 0.0), tril) for g in gcs]
    brs = [_tri_right(jnp.where(valid_row, _log_sigmoid(g), 0.0), triu) for g in grs]

    pairs = [(bi, hd) for bi in range(nbatch) for hd in range(ML_HEADS)]
    heads = range(len(pairs))
    sl = [slice(hd * ML_HD, (hd + 1) * ML_HD) for _, hd in pairs]
    q = [q_ref[pairs[h][0], :, sl[h]] * (ML_HD ** -0.5) for h in heads]
    k = [k_ref[pairs[h][0], :, sl[h]] for h in heads]
    qb = [q[h].astype(BF16) for h in heads]
    vb = [v_ref[pairs[h][0], :, sl[h]].astype(BF16) for h in heads]
    qk = [_dot_nt(qb[h], k[h].astype(BF16)) for h in heads]
    c_prev = [c_ref[h] for h in heads]
    qc = [_dot(qb[h], c_prev[h].astype(BF16)) for h in heads]

    def gate_col(arr, h, off):
        return arr[pairs[h][0]][:, off + pairs[h][1]:off + pairs[h][1] + 1]

    def gate_row(arr, h, off):
        return arr[pairs[h][0]][off + pairs[h][1]:off + pairs[h][1] + 1, :]

    li_col = [jnp.where(valid_col, gate_col(gcs, h, 0), NEG) for h in heads]
    li_row = [jnp.where(valid_row, gate_row(grs, h, 0), NEG) for h in heads]
    b_col = [gate_col(bcs, h, ML_HEADS) for h in heads]
    b_row = [gate_row(brs, h, ML_HEADS) for h in heads]
    m_prev = [m_ref[:, h:h + 1] for h in heads]
    dmat = [jnp.where(causal, b_col[h] - b_row[h] + li_row[h], NEG) for h in heads]
    g_col = [b_col[h] + m_prev[h] for h in heads]
    m_out = [jnp.maximum(g_col[h], jnp.max(dmat[h], axis=1, keepdims=True)) for h in heads]

    b_end = [b_row[h][:, T - 1:T] for h in heads]
    wlog = [b_end[h] - b_col[h] + li_col[h] for h in heads]
    m_new = [jnp.maximum(b_end[h] + m_prev[h], jnp.max(wlog[h], axis=0, keepdims=True)) for h in heads]
    decay = [jnp.exp(b_end[h] + m_prev[h] - m_new[h]) for h in heads]
    kw = [k[h] * jnp.exp(wlog[h] - m_new[h]) for h in heads]
    kv = [_dot(kw[h].T.astype(BF16), vb[h]) for h in heads]

    s = [qk[h] * jnp.exp(dmat[h] - m_out[h]) for h in heads]
    inter = [jnp.exp(g_col[h] - m_out[h]) for h in heads]
    sv = [_dot(s[h].astype(BF16), vb[h]) for h in heads]

    for h in heads:
        num = sv[h] + inter[h] * qc[h]
        den = (jnp.sum(s[h], axis=1, keepdims=True)
               + inter[h] * jnp.sum(q[h] * n_ref[h], axis=1, keepdims=True))
        hh = num / jnp.maximum(jnp.abs(den), jnp.exp(-m_out[h]))
        y = hh * lax.rsqrt(jnp.mean(hh * hh, axis=-1, keepdims=True) + EPS) * gn_ref[:, sl[h]]
        bi = pairs[h][0]
        out_ref[bi, :, sl[h]] = (y * _sigmoid(o_ref[bi, :, sl[h]])).astype(out_ref.dtype)
        c_ref[h] = decay[h] * c_prev[h] + kv[h]
        n_ref[h] = decay[h] * n_ref[h] + jnp.sum(kw[h], axis=0, keepdims=True)
        m_ref[:, h:h + 1] = m_new[h]


def _mlstm(proj, gates, gates_t, gnorm, casts=()):
    b, lp, _ = proj.shape
    T = ML_CHUNK
    grid = (lp // T,)

    def col(idx):
        return pl.BlockSpec((b, T, MIX_W), lambda ci: (0, ci, idx))

    in_specs = [col(0), col(1), col(2), col(3),
                pl.BlockSpec((b, T, LANES), lambda ci: (0, ci, 0)),
                pl.BlockSpec((b, 8, T), lambda ci: (0, 0, ci)),
                pl.BlockSpec((1, MIX_W), lambda ci: (0, 0))]
    c_in, c_out, c_shapes, c_ops = _cast_specs(casts, grid)
    return pl.pallas_call(
        _with_casts(_mlstm_kernel, len(in_specs), 1, len(casts)),
        out_shape=[jax.ShapeDtypeStruct((b, lp, MIX_W), BF16)] + c_shapes,
        grid=grid,
        in_specs=in_specs + c_in,
        out_specs=[pl.BlockSpec((b, T, MIX_W), lambda ci: (0, ci, 0))] + c_out,
        scratch_shapes=[pltpu.VMEM((b * ML_HEADS, ML_HD, ML_HD), F32),
                        pltpu.VMEM((b * ML_HEADS, 1, ML_HD), F32),
                        pltpu.VMEM((1, LANES), F32)],
        compiler_params=pltpu.CompilerParams(
            dimension_semantics=("arbitrary",), vmem_limit_bytes=VMEM_LIMIT),
        name="mlstm",
    )(proj, proj, proj, proj, gates, gates_t, gnorm.reshape(1, MIX_W), *c_ops)


def _sb_kernel(q_ref, k_ref, v_ref, gq_ref, gk_ref, out_ref, kn_ref, vb_ref, acc_ref, run_ref, qn_ref, tri_ref,
               z_ref, a_ref):
    qi = pl.program_id(2)
    T = SB_BLOCK
    W = SB_SPAN
    row0 = pl.multiple_of(qi * T, T)

    @pl.when(jnp.bitwise_and(qi, 1) == 0)
    def _():
        kn_ref[pl.ds(row0 + T, T), :] = jnp.zeros((T, kn_ref.shape[1]), BF16)
        vb_ref[pl.ds(row0 + T, T), :] = jnp.zeros((T, vb_ref.shape[1]), BF16)

    @pl.when(qi == 0)
    def _():
        jj = jnp.bitwise_and(_iota((2 * W, W), 0), W - 1)
        tri_ref[...] = jnp.where(jj > _iota((2 * W, W), 1), 1.0, 0.0).astype(BF16)

    heads = range(SB_HPS)
    sl = [slice(h * SB_HD, (h + 1) * SB_HD) for h in heads]
    vb_ref[pl.ds(row0, T), :] = v_ref[...].astype(BF16)

    @pl.when(qi == 0)
    def _():
        vb_ref[0:PAD, :] = jnp.zeros((PAD, vb_ref.shape[1]), BF16)

    for h in heads:
        kh = k_ref[:, sl[h]]
        kn = kh * lax.rsqrt(jnp.mean(kh * kh, axis=-1, keepdims=True) + EPS) * gk_ref[...]
        kn_ref[pl.ds(row0, T), sl[h]] = kn.astype(BF16)
        qh = q_ref[:, sl[h]]
        qs = qh * lax.rsqrt(jnp.mean(qh * qh, axis=-1, keepdims=True) + EPS) * (gq_ref[...] * SB_HD ** -0.5)
        qn_ref[:, sl[h]] = qs.astype(BF16)

    acc_ref[...] = jnp.zeros_like(acc_ref)
    run_ref[...] = jnp.zeros_like(run_ref)
    t_pos = row0 + _iota((T, W), 0)
    lane = _iota((T, W), 1)
    top = lax.shift_right_logical(qi, 1)

    def scores(col0, slot):
        for h in heads:
            z_ref[slot, h] = _dot_nt(qn_ref[:, sl[h]], kn_ref[pl.ds(col0, W), sl[h]])

    def weighted_values(col0):
        for h in heads:
            acc_ref[h] += _dot(a_ref[h], vb_ref[pl.ds(col0, W), sl[h]])

    def span_step(it, masked, has_prev):
        col0 = pl.multiple_of((top - it) * W, W)
        slot = it & 1
        z = [z_ref[slot, h] for h in heads]
        if has_prev:
            weighted_values(pl.multiple_of(col0 + W, W))
        l1p = [jnp.log(1.0 + jnp.exp(-jnp.abs(z[h]))) for h in heads]
        ls = [jnp.minimum(z[h], 0.0) - l1p[h] for h in heads]
        lk = [ls[h] - z[h] for h in heads]
        if masked:
            vis = col0 + lane < t_pos
            lk = [jnp.where(vis, lk[h], 0.0) for h in heads]
        lk_hi = [lk[h].astype(BF16) for h in heads]
        lk_lo = [(lk[h] - lk_hi[h].astype(F32)).astype(BF16) for h in heads]
        cs = [_dot(jnp.concatenate([lk_hi[h], lk_lo[h]], axis=1), tri_ref[...]) for h in heads]
        scores(pl.multiple_of(jnp.maximum(top - it - 1, 0) * W, W), 1 - slot)
        a = [jnp.exp(ls[h] + cs[h] + run_ref[h]) for h in heads]
        if masked:
            a = [jnp.where(vis, a[h], 0.0) for h in heads]
        for h in heads:
            a_ref[h] = a[h].astype(BF16)
            run_ref[h] += jnp.sum(lk[h], axis=1, keepdims=True)

    scores(pl.multiple_of(top * W, W), 0)
    span_step(0, True, False)

    def body(it, carry):
        span_step(it, False, True)
        return carry

    lax.fori_loop(1, top + 1, body, 0)
    weighted_values(0)
    for h in heads:
        out_ref[:, sl[h]] = acc_ref[h].astype(out_ref.dtype)


def _stick_breaking(proj, gq, gk, casts=()):
    b, lp, _ = proj.shape
    T = SB_BLOCK
    wid = SB_HPS * SB_HD
    ng = SB_HEADS // SB_HPS
    per = wid // LANES
    grid = (b, ng, lp // T)
    in_specs = [pl.BlockSpec((None, T, wid), lambda bi, gi, qi: (bi, qi, _SBQ // per + gi)),
                pl.BlockSpec((None, T, wid), lambda bi, gi, qi: (bi, qi, _SBK // per + gi)),
                pl.BlockSpec((None, T, wid), lambda bi, gi, qi: (bi, qi, _SBV // per + gi)),
                pl.BlockSpec((1, SB_HD), lambda bi, gi, qi: (0, 0)),
                pl.BlockSpec((1, SB_HD), lambda bi, gi, qi: (0, 0))]
    c_in, c_out, c_shapes, c_ops = _cast_specs(casts, grid)
    return pl.pallas_call(
        _with_casts(_sb_kernel, len(in_specs), 1, len(casts)),
        out_shape=[jax.ShapeDtypeStruct((b, lp, MIX_W), BF16)] + c_shapes,
        grid=grid,
        in_specs=in_specs + c_in,
        out_specs=[pl.BlockSpec((None, T, wid), lambda bi, gi, qi: (bi, qi, gi))] + c_out,
        scratch_shapes=[pltpu.VMEM((lp + SB_SPAN - T, wid), BF16),
                        pltpu.VMEM((lp + SB_SPAN - T, wid), BF16),
                        pltpu.VMEM((SB_HPS, T, SB_HD), F32),
                        pltpu.VMEM((SB_HPS, T, 1), F32),
                        pltpu.VMEM((T, wid), BF16),
                        pltpu.VMEM((2 * SB_SPAN, SB_SPAN), BF16),
                        pltpu.VMEM((2, SB_HPS, T, SB_SPAN), F32),
                        pltpu.VMEM((SB_HPS, T, SB_SPAN), BF16)],
        compiler_params=pltpu.CompilerParams(
            dimension_semantics=("arbitrary", "arbitrary", "arbitrary"), vmem_limit_bytes=VMEM_LIMIT),
        name="stick_breaking",
    )(proj, proj, proj, gq.reshape(1, SB_HD), gk.reshape(1, SB_HD), *c_ops)


def _hgrn2_kernel(q_ref, f_ref, i_ref, g_ref, lb_ref, gn_ref, out_ref, st_ref):
    c = pl.program_id(2)
    T = HG_CHUNK
    S8 = HG_SUB

    @pl.when(c == 0)
    def _():
        st_ref[...] = jnp.zeros_like(st_ref)

    valid = (c * T + _iota((T, 1), 0)) >= PAD
    tril = jnp.where(_iota((T, T), 1) <= _iota((T, T), 0), 1.0, 0.0).astype(BF16)
    lane8 = _iota((S8, T), 1)
    row8 = _iota((S8, T), 0)

    heads = range(HG_HPS)
    hs = [slice(h * HG_HD, (h + 1) * HG_HD) for h in heads]
    lb = [lb_ref[:, hs[h]] for h in heads]
    sig = [_sigmoid(f_ref[:, hs[h]]) for h in heads]
    logf = [jnp.where(valid, jnp.log(lb[h] + (1.0 - lb[h]) * sig[h]), 0.0) for h in heads]
    kk = [jnp.where(valid, (1.0 - lb[h]) * (1.0 - sig[h]), 0.0) for h in heads]
    q = [q_ref[:, hs[h]] * _sigmoid(q_ref[:, hs[h]]) for h in heads]
    iv = [i_ref[:, hs[h]].astype(BF16) for h in heads]
    b = [_tri_left(logf[h], tril) for h in heads]

    st = [st_ref[h] for h in heads]
    b_end = [b[h][T - 1:T] for h in heads]
    o_inter = [_dot_nt((q[h] * jnp.exp(b[h])).astype(BF16), st[h].astype(BF16)) for h in heads]
    st_add = [_dot_tn(iv[h], (kk[h] * jnp.exp(b_end[h] - b[h])).astype(BF16)) for h in heads]

    def diag_block(h, blk):
        r0 = blk * S8
        qb = q[h][r0:r0 + S8]
        bb = b[h][r0:r0 + S8]
        diag = jnp.zeros((S8, T), F32)
        for j in range(S8):
            s_idx = r0 + j
            p = qb * kk[h][s_idx:s_idx + 1] * jnp.exp(jnp.minimum(bb - b[h][s_idx:s_idx + 1], 0.0))
            diag = jnp.where(lane8 == s_idx, jnp.sum(p, axis=1, keepdims=True), diag)
        return jnp.where(lane8 - r0 <= row8, diag, 0.0)

    def off_block(h, blk):
        r0 = blk * S8
        bref = b[h][r0 - 1:r0]
        qs = (q[h][r0:r0 + S8] * jnp.exp(b[h][r0:r0 + S8] - bref)).astype(BF16)
        ks = (kk[h] * jnp.exp(jnp.minimum(bref - b[h], 0.0))).astype(BF16)
        return _dot_nt(qs, ks)

    nblk = T // S8
    off = [[off_block(h, blk) for h in heads] for blk in range(1, nblk)]
    att = []
    for h in heads:
        rows = [diag_block(h, 0)]
        for blk in range(1, nblk):
            rows.append(jnp.where(lane8 < blk * S8, off[blk - 1][h], diag_block(h, blk)))
        att.append(jnp.concatenate(rows, axis=0).astype(BF16))
    o_intra = [_dot(att[h], iv[h]) for h in heads]

    for h in heads:
        o = o_intra[h] + o_inter[h]
        st_ref[h] = st[h] * jnp.exp(b_end[h]) + st_add[h]
        y = o * lax.rsqrt(jnp.mean(o * o, axis=-1, keepdims=True) + EPS) * gn_ref[:, hs[h]]
        out_ref[:, hs[h]] = (y * _sigmoid(g_ref[:, hs[h]])).astype(out_ref.dtype)


def _hgrn2(proj, lb, gnorm, casts=()):
    b, lp, _ = proj.shape
    T = HG_CHUNK
    wid = HG_HPS * HG_HD
    ng = HG_HEADS // HG_HPS
    per = wid // LANES
    grid = (b, ng, lp // T)

    def col(off):
        return pl.BlockSpec((None, T, wid), lambda bi, gi, ci: (bi, ci, off // per + gi))

    vec = pl.BlockSpec((1, wid), lambda bi, gi, ci: (0, gi))
    in_specs = [col(_HGQ), col(_HGF), col(_HGI), col(_HGG), vec, vec]
    c_in, c_out, c_shapes, c_ops = _cast_specs(casts, grid)
    return pl.pallas_call(
        _with_casts(_hgrn2_kernel, len(in_specs), 1, len(casts)),
        out_shape=[jax.ShapeDtypeStruct((b, lp, MIX_W), BF16)] + c_shapes,
        grid=grid,
        in_specs=in_specs + c_in,
        out_specs=[pl.BlockSpec((None, T, wid), lambda bi, gi, ci: (bi, ci, gi))] + c_out,
        scratch_shapes=[pltpu.VMEM((HG_HPS, HG_HD, HG_HD), F32)],
        compiler_params=pltpu.CompilerParams(
            dimension_semantics=("arbitrary", "arbitrary", "arbitrary"), vmem_limit_bytes=VMEM_LIMIT),
        name="hgrn2",
    )(proj, proj, proj, proj, lb.reshape(1, MIX_W), gnorm.reshape(1, MIX_W), *c_ops)


def _rglru_kernel(x_ref, y_ref, cw_ref, cb_ref, w_ref, ba_ref, bx_ref, lam_ref, out_ref, xbuf_ref, h_ref):
    c = pl.program_id(2)
    T = RG_T
    G = RG_GROUP

    @pl.when(c == 0)
    def _():
        xbuf_ref[0:8, :] = jnp.zeros((8, G), F32)
        h_ref[...] = jnp.zeros_like(h_ref)

    valid = (c * T + _iota((T, 1), 0)) >= PAD
    x = jnp.where(valid, x_ref[...], 0.0)
    xbuf_ref[8:8 + T, :] = x
    xc = cb_ref[...] + cw_ref[RG_CONV - 1:RG_CONV, :] * x
    for j in range(1, RG_CONV):
        xc = xc + cw_ref[RG_CONV - 1 - j:RG_CONV - j, :] * xbuf_ref[8 - j:8 - j + T, :]
    xbuf_ref[0:8, :] = x[T - 8:T]

    pre = _dot(xc.astype(BF16), w_ref[...])
    r = _sigmoid(pre[:, :G] + ba_ref[...])
    ig = _sigmoid(pre[:, G:] + bx_ref[...])
    lam = lam_ref[...]
    softplus_neg = jnp.maximum(-lam, 0.0) + jnp.log1p(jnp.exp(-jnp.abs(lam)))
    log_a = -RG_C * r * softplus_neg
    a = jnp.exp(log_a)
    th = jnp.tanh(log_a)
    u = jnp.where(valid, jnp.sqrt(-2.0 * th / (1.0 - th)) * (ig * xc), 0.0)

    row = _iota((T, G), 0)
    shift = 1
    while shift < T:
        keep = row >= shift
        u_s = pltpu.roll(u, shift, axis=0)
        a_s = pltpu.roll(a, shift, axis=0)
        u = jnp.where(keep, a * u_s + u, u)
        a = jnp.where(keep, a * a_s, a)
        shift *= 2
    hseq = a * h_ref[...] + u
    h_ref[...] = hseq[T - 1:T]

    y = y_ref[...]
    gelu = 0.5 * y * (1.0 + jnp.tanh(0.7978845608028654 * (y + 0.044715 * (y * y * y))))
    out_ref[...] = (hseq * gelu).astype(out_ref.dtype)


def _rglru(proj, conv_w, conv_b, w_bd, ba, bx, lam, casts=()):
    b, lp, _ = proj.shape
    T = RG_T
    G = RG_GROUP
    ng = MIX_W // G
    per = G // LANES
    grid = (b, ng, lp // T)

    def col(off):
        return pl.BlockSpec((None, T, G), lambda bi, gi, ci: (bi, ci, off // per + gi))

    vec = pl.BlockSpec((1, G), lambda bi, gi, ci: (0, gi))
    in_specs = [col(_RGX), col(_RGY),
                pl.BlockSpec((RG_CONV, G), lambda bi, gi, ci: (0, gi)),
                vec,
                pl.BlockSpec((None, G, 2 * G), lambda bi, gi, ci: (gi, 0, 0)),
                vec, vec, vec]
    c_in, c_out, c_shapes, c_ops = _cast_specs(casts, grid)
    return pl.pallas_call(
        _with_casts(_rglru_kernel, len(in_specs), 1, len(casts)),
        out_shape=[jax.ShapeDtypeStruct((b, lp, MIX_W), BF16)] + c_shapes,
        grid=grid,
        in_specs=in_specs + c_in,
        out_specs=[pl.BlockSpec((None, T, G), lambda bi, gi, ci: (bi, ci, gi))] + c_out,
        scratch_shapes=[pltpu.VMEM((T + 8, G), F32), pltpu.VMEM((1, G), F32)],
        compiler_params=pltpu.CompilerParams(
            dimension_semantics=("arbitrary", "arbitrary", "arbitrary"), vmem_limit_bytes=VMEM_LIMIT),
        name="rglru",
    )(proj, proj, conv_w, conv_b.reshape(1, MIX_W), w_bd, ba.reshape(1, MIX_W), bx.reshape(1, MIX_W),
      lam.reshape(1, MIX_W), *c_ops)


def _block_diag_gates(wa, wx):
    per = RG_GROUP // RG_BD
    ng = RG_BLOCKS // per
    eye = jnp.eye(per, dtype=F32)

    def bd(w):
        w4 = w.reshape(ng, per, RG_BD, RG_BD)
        return jnp.einsum("gaij,ab->gaibj", w4, eye).reshape(ng, RG_GROUP, RG_GROUP)

    return jnp.concatenate([bd(wa), bd(wx)], axis=-1).astype(BF16)


def _assemble_kernel(x_ref, front_ref, g_ref, o_ref, n_ref):
    r = pl.program_id(1)

    def emit(rows):
        o_ref[...] = rows
        y = rows * lax.rsqrt(jnp.mean(rows * rows, axis=-1, keepdims=True) + EPS)
        n_ref[...] = (y * g_ref[...]).astype(n_ref.dtype)

    @pl.when(r == 0)
    def _():
        emit(front_ref[...])

    @pl.when(r > 0)
    def _():
        emit(x_ref[...])


def _assemble(x, front, gain, casts=()):
    b, seq, d = x.shape
    tr = PAD + N_META
    grid = (b, 1 + seq // tr)
    in_specs = [pl.BlockSpec((None, tr, d), lambda bi, r: (bi, jnp.maximum(r - 1, 0), 0)),
                pl.BlockSpec((tr, d), lambda bi, r: (0, 0)),
                pl.BlockSpec((1, d), lambda bi, r: (0, 0))]
    row_spec = pl.BlockSpec((None, tr, d), lambda bi, r: (bi, r, 0))
    c_in, c_out, c_shapes, c_ops = _cast_specs(casts, grid)
    return pl.pallas_call(
        _with_casts(_assemble_kernel, len(in_specs), 2, len(casts)),
        out_shape=[jax.ShapeDtypeStruct((b, tr + seq, d), x.dtype),
                   jax.ShapeDtypeStruct((b, tr + seq, d), BF16)] + c_shapes,
        grid=grid,
        in_specs=in_specs + c_in,
        out_specs=[row_spec, row_spec] + c_out,
        compiler_params=pltpu.CompilerParams(
            dimension_semantics=("arbitrary", "arbitrary"), vmem_limit_bytes=VMEM_LIMIT),
        name="assemble",
    )(x, front, gain.reshape(1, d), *c_ops)


def _layer(h, xn, l, wt_in_bf, p, lower_bounds, tiles):
    b, lp = p["b"], p["lp"]
    m = b * lp
    tm = tiles["tm"]

    depth = p["wt_in"].shape[0]
    b_in = p["b_in"][l]
    if_lo, if_hi, gate_lo = _IN_SPLITS
    wt_ml, wt_rest, wt_gate = wt_in_bf
    wt_if = jnp.pad(p["wt_in"][l, if_lo:if_hi], ((0, LANES - 2 * ML_HEADS), (0, 0)))
    b_if = jnp.pad(b_in[if_lo:if_hi], (0, LANES - 2 * ML_HEADS)).reshape(1, LANES)

    if xn is None:
        xn = _rmsnorm(h, p["norm_mix"][l])
    proj_ml = _linear_bias(xn, wt_ml, b_in[:N_ML].reshape(1, N_ML), tm, tiles["tn_mix"], "in_proj_ml")
    proj_rest = _linear_bias(xn, wt_rest, b_in[if_hi:gate_lo].reshape(1, N_REST), tm, tiles["tn_mix"],
                             "in_proj_rest")
    proj_ml = proj_ml.reshape(b, lp, N_ML)
    proj_rest = proj_rest.reshape(b, lp, N_REST)
    gates = _linear_bias(xn, wt_if, b_if, tm, LANES, "in_proj_gates").reshape(b, lp, LANES)
    gates_t = jnp.swapaxes(gates[:, :, :8], 1, 2)

    w_up_stack = p["w_up"].reshape(depth, N_BRANCH * MIX_W, D_MODEL)
    sb_casts = []
    if l + 1 < depth:
        sb_casts = [(p["wt_in"], l + 1, 64, 0, N_ML),
                    (p["wt_in"], l + 1, 144, if_hi, N_REST),
                    (p["wt_in"], l + 1, 256, gate_lo, N_BRANCH * D_MODEL)]
    ml_out, w_out_bf, w_up_bf = _mlstm(proj_ml, gates, gates_t, p["ml_norm"][l],
                                       casts=[(p["w_out"], l, 128), (w_up_stack, l, 128)])
    sb_out, *w_in_next = _stick_breaking(proj_rest, p["sb_q_norm"][l], p["sb_k_norm"][l], casts=sb_casts)
    hg_out, w_fgate_bf, w_down_bf = _hgrn2(proj_rest, lower_bounds[l], p["hg_norm"][l],
                                           casts=[(p["w_ffn_gate"], l, 32), (p["w_ffn_down"], l, 128)])
    rg_out, w_fup_bf = _rglru(proj_rest, p["rg_conv_w"][l], p["rg_conv_b"][l],
                              _block_diag_gates(p["rg_wa"][l], p["rg_wx"][l]),
                              p["rg_ba"][l], p["rg_bx"][l], p["rg_lambda"][l], casts=[(p["w_ffn_up"], l, 32)])
    branches = tuple(br.reshape(m, MIX_W) for br in (ml_out, sb_out, hg_out, rg_out))
    merged = _merge(xn, branches, wt_gate, b_in[gate_lo:].reshape(1, N_BRANCH * D_MODEL),
                    w_up_bf.reshape(1, N_BRANCH, MIX_W, D_MODEL), 0, tiles["tm_merge"], tiles["tn_merge"])
    h = _linear_res(merged, w_out_bf[None], 0, h, tm, tiles["tn_out"], D_MODEL, "out_proj")

    d_ff = p["w_ffn_gate"].shape[-1]
    hn = _rmsnorm(h, p["norm_ffn"][l])
    act = _ffn_gate_up(hn, w_fgate_bf[None], w_fup_bf[None], 0, tiles["tm_ff"], tiles["tn_ff"])
    if l + 1 == depth:
        out = _linear_res_tokens(act.reshape(b, lp, d_ff), w_down_bf[None], h.reshape(b, lp, D_MODEL),
                                 PAD + N_META, tiles["tm_last"], tiles["tn_down"], "ffn_down_out")
        return out, None
    h = _linear_res(act, w_down_bf[None], 0, h, tiles["tm_down"], tiles["tn_down"], d_ff // tiles["nk_down"],
                    "ffn_down")
    return h, tuple(w_in_next)


def _forward(x, meta, params, tiles):
    b, seq, d = x.shape
    lp = PAD + N_META + seq
    depth = params["w_in"].shape[0]
    hg_lb = params["hg_lb"]
    p_lb = jax.nn.softmax(hg_lb.astype(F32), axis=0)
    lower_bounds = jnp.clip(jnp.cumsum(p_lb, axis=0) - p_lb[0:1], 0.0, 0.999)

    front = jnp.concatenate([jnp.zeros((PAD, d), x.dtype), meta.astype(x.dtype)], axis=0)
    p = dict(params, b=b, lp=lp, wt_in=jnp.swapaxes(params["w_in"], 1, 2))
    if_lo, if_hi, gate_lo = _IN_SPLITS
    h, xn, *wt_in_bf = _assemble(x, front, params["norm_mix"][0],
                                 casts=[(p["wt_in"], 0, 64, 0, N_ML),
                                        (p["wt_in"], 0, 144, if_hi, N_REST),
                                        (p["wt_in"], 0, 256, gate_lo, N_BRANCH * D_MODEL)])
    h = h.reshape(b * lp, d)
    xn = xn.reshape(b * lp, d)
    for l in range(depth):
        h, wt_in_bf = _layer(h, xn, l, wt_in_bf, p, lower_bounds, tiles)
        xn = None
    return h


_TILES = dict(tm=1056, tn_mix=1024, tm_merge=768, tn_merge=256, tn_out=512,
              tm_ff=2112, tn_ff=256, tm_down=768, tn_down=256, nk_down=1, tm_last=512)


def kernel(x, meta, norm_mix, norm_ffn, w_in, b_in, ml_norm, sb_q_norm, sb_k_norm, hg_lb, hg_norm,
           rg_conv_w, rg_conv_b, rg_wa, rg_ba, rg_wx, rg_bx, rg_lambda, w_up, w_out,
           w_ffn_gate, w_ffn_up, w_ffn_down):
    params = dict(norm_mix=norm_mix, norm_ffn=norm_ffn, w_in=w_in, b_in=b_in, ml_norm=ml_norm,
                  sb_q_norm=sb_q_norm, sb_k_norm=sb_k_norm, hg_lb=hg_lb, hg_norm=hg_norm,
                  rg_conv_w=rg_conv_w, rg_conv_b=rg_conv_b, rg_wa=rg_wa, rg_ba=rg_ba, rg_wx=rg_wx,
                  rg_bx=rg_bx, rg_lambda=rg_lambda, w_up=w_up, w_out=w_out,
                  w_ffn_gate=w_ffn_gate, w_ffn_up=w_ffn_up, w_ffn_down=w_ffn_down)
    return _forward(x, meta, params, _TILES)
```
